```python
import math
import jax
import jax.numpy as jnp
from jax import lax
import numpy as np

D_MODEL = 1024
BATCH = 4
SEQ = 4096
DEPTH = 4
DEC_BATCH = 32
DEC_SEQ = 1
PAST_LEN = 8192
PAGE_SIZE = 128

HEAD_DIM = 64
N_MIXERS = 3
D_FF = 2816
RMS_EPS = 1e-6
NEG_INF = -1e30

NUM_BUCKETS = 32
REL_MAX_DIST = 2048
BIAS_HEADS = 12

A_GROUPS = ((128, 1), (512, 4), (2048, 16))
A_GROUP_HEADS = 4
A_HEADS = A_GROUP_HEADS * len(A_GROUPS)

B_HEADS = 12
B_KV_HEADS = 4
B_REP = B_HEADS // B_KV_HEADS
B_CMP_BLOCK = 32
B_CMP_STRIDE = 16
B_SEL_BLOCK = 64
B_TOPN = 16
B_WINDOW = 512
B_PHI_HIDDEN = 128
B_QBLOCK = 64
B_FORCE = 1e4
B_IN_COLS = B_HEADS * HEAD_DIM + 6 * B_KV_HEADS * HEAD_DIM + 3 * B_HEADS

C_HEADS = 16
C_QBLOCK = 128
C_IN_COLS = 3 * C_HEADS * HEAD_DIM + C_HEADS
FORGET_OFFSET = 3.0

N_A_LAYERS = len(range(0, DEPTH, N_MIXERS))
N_B_LAYERS = len(range(1, DEPTH, N_MIXERS))
N_C_LAYERS = len(range(2, DEPTH, N_MIXERS))

kernel_name = 'hybrid_dilated_nsa_fox_step'


def rms_norm(x, g):
    xf = x.astype(jnp.float32)
    y = xf * lax.rsqrt(jnp.mean(xf * xf, axis=-1, keepdims=True) + RMS_EPS)
    return (y * g.astype(jnp.float32)).astype(x.dtype)


def macaron_half(x, g, w_gate, w_up, w_down):
    h = rms_norm(x, g)
    return x + 0.5 * ((jax.nn.silu(h @ w_gate) * (h @ w_up)) @ w_down)


def rel_bucket(dist):
    exact = NUM_BUCKETS // 2
    d = jnp.maximum(dist, 0)
    logd = jnp.log(jnp.maximum(d, 1).astype(jnp.float32) / exact) / math.log(REL_MAX_DIST / exact)
    far = jnp.minimum(exact + (logd * (NUM_BUCKETS - exact)).astype(jnp.int32), NUM_BUCKETS - 1)
    return jnp.where(d < exact, d, far)


def rel_bias(table, dist):
    return table.astype(jnp.float32)[rel_bucket(dist)]


def masked_softmax(s, mask, axes):
    s = jnp.where(mask, s, NEG_INF)
    m = jnp.max(s, axis=axes, keepdims=True)
    p = jnp.where(mask, jnp.exp(s - m), 0.0)
    den = jnp.sum(p, axis=axes, keepdims=True)
    den = jnp.where(den > 0, den, 1.0)
    return p / den, m + jnp.log(den)


def band_attention(q, k, v, band, bias):
    n, lp, nh, dh = q.shape
    nb = lp // band

    def with_prev(t):
        tb = t.reshape(n, nb, band, nh, dh)
        prev = jnp.pad(tb, ((0, 0), (1, 0), (0, 0), (0, 0), (0, 0)))[:, :-1]
        return jnp.concatenate([prev, tb], axis=2)

    kb, vb = with_prev(k), with_prev(v)
    s = jnp.einsum('nbqhd,nbkhd->nbhqk', q.reshape(n, nb, band, nh, dh), kb,
                   preferred_element_type=jnp.float32) * dh ** -0.5
    off = jnp.arange(band)[:, None] + band - jnp.arange(2 * band)[None, :]
    kidx = (jnp.arange(nb)[:, None] - 1) * band + jnp.arange(2 * band)[None, :]
    s = s + bias[jnp.clip(off, 0, band)].transpose(2, 0, 1)
    mask = ((off >= 0) & (off <= band))[None, None, None] & (kidx >= 0)[None, :, None, None, :]
    p, lse = masked_softmax(s, mask, -1)
    o = jnp.einsum('nbhqk,nbkhd->nbqhd', p, vb).reshape(n, lp, nh, dh)
    return o, lse[..., 0].transpose(0, 1, 3, 2).reshape(n, lp, nh)


def combine_groups(outs, lses, w_out, dtype):
    alpha = jax.nn.softmax(jnp.stack(lses, axis=2), axis=2)
    o = jnp.stack(outs, axis=2) * alpha[..., None]
    n, l = o.shape[:2]
    return o.reshape(n, l, A_HEADS * HEAD_DIM).astype(dtype) @ w_out


def dilated_prompt(h, w_in, w_out, table):
    nb_, s_, _ = h.shape
    qkv = (h @ w_in).reshape(nb_, s_, 3, len(A_GROUPS), A_GROUP_HEADS, HEAD_DIM)
    outs, lses, bufs = [], [], []
    for g, (win, dil) in enumerate(A_GROUPS):
        band = win // dil
        sub = s_ // dil
        pad = -(-sub // band) * band - sub

        def by_residue(t):
            t = t.reshape(nb_, sub, dil, A_GROUP_HEADS, HEAD_DIM).transpose(0, 2, 1, 3, 4)
            t = t.reshape(nb_ * dil, sub, A_GROUP_HEADS, HEAD_DIM)
            return jnp.pad(t, ((0, 0), (0, pad), (0, 0), (0, 0)))

        q, k, v = [by_residue(qkv[:, :, c, g]) for c in range(3)]
        bias = rel_bias(table[:, g * A_GROUP_HEADS:(g + 1) * A_GROUP_HEADS], jnp.arange(band + 1) * dil)
        o, lse = band_attention(q, k, v, band, bias)
        outs.append(o[:, :sub].reshape(nb_, dil, sub, A_GROUP_HEADS, HEAD_DIM)
                    .transpose(0, 2, 1, 3, 4).reshape(nb_, s_, A_GROUP_HEADS, HEAD_DIM))
        lses.append(lse[:, :sub].reshape(nb_, dil, sub, A_GROUP_HEADS)
                    .transpose(0, 2, 1, 3).reshape(nb_, s_, A_GROUP_HEADS))
        keep = min(win, s_)
        bufs.append(qkv[:, s_ - keep:, 1:3, g])
    y = combine_groups(outs, lses, w_out, h.dtype)
    return (y, *bufs)


def dilated_step(h, bufs, w_in, w_out, table):
    nb_, t_, _ = h.shape
    qkv = (h @ w_in).reshape(nb_, t_, 3, len(A_GROUPS), A_GROUP_HEADS, HEAD_DIM)
    outs, lses, new_bufs = [], [], []
    for g, ((win, dil), buf) in enumerate(zip(A_GROUPS, bufs)):
        band = win // dil
        wb = buf.shape[1]
        kv = jnp.concatenate([buf, qkv[:, :, 1:3, g]], axis=1)
        j = jnp.arange(band + 1)
        idx = (wb + jnp.arange(t_))[:, None] - j[None, :] * dil
        kvg = kv[:, jnp.maximum(idx, 0)]
        s = jnp.einsum('bthd,btjhd->bthj', qkv[:, :, 0, g], kvg[:, :, :, 0],
                       preferred_element_type=jnp.float32) * HEAD_DIM ** -0.5
        s = s + rel_bias(table[:, g * A_GROUP_HEADS:(g + 1) * A_GROUP_HEADS], j * dil).T
        p, lse = masked_softmax(s, (idx >= 0)[None, :, None, :], -1)
        outs.append(jnp.einsum('bthj,btjhd->bthd', p, kvg[:, :, :, 1]))
        lses.append(lse[..., 0])
        keep = min(win, wb + t_)
        new_bufs.append(kv[:, wb + t_ - keep:])
    y = combine_groups(outs, lses, w_out, h.dtype)
    return (y, *new_bufs)


def nsa_split(h, w_in, gate_bias):
    n, l, _ = h.shape
    proj = h @ w_in
    nq = B_HEADS * HEAD_DIM
    nkv = 6 * B_KV_HEADS * HEAD_DIM
    q = proj[..., :nq].reshape(n, l, B_HEADS, HEAD_DIM)
    kvs = proj[..., nq:nq + nkv].reshape(n, l, 6, B_KV_HEADS, HEAD_DIM)
    gate = jax.nn.sigmoid((proj[..., nq + nkv:] + gate_bias).astype(jnp.float32)).reshape(n, l, 3, B_HEADS)
    return q, kvs, gate


def nsa_compress(k, pos, w1, w2):
    n, l = k.shape[:2]
    half = B_CMP_BLOCK // B_CMP_STRIDE
    n_chunks = l // B_CMP_STRIDE
    nc = n_chunks - half + 1
    c = k[:, :n_chunks * B_CMP_STRIDE].reshape(n, n_chunks, B_CMP_STRIDE, B_KV_HEADS, HEAD_DIM)
    w1r = w1.reshape(half, B_CMP_STRIDE, HEAD_DIM, B_PHI_HIDDEN)
    pos_term = jnp.einsum('psd,psde->pe', pos.reshape(half, B_CMP_STRIDE, HEAD_DIM), w1r)
    proj = jnp.einsum('ncsgd,psde->pncge', c, w1r) + pos_term[:, None, None, None, :]
    pre = sum(proj[p, :, p:p + nc] for p in range(half))
    return jnp.einsum('ncge,ed->ncgd', jax.nn.gelu(pre), w2)


def nsa_sel_blocks(k):
    n, l = k.shape[:2]
    ns = -(-l // B_SEL_BLOCK)
    k = jnp.pad(k, ((0, 0), (0, ns * B_SEL_BLOCK - l), (0, 0), (0, 0)))
    return k.reshape(n, ns, B_SEL_BLOCK, B_KV_HEADS, HEAD_DIM).transpose(0, 3, 1, 2, 4)


def nsa_attend(q, tq, kc, vc, ks, vs, kw, vw, pw, gate, table):
    n, nq_ = q.shape[:2]
    qg = q.reshape(n, nq_, B_KV_HEADS, B_REP, HEAD_DIM)
    scale = HEAD_DIM ** -0.5

    def grouped(b):
        return b.reshape(b.shape[0], b.shape[1], B_KV_HEADS, B_REP).transpose(2, 3, 0, 1)

    nc = kc.shape[1]
    c_end = jnp.arange(nc) * B_CMP_STRIDE + B_CMP_BLOCK - 1
    dist_c = tq[:, None] - c_end[None, :]
    s = jnp.einsum('nqgrd,ncgd->ngrqc', qg, kc, preferred_element_type=jnp.float32) * scale
    p_c, _ = masked_softmax(s + grouped(rel_bias(table, dist_c)), dist_c >= 0, -1)
    o_c = jnp.einsum('ngrqc,ncgd->nqgrd', p_c, vc)

    ns = ks.shape[2]
    c_start = jnp.arange(nc) * B_CMP_STRIDE
    s_start = jnp.arange(ns) * B_SEL_BLOCK
    overlap = ((c_start[:, None] < s_start[None, :] + B_SEL_BLOCK)
               & (c_start[:, None] + B_CMP_BLOCK > s_start[None, :])).astype(jnp.float32)
    imp = jnp.einsum('ngrqc,cs->ngqs', p_c, overlap)
    blk = jnp.arange(ns)[None, :]
    cur = (tq // B_SEL_BLOCK)[:, None]
    forced = (blk == 0) | (blk == cur) | (blk == cur - 1)
    imp = jnp.where(forced, B_FORCE, jnp.where(blk <= cur, imp, NEG_INF))
    _, sel = lax.top_k(imp, min(B_TOPN, ns))
    bi = jnp.arange(n)[:, None, None, None]
    gi = jnp.arange(B_KV_HEADS)[None, :, None, None]
    kg, vg = ks[bi, gi, sel], vs[bi, gi, sel]
    dist_s = tq[:, None, None] - (sel[..., None] * B_SEL_BLOCK + jnp.arange(B_SEL_BLOCK))
    tab = table.astype(jnp.float32).reshape(NUM_BUCKETS, B_KV_HEADS, B_REP).transpose(1, 0, 2)
    bias_s = tab[gi[..., None], rel_bucket(dist_s)].transpose(0, 1, 5, 2, 3, 4)
    s = jnp.einsum('nqgrd,ngqksd->ngrqks', qg, kg, preferred_element_type=jnp.float32) * scale
    p_s, _ = masked_softmax(s + bias_s, (dist_s >= 0)[:, :, None], (-2, -1))
    o_s = jnp.einsum('ngrqks,ngqksd->nqgrd', p_s, vg)

    dist_w = tq[:, None] - pw[None, :]
    mask_w = (dist_w >= 0) & (dist_w < B_WINDOW) & (pw >= 0)[None, :]
    s = jnp.einsum('nqgrd,nwgd->ngrqw', qg, kw, preferred_element_type=jnp.float32) * scale
    p_w, _ = masked_softmax(s + grouped(rel_bias(table, dist_w)), mask_w, -1)
    o_w = jnp.einsum('ngrqw,nwgd->nqgrd', p_w, vw)

    gate = gate.reshape(n, nq_, 3, B_KV_HEADS, B_REP, 1)
    o = gate[:, :, 0] * o_c + gate[:, :, 1] * o_s + gate[:, :, 2] * o_w
    return o.reshape(n, nq_, B_HEADS * HEAD_DIM)


def nsa_prompt(h, w_in, gate_bias, phi_pos, phi_w1, phi_w2, w_out, table):
    n, l, _ = h.shape
    q, kvs, gate = nsa_split(h, w_in, gate_bias)
    kc = nsa_compress(kvs[:, :, 0], phi_pos[0], phi_w1[0], phi_w2[0])
    vc = nsa_compress(kvs[:, :, 1], phi_pos[1], phi_w1[1], phi_w2[1])
    ks, vs = nsa_sel_blocks(kvs[:, :, 2]), nsa_sel_blocks(kvs[:, :, 3])
    win_pad = jnp.pad(kvs[:, :, 4:6], ((0, 0), (B_WINDOW, 0), (0, 0), (0, 0), (0, 0)))
    nq = l // B_QBLOCK
    qb = q.reshape(n, nq, B_QBLOCK, B_HEADS, HEAD_DIM).transpose(1, 0, 2, 3, 4)
    gb = gate.reshape(n, nq, B_QBLOCK, 3, B_HEADS).transpose(1, 0, 2, 3, 4)

    def block(args):
        i, q_i, g_i = args
        s0 = i * B_QBLOCK
        w_i = lax.dynamic_slice_in_dim(win_pad, s0, B_WINDOW + B_QBLOCK, axis=1)
        pw = s0 - B_WINDOW + jnp.arange(B_WINDOW + B_QBLOCK)
        return nsa_attend(q_i, s0 + jnp.arange(B_QBLOCK), kc, vc, ks, vs,
                          w_i[:, :, 0], w_i[:, :, 1], pw, g_i, table)

    o = lax.map(block, (jnp.arange(nq), qb, gb))
    y = o.transpose(1, 0, 2, 3).reshape(n, l, B_HEADS * HEAD_DIM).astype(h.dtype) @ w_out
    keep = min(B_WINDOW, l)
    return (y, kvs[:, :, 0:2], kvs[:, :, 2:4], kvs[:, l - keep:, 4:6])


def nsa_step(h, cache_cmp, cache_sel, cache_win, page_table, w_in, gate_bias, phi_pos, phi_w1, phi_w2, w_out, table):
    n, t_, _ = h.shape
    q, kvs, gate = nsa_split(h, w_in, gate_bias)
    past = page_table.shape[1] * cache_cmp.shape[1]

    def full(pool, new):
        rows = pool[page_table].reshape(n, past, 2, B_KV_HEADS, HEAD_DIM)
        return jnp.concatenate([rows, new], axis=1)

    cmp_all, sel_all = full(cache_cmp, kvs[:, :, 0:2]), full(cache_sel, kvs[:, :, 2:4])
    kc = nsa_compress(cmp_all[:, :, 0], phi_pos[0], phi_w1[0], phi_w2[0])
    vc = nsa_compress(cmp_all[:, :, 1], phi_pos[1], phi_w1[1], phi_w2[1])
    ks, vs = nsa_sel_blocks(sel_all[:, :, 0]), nsa_sel_blocks(sel_all[:, :, 1])
    wb = cache_win.shape[1]
    win_all = jnp.concatenate([cache_win, kvs[:, :, 4:6]], axis=1)
    pw = past - wb + jnp.arange(wb + t_)
    o = nsa_attend(q, past + jnp.arange(t_), kc, vc, ks, vs,
                   win_all[:, :, 0], win_all[:, :, 1], pw, gate, table)
    y = o.astype(h.dtype) @ w_out
    keep = min(B_WINDOW, wb + t_)
    return (y, kvs[:, :, 0:2], kvs[:, :, 2:4], win_all[:, wb + t_ - keep:])


def fox_split(h, w_in, forget_bias):
    n, l, _ = h.shape
    proj = h @ w_in
    w = C_HEADS * HEAD_DIM
    q, k, v = [proj[..., c * w:(c + 1) * w].reshape(n, l, C_HEADS, HEAD_DIM) for c in range(3)]
    logf = jax.nn.log_sigmoid((proj[..., 3 * w:] + forget_bias).astype(jnp.float32))
    return q, k, v, logf


def fox_prompt(h, w_in, forget_bias, w_out):
    n, l, _ = h.shape
    q, k, v, logf = fox_split(h, w_in, forget_bias)
    dcum = jnp.cumsum(logf, axis=1).transpose(0, 2, 1)
    nq = l // C_QBLOCK
    qb = q.reshape(n, nq, C_QBLOCK, C_HEADS, HEAD_DIM).transpose(1, 0, 2, 3, 4)
    kpos = jnp.arange(l)

    def block(args):
        i, q_i = args
        tq = i * C_QBLOCK + jnp.arange(C_QBLOCK)
        d_q = lax.dynamic_slice_in_dim(dcum, i * C_QBLOCK, C_QBLOCK, axis=2)
        s = jnp.einsum('nqhd,nkhd->nhqk', q_i, k, preferred_element_type=jnp.float32) * HEAD_DIM ** -0.5
        s = s + d_q[..., None] - dcum[:, :, None, :]
        p, _ = masked_softmax(s, kpos[None, :] <= tq[:, None], -1)
        return jnp.einsum('nhqk,nkhd->nqhd', p, v)

    o = lax.map(block, (jnp.arange(nq), qb))
    y = o.transpose(1, 0, 2, 3, 4).reshape(n, l, C_HEADS * HEAD_DIM).astype(h.dtype) @ w_out
    return (y, k, v, logf.astype(h.dtype))


def fox_step(h, cache_k, cache_v, cache_logf, page_table, w_in, forget_bias, w_out):
    n, t_, _ = h.shape
    q, k, v, logf = fox_split(h, w_in, forget_bias)
    past = page_table.shape[1] * cache_k.shape[1]
    k_past = cache_k[page_table].reshape(n, past, C_HEADS, HEAD_DIM)
    v_past = cache_v[page_table].reshape(n, past, C_HEADS, HEAD_DIM)
    logf_all = jnp.concatenate(
        [cache_logf[page_table].reshape(n, past, C_HEADS).astype(jnp.float32), logf], axis=1)
    dcum = jnp.cumsum(logf_all, axis=1).transpose(0, 2, 1)
    s = jnp.concatenate([
        jnp.einsum('nqhd,nkhd->nhqk', q, k_past, preferred_element_type=jnp.float32),
        jnp.einsum('nqhd,nkhd->nhqk', q, k, preferred_element_type=jnp.float32)], axis=-1) * HEAD_DIM ** -0.5
    s = s + dcum[:, :, past:, None] - dcum[:, :, None, :]
    tq = past + jnp.arange(t_)
    kpos = jnp.arange(past + t_)
    p, _ = masked_softmax(s, kpos[None, :] <= tq[:, None], -1)
    o = (jnp.einsum('nhqk,nkhd->nqhd', p[..., :past], v_past)
         + jnp.einsum('nhqk,nkhd->nqhd', p[..., past:], v))
    y = o.reshape(n, t_, C_HEADS * HEAD_DIM).astype(h.dtype) @ w_out
    return (y, k, v, logf.astype(h.dtype))


def setup_inputs(seed: int = 0) -> dict:
    key = jax.random.key(seed)
    keys = iter(jax.random.split(key, 48))

    def nrm(shape, scale=1.0):
        return jax.random.normal(next(keys), shape, jnp.float32) * scale

    n_pages = PAST_LEN // PAGE_SIZE
    n_used = DEC_BATCH * n_pages
    n_pool = n_used + max(1, n_used // 4)
    hd = HEAD_DIM
    x_prompt = nrm((BATCH, SEQ, D_MODEL))
    x_sample = nrm((DEC_BATCH, DEC_SEQ, D_MODEL))
    cache_l0_w128, cache_l0_w512, cache_l0_w2048 = [
        nrm((DEC_BATCH, min(w, PAST_LEN), 2, A_GROUP_HEADS, hd)) for w, _ in A_GROUPS]
    cache_l1_cmp = nrm((n_pool, PAGE_SIZE, 2, B_KV_HEADS, hd))
    cache_l1_sel = nrm((n_pool, PAGE_SIZE, 2, B_KV_HEADS, hd))
    cache_l1_win = nrm((DEC_BATCH, min(B_WINDOW, PAST_LEN), 2, B_KV_HEADS, hd))
    cache_l2_k = nrm((n_pool, PAGE_SIZE, C_HEADS, hd))
    cache_l2_v = nrm((n_pool, PAGE_SIZE, C_HEADS, hd))
    cache_l2_logf = jax.nn.log_sigmoid(FORGET_OFFSET + nrm((n_pool, PAGE_SIZE, C_HEADS)))
    cache_l3_w128, cache_l3_w512, cache_l3_w2048 = [
        nrm((DEC_BATCH, min(w, PAST_LEN), 2, A_GROUP_HEADS, hd)) for w, _ in A_GROUPS]
    page_table = jax.random.permutation(next(keys), n_pool)[:n_used].reshape(DEC_BATCH, n_pages).astype(jnp.int32)

    norm_g = 1.0 + nrm((DEPTH, 3, D_MODEL), 0.05)
    ffn_w_gate = nrm((DEPTH, 2, D_MODEL, D_FF), D_MODEL ** -0.5)
    ffn_w_up = nrm((DEPTH, 2, D_MODEL, D_FF), D_MODEL ** -0.5)
    ffn_w_down = nrm((DEPTH, 2, D_FF, D_MODEL), D_FF ** -0.5)
    final_norm_g = 1.0 + nrm((D_MODEL,), 0.05)
    rel_bias_table = nrm((NUM_BUCKETS, BIAS_HEADS), 0.5)
    a_w_in = nrm((N_A_LAYERS, D_MODEL, 3 * A_HEADS * hd), D_MODEL ** -0.5)
    a_w_out = nrm((N_A_LAYERS, A_HEADS * hd, D_MODEL), (A_HEADS * hd) ** -0.5)
    b_w_in = nrm((N_B_LAYERS, D_MODEL, B_IN_COLS), D_MODEL ** -0.5)
    b_gate_bias = nrm((N_B_LAYERS, 3 * B_HEADS), 0.1)
    b_phi_pos = nrm((N_B_LAYERS, 2, B_CMP_BLOCK, hd), 0.2)
    b_phi_w1 = nrm((N_B_LAYERS, 2, B_CMP_BLOCK, hd, B_PHI_HIDDEN), (B_CMP_BLOCK * hd) ** -0.5)
    b_phi_w2 = nrm((N_B_LAYERS, 2, B_PHI_HIDDEN, hd), B_PHI_HIDDEN ** -0.5)
    b_w_out = nrm((N_B_LAYERS, B_HEADS * hd, D_MODEL), (B_HEADS * hd) ** -0.5)
    c_w_in = nrm((N_C_LAYERS, D_MODEL, C_IN_COLS), D_MODEL ** -0.5)
    c_forget_bias = FORGET_OFFSET + nrm((N_C_LAYERS, C_HEADS), 0.5)
    c_w_out = nrm((N_C_LAYERS, C_HEADS * hd, D_MODEL), (C_HEADS * hd) ** -0.5)
    return {
        'x_prompt': x_prompt, 'x_sample': x_sample,
        'cache_l0_w128': cache_l0_w128, 'cache_l0_w512': cache_l0_w512, 'cache_l0_w2048': cache_l0_w2048,
        'cache_l1_cmp': cache_l1_cmp, 'cache_l1_sel': cache_l1_sel, 'cache_l1_win': cache_l1_win,
        'cache_l2_k': cache_l2_k, 'cache_l2_v': cache_l2_v, 'cache_l2_logf': cache_l2_logf,
        'cache_l3_w128': cache_l3_w128, 'cache_l3_w512': cache_l3_w512, 'cache_l3_w2048': cache_l3_w2048,
        'page_table': page_table,
        'norm_g': norm_g, 'ffn_w_gate': ffn_w_gate, 'ffn_w_up': ffn_w_up, 'ffn_w_down': ffn_w_down,
        'final_norm_g': final_norm_g, 'rel_bias_table': rel_bias_table,
        'a_w_in': a_w_in, 'a_w_out': a_w_out,
        'b_w_in': b_w_in, 'b_gate_bias': b_gate_bias, 'b_phi_pos': b_phi_pos, 'b_phi_w1': b_phi_w1,
        'b_phi_w2': b_phi_w2, 'b_w_out': b_w_out,
        'c_w_in': c_w_in, 'c_forget_bias': c_forget_bias, 'c_w_out': c_w_out,
    }


def reference(x_prompt, x_sample, cache_l0_w128, cache_l0_w512, cache_l0_w2048,
              cache_l1_cmp, cache_l1_sel, cache_l1_win, cache_l2_k, cache_l2_v, cache_l2_logf,
              cache_l3_w128, cache_l3_w512, cache_l3_w2048, page_table,
              norm_g, ffn_w_gate, ffn_w_up, ffn_w_down, final_norm_g, rel_bias_table,
              a_w_in, a_w_out, b_w_in, b_gate_bias, b_phi_pos, b_phi_w1, b_phi_w2, b_w_out,
              c_w_in, c_forget_bias, c_w_out):
    layer_caches = (
        (cache_l0_w128, cache_l0_w512, cache_l0_w2048),
        (cache_l1_cmp, cache_l1_sel, cache_l1_win),
        (cache_l2_k, cache_l2_v, cache_l2_logf),
        (cache_l3_w128, cache_l3_w512, cache_l3_w2048),
    )
    xp, xs = x_prompt, x_sample
    new_state = []
    for i in range(DEPTH):
        kind, j = i % N_MIXERS, i // N_MIXERS
        f1 = (norm_g[i, 0], ffn_w_gate[i, 0], ffn_w_up[i, 0], ffn_w_down[i, 0])
        f2 = (norm_g[i, 2], ffn_w_gate[i, 1], ffn_w_up[i, 1], ffn_w_down[i, 1])
        xp, xs = macaron_half(xp, *f1), macaron_half(xs, *f1)
        hp, hs = rms_norm(xp, norm_g[i, 1]), rms_norm(xs, norm_g[i, 1])
        if kind == 0:
            out_p = dilated_prompt(hp, a_w_in[j], a_w_out[j], rel_bias_table)
            out_s = dilated_step(hs, layer_caches[i], a_w_in[j], a_w_out[j], rel_bias_table)
        elif kind == 1:
            b_args = (b_w_in[j], b_gate_bias[j], b_phi_pos[j], b_phi_w1[j], b_phi_w2[j], b_w_out[j], rel_bias_table)
            out_p = nsa_prompt(hp, *b_args)
            out_s = nsa_step(hs, *layer_caches[i], page_table, *b_args)
        else:
            c_args = (c_w_in[j], c_forget_bias[j], c_w_out[j])
            out_p = fox_prompt(hp, *c_args)
            out_s = fox_step(hs, *layer_caches[i], page_table, *c_args)
        xp, xs = xp + out_p[0], xs + out_s[0]
        new_state.extend(out_p[1:])
        new_state.extend(out_s[1:])
        xp, xs = macaron_half(xp, *f2), macaron_half(xs, *f2)
    y_prompt = rms_norm(xp, final_norm_g)
    y_sample = rms_norm(xs, final_norm_g)
    return (y_prompt, y_sample, *new_state)
```

```python
import functools
import math

import numpy as np
import jax
import jax.numpy as jnp
from jax import lax
from jax.experimental import pallas as pl
from jax.experimental.pallas import tpu as pltpu

F32 = jnp.float32
BF16 = jnp.bfloat16
I32 = jnp.int32

D_MODEL = 1024
HEAD_DIM = 64
D_FF = 2816
RMS_EPS = 1e-6
NEG_INF = -1e30
SCALE = HEAD_DIM ** -0.5
N_MIXERS = 3

NUM_BUCKETS = 32
REL_MAX_DIST = 2048

A_GROUPS = ((128, 1), (512, 4), (2048, 16))
A_GROUP_HEADS = 4
A_GROUP_COLS = A_GROUP_HEADS * HEAD_DIM
A_HEADS = A_GROUP_HEADS * len(A_GROUPS)
A_BAND = 128

B_HEADS = 12
B_KV_HEADS = 4
B_REP = B_HEADS // B_KV_HEADS
B_CMP_BLOCK = 32
B_CMP_STRIDE = 16
B_SEL_BLOCK = 64
B_TOPN = 16
B_WINDOW = 512
B_PHI_HIDDEN = 128
B_FORCE = 1e4
B_Q_COLS = B_HEADS * HEAD_DIM
B_KV_COLS = B_KV_HEADS * HEAD_DIM
B_GATE_COLS = 3 * B_HEADS

C_HEADS = 16
C_COLS = C_HEADS * HEAD_DIM

V7X_LANES = 128
V7X_VMEM_LIMIT_BYTES = 56 * 1024 * 1024
ROW_TILE = 512
FF_CHUNK = 256
HALF = HEAD_DIM


def _params(*sem):
    return pltpu.CompilerParams(dimension_semantics=sem, vmem_limit_bytes=V7X_VMEM_LIMIT_BYTES)


def _dot(a, b):
    return jnp.dot(a, b, preferred_element_type=F32)


def _dot_nt(a, b):
    return lax.dot_general(a, b, (((1,), (1,)), ((), ())), preferred_element_type=F32)


def _split3(x):
    hi = x.astype(BF16)
    r1 = x - hi.astype(F32)
    mid = r1.astype(BF16)
    lo = (r1 - mid.astype(F32)).astype(BF16)
    return hi, mid, lo


def _dot3(a_bf16_exact, x):
    hi, mid, lo = _split3(x)
    return _dot(a_bf16_exact, hi) + _dot(a_bf16_exact, mid) + _dot(a_bf16_exact, lo)


def _rms(x, g):
    return x * lax.rsqrt(jnp.mean(x * x, axis=-1, keepdims=True) + RMS_EPS) * g


def _lane_lo(shape):
    return (lax.broadcasted_iota(I32, shape, len(shape) - 1) % V7X_LANES) < HALF


def _pick_head(slab, half):
    lo = _lane_lo(slab.shape)
    return jnp.where(lo if half == 0 else jnp.logical_not(lo), slab, jnp.zeros_like(slab))


def _align_head(slab, src_half, dst_half):
    if src_half == dst_half:
        return slab
    return pltpu.roll(slab, HALF, 1)


def _masked_softmax(s, mask):
    s = jnp.where(mask, s, NEG_INF)
    m = jnp.max(s, axis=-1, keepdims=True)
    p = jnp.where(mask, jnp.exp(s - m), 0.0)
    den = jnp.sum(p, axis=-1, keepdims=True)
    return p, m, jnp.where(den > 0, den, 1.0)


def _rel_bucket_np(dist):
    exact = NUM_BUCKETS // 2
    d = np.maximum(dist, 0)
    logd = (np.log(np.maximum(d, 1).astype(np.float32) / np.float32(exact))
            / np.float32(math.log(REL_MAX_DIST / exact))).astype(np.float32)
    far = np.minimum(exact + (logd * np.float32(NUM_BUCKETS - exact)).astype(np.int32), NUM_BUCKETS - 1)
    return np.where(d < exact, d, far).astype(np.int32)


def _bias_of_dist(table, dist_np):
    return jnp.moveaxis(table.astype(F32)[_rel_bucket_np(dist_np)], -1, 0)


def _ffn_kernel(x_ref, g_ref, wg_ref, wu_ref, wd_ref, *rest, final):
    x = x_ref[...]
    h = _rms(x, g_ref[...]).astype(BF16)
    acc = jnp.zeros_like(x)
    for c in range(D_FF // FF_CHUNK):
        sl = slice(c * FF_CHUNK, (c + 1) * FF_CHUNK)
        gate = _dot(h, wg_ref[:, sl])
        up = _dot(h, wu_ref[:, sl])
        act = (gate * jax.nn.sigmoid(gate) * up).astype(BF16)
        acc = acc + _dot(act, wd_ref[sl, :])
    y = x + 0.5 * acc
    if final:
        gf_ref, o_ref = rest
        o_ref[...] = _rms(y, gf_ref[...])
    else:
        (o_ref,) = rest
        o_ref[...] = y


def _row_tile(rows):
    return ROW_TILE if rows % ROW_TILE == 0 else rows


def _ffn(x, g, wg, wu, wd, final_g=None):
    rows = x.shape[0]
    tm = _row_tile(rows)
    row = pl.BlockSpec((tm, D_MODEL), lambda i: (i, 0))
    vec = pl.BlockSpec((1, D_MODEL), lambda i: (0, 0))
    whole = lambda a: pl.BlockSpec(a.shape, lambda i: (0,) * a.ndim)
    args = [x, g.reshape(1, D_MODEL), wg, wu, wd]
    specs = [row, vec, whole(wg), whole(wu), whole(wd)]
    if final_g is not None:
        args.append(final_g.reshape(1, D_MODEL))
        specs.append(vec)
    return pl.pallas_call(
        functools.partial(_ffn_kernel, final=final_g is not None),
        grid=(rows // tm,), in_specs=specs, out_specs=row,
        out_shape=jax.ShapeDtypeStruct(x.shape, F32),
        compiler_params=_params("arbitrary"), name="macaron_swiglu")(*args)


def _out_proj_kernel(x_ref, o_ref, w_ref, y_ref):
    y_ref[...] = x_ref[...] + _dot(o_ref[...].astype(BF16), w_ref[...])


def _out_proj(x, o, w):
    rows = x.shape[0]
    tm = _row_tile(rows)
    cols = o.shape[1]
    return pl.pallas_call(
        _out_proj_kernel, grid=(rows // tm,),
        in_specs=[pl.BlockSpec((tm, D_MODEL), lambda i: (i, 0)),
                  pl.BlockSpec((tm, cols), lambda i: (i, 0)),
                  pl.BlockSpec(w.shape, lambda i: (0, 0))],
        out_specs=pl.BlockSpec((tm, D_MODEL), lambda i: (i, 0)),
        out_shape=jax.ShapeDtypeStruct(x.shape, F32),
        compiler_params=_params("arbitrary"), name="mixer_out_proj")(x, o, w)


def _a_proj_kernel(x_ref, g_ref, w_ref, q0_ref, q1_ref, q2_ref, kv0_ref, kv1_ref, kv2_ref):
    h = _rms(x_ref[...], g_ref[...]).astype(BF16)
    nq = A_HEADS * HEAD_DIM
    for gi, (q_ref, kv_ref) in enumerate(((q0_ref, kv0_ref), (q1_ref, kv1_ref), (q2_ref, kv2_ref))):
        c0 = gi * A_GROUP_COLS
        q_ref[...] = _dot(h, w_ref[:, c0:c0 + A_GROUP_COLS]).astype(BF16)
        kv_ref[:, :A_GROUP_COLS] = _dot(h, w_ref[:, nq + c0:nq + c0 + A_GROUP_COLS])
        kv_ref[:, A_GROUP_COLS:] = _dot(h, w_ref[:, 2 * nq + c0:2 * nq + c0 + A_GROUP_COLS])


def _a_proj(x, g, w):
    rows = x.shape[0]
    tm = _row_tile(rows)
    qspec = pl.BlockSpec((tm, A_GROUP_COLS), lambda i: (i, 0))
    kvspec = pl.BlockSpec((tm, 2 * A_GROUP_COLS), lambda i: (i, 0))
    return pl.pallas_call(
        _a_proj_kernel, grid=(rows // tm,),
        in_specs=[pl.BlockSpec((tm, D_MODEL), lambda i: (i, 0)),
                  pl.BlockSpec((1, D_MODEL), lambda i: (0, 0)),
                  pl.BlockSpec(w.shape, lambda i: (0, 0))],
        out_specs=[qspec] * 3 + [kvspec] * 3,
        out_shape=[jax.ShapeDtypeStruct((rows, A_GROUP_COLS), BF16)] * 3
        + [jax.ShapeDtypeStruct((rows, 2 * A_GROUP_COLS), F32)] * 3,
        compiler_params=_params("arbitrary"), name="dilated_in_proj")(x, g.reshape(1, D_MODEL), w)


def _a_attn_kernel(q_ref, kvc_ref, kvp_ref, bias_ref, o_ref, l_ref):
    blk = pl.program_id(2)
    band = A_BAND
    q = q_ref[0]
    kvc = kvc_ref[0]
    kvp = kvp_ref[0]
    qi = lax.broadcasted_iota(I32, (band, 2 * band), 0)
    kj = lax.broadcasted_iota(I32, (band, 2 * band), 1)
    off = qi + band - kj
    mask = (off >= 0) & (off <= band) & ((kj >= band) | (blk > 0))
    lo = _lane_lo((band, V7X_LANES))
    for hp in range(A_GROUP_HEADS // 2):
        ksl = slice(hp * V7X_LANES, (hp + 1) * V7X_LANES)
        vsl = slice(A_GROUP_COLS + hp * V7X_LANES, A_GROUP_COLS + (hp + 1) * V7X_LANES)
        k = jnp.concatenate([kvp[:, ksl], kvc[:, ksl]], axis=0).astype(BF16)
        v = jnp.concatenate([kvp[:, vsl], kvc[:, vsl]], axis=0).astype(BF16)
        qs = q[:, ksl]
        outs, lses = [], []
        for half in range(2):
            s = _dot_nt(_pick_head(qs, half), k) * SCALE + bias_ref[2 * hp + half]
            p, m, den = _masked_softmax(s, mask)
            outs.append(_dot(p.astype(BF16), v) / den)
            lses.append(jnp.broadcast_to(m + jnp.log(den), (band, V7X_LANES)))
        o_ref[0, :, ksl] = jnp.where(lo, outs[0], outs[1])
        l_ref[0, :, ksl] = jnp.where(lo, lses[0], lses[1])


def _a_attn_group(q, kv, bias, batch, seq, dil):
    sub = seq // dil
    nb = sub // A_BAND
    qv = q.reshape(batch, sub, dil * A_GROUP_COLS)
    kvv = kv.reshape(batch, sub, dil * 2 * A_GROUP_COLS)
    qspec = pl.BlockSpec((1, A_BAND, A_GROUP_COLS), lambda b, r, i: (b, i, r))
    o, lse = pl.pallas_call(
        _a_attn_kernel, grid=(batch, dil, nb),
        in_specs=[qspec,
                  pl.BlockSpec((1, A_BAND, 2 * A_GROUP_COLS), lambda b, r, i: (b, i, r)),
                  pl.BlockSpec((1, A_BAND, 2 * A_GROUP_COLS), lambda b, r, i: (b, jnp.maximum(i - 1, 0), r)),
                  pl.BlockSpec(bias.shape, lambda b, r, i: (0, 0, 0))],
        out_specs=[qspec, qspec],
        out_shape=[jax.ShapeDtypeStruct(qv.shape, F32)] * 2,
        compiler_params=_params("arbitrary", "arbitrary", "arbitrary"), name="dilated_band_attention")(
            qv, kvv, kvv, bias)
    return o.reshape(batch * seq, A_GROUP_COLS), lse.reshape(batch * seq, A_GROUP_COLS)


def _a_band_bias(table, g, dil):
    off = np.arange(A_BAND)[:, None] + A_BAND - np.arange(2 * A_BAND)[None, :]
    dist = np.clip(off, 0, A_BAND) * dil
    return _bias_of_dist(table[:, g * A_GROUP_HEADS:(g + 1) * A_GROUP_HEADS], dist)


def _a_out_kernel(x_ref, o0_ref, o1_ref, o2_ref, l0_ref, l1_ref, l2_ref, w_ref, y_ref):
    ls = [l0_ref[...], l1_ref[...], l2_ref[...]]
    m = jnp.maximum(jnp.maximum(ls[0], ls[1]), ls[2])
    es = [jnp.exp(l - m) for l in ls]
    den = es[0] + es[1] + es[2]
    y = x_ref[...]
    for gi, o_ref in enumerate((o0_ref, o1_ref, o2_ref)):
        og = (o_ref[...] * (es[gi] / den)).astype(BF16)
        y = y + _dot(og, w_ref[gi * A_GROUP_COLS:(gi + 1) * A_GROUP_COLS, :])
    y_ref[...] = y


def _a_out(x, outs, lses, w):
    rows = x.shape[0]
    tm = _row_tile(rows)
    gspec = pl.BlockSpec((tm, A_GROUP_COLS), lambda i: (i, 0))
    xspec = pl.BlockSpec((tm, D_MODEL), lambda i: (i, 0))
    return pl.pallas_call(
        _a_out_kernel, grid=(rows // tm,),
        in_specs=[xspec] + [gspec] * 6 + [pl.BlockSpec(w.shape, lambda i: (0, 0))],
        out_specs=xspec, out_shape=jax.ShapeDtypeStruct(x.shape, F32),
        compiler_params=_params("arbitrary"), name="dilated_combine_out_proj")(x, *outs, *lses, w)


A_STEP_KEYS = 2 * A_BAND


def _a_step_kernel(q0_ref, q1_ref, q2_ref, n0_ref, n1_ref, n2_ref, b0_ref, b1_ref, b2_ref, bias_ref,
                   o0_ref, o1_ref, o2_ref, l0_ref, l1_ref, l2_ref):
    rows = 8
    lane = lax.broadcasted_iota(I32, (rows, A_STEP_KEYS), 1)
    mask = lane <= A_BAND
    lo = _lane_lo((rows, V7X_LANES))
    groups = ((q0_ref, n0_ref, b0_ref, o0_ref, l0_ref), (q1_ref, n1_ref, b1_ref, o1_ref, l1_ref),
              (q2_ref, n2_ref, b2_ref, o2_ref, l2_ref))
    for gi, (q_ref, n_ref, b_ref, o_ref, l_ref) in enumerate(groups):
        q = jnp.broadcast_to(q_ref[0], (rows, A_GROUP_COLS))
        new = jnp.broadcast_to(n_ref[0], (rows, 2 * A_GROUP_COLS))
        buf = b_ref[0]
        pad = jnp.zeros((A_STEP_KEYS - A_BAND - rows, V7X_LANES), BF16)
        for hp in range(A_GROUP_HEADS // 2):
            ksl = slice(hp * V7X_LANES, (hp + 1) * V7X_LANES)
            vsl = slice(A_GROUP_COLS + hp * V7X_LANES, A_GROUP_COLS + (hp + 1) * V7X_LANES)
            k = jnp.concatenate([buf[:, ksl].astype(BF16), new[:, ksl].astype(BF16), pad], axis=0)
            v = jnp.concatenate([buf[:, vsl].astype(BF16), new[:, vsl].astype(BF16), pad], axis=0)
            qs = q[:, ksl]
            outs, lses = [], []
            for half in range(2):
                head = gi * A_GROUP_HEADS + 2 * hp + half
                s = _dot_nt(_pick_head(qs, half), k) * SCALE + bias_ref[head:head + 1, :]
                p, m, den = _masked_softmax(s, mask)
                outs.append(_dot(p.astype(BF16), v) / den)
                lses.append(jnp.broadcast_to(m + jnp.log(den), (rows, V7X_LANES)))
            o_ref[0, :, ksl] = jnp.where(lo, outs[0], outs[1])[0:1]
            l_ref[0, :, ksl] = jnp.where(lo, lses[0], lses[1])[0:1]


def _a_step_bias(table):
    rows = []
    for g, (_, dil) in enumerate(A_GROUPS):
        dist = np.concatenate([(A_BAND - np.arange(A_BAND)) * dil, np.zeros(A_STEP_KEYS - A_BAND, np.int64)])
        rows.append(_bias_of_dist(table[:, g * A_GROUP_HEADS:(g + 1) * A_GROUP_HEADS], dist))
    return jnp.concatenate(rows, axis=0)


def _a_step_attn(qs, news, bufs, bias):
    n = qs[0].shape[0]
    q3 = [q.reshape(n, 1, A_GROUP_COLS) for q in qs]
    n3 = [x.reshape(n, 1, 2 * A_GROUP_COLS) for x in news]
    b3 = [b.reshape(n, A_BAND, dil * 2 * A_GROUP_COLS) for b, (_, dil) in zip(bufs, A_GROUPS)]
    qspec = pl.BlockSpec((1, 1, A_GROUP_COLS), lambda b: (b, 0, 0))
    outs = pl.pallas_call(
        _a_step_kernel, grid=(n,),
        in_specs=[qspec] * 3 + [pl.BlockSpec((1, 1, 2 * A_GROUP_COLS), lambda b: (b, 0, 0))] * 3
        + [pl.BlockSpec((1, A_BAND, 2 * A_GROUP_COLS), lambda b: (b, 0, 0))] * 3
        + [pl.BlockSpec(bias.shape, lambda b: (0, 0))],
        out_specs=[qspec] * 6, out_shape=[jax.ShapeDtypeStruct((n, 1, A_GROUP_COLS), F32)] * 6,
        compiler_params=_params("arbitrary"), name="dilated_step_attention")(*q3, *n3, *b3, bias)
    outs = [o.reshape(n, A_GROUP_COLS) for o in outs]
    return outs[:3], outs[3:]


def _mixer_a(xp, xs, g, w_in, w_out, table, caches, batch, seq):
    w_in = w_in.astype(BF16)
    w_out = w_out.astype(BF16)
    pq0, pq1, pq2, pkv0, pkv1, pkv2 = _a_proj(xp, g, w_in)
    outs, lses, state = [], [], []
    for gi, ((win, dil), q, kv) in enumerate(zip(A_GROUPS, (pq0, pq1, pq2), (pkv0, pkv1, pkv2))):
        o, l = _a_attn_group(q, kv, _a_band_bias(table, gi, dil), batch, seq, dil)
        outs.append(o)
        lses.append(l)
        keep = min(win, seq)
        state.append(kv.reshape(batch, seq, 2, A_GROUP_HEADS, HEAD_DIM)[:, seq - keep:])
    yp = _a_out(xp, outs, lses, w_out)

    n = xs.shape[0]
    sq0, sq1, sq2, skv0, skv1, skv2 = _a_proj(xs, g, w_in)
    souts, slses = _a_step_attn((sq0, sq1, sq2), (skv0, skv1, skv2), caches, _a_step_bias(table))
    ys = _a_out(xs, souts, slses, w_out)
    for (win, _), kv, buf in zip(A_GROUPS, (skv0, skv1, skv2), caches):
        new = kv.reshape(n, 1, 2, A_GROUP_HEADS, HEAD_DIM)
        assert buf.shape[1] == win, "the step kernel reads a full window buffer"
        state.append(jnp.concatenate([buf, new], axis=1)[:, buf.shape[1] + 1 - min(win, buf.shape[1] + 1):])
    return yp, ys, state


C_TAIL = V7X_LANES


def _log_sigmoid(z):
    return -(jnp.maximum(-z, 0.0) + jnp.log1p(jnp.exp(-jnp.abs(z))))


def _c_proj_kernel(x_ref, g_ref, w_ref, wt_ref, b_ref, tri_ref, q_ref, k_ref, v_ref, lf_ref, dc_ref, carry_ref,
                   *, tiles_per_seq):
    i = pl.program_id(0)
    h = _rms(x_ref[...], g_ref[...]).astype(BF16)
    q_ref[...] = _dot(h, w_ref[:, 0:C_COLS]).astype(BF16)
    k_ref[...] = _dot(h, w_ref[:, C_COLS:2 * C_COLS])
    v_ref[...] = _dot(h, w_ref[:, 2 * C_COLS:3 * C_COLS])
    logf = _log_sigmoid(_dot(h, wt_ref[...]) + b_ref[...])
    lf_ref[...] = logf

    @pl.when(i % tiles_per_seq == 0)
    def _():
        carry_ref[...] = jnp.zeros_like(carry_ref)

    cum = _dot3(tri_ref[...], logf) + carry_ref[0:1, :]
    dc_ref[...] = cum
    carry_ref[...] = jnp.broadcast_to(cum[cum.shape[0] - 1:, :], carry_ref.shape)


def _c_proj(x, g, w, w_tail, bias, seq):
    rows = x.shape[0]
    tm = _row_tile(rows)
    tiles_per_seq = max(seq // tm, 1)
    tri = jnp.asarray(np.tril(np.ones((tm, tm), np.float32)), BF16)
    big = pl.BlockSpec((tm, C_COLS), lambda i: (i, 0))
    small = pl.BlockSpec((tm, C_TAIL), lambda i: (i, 0))
    return pl.pallas_call(
        functools.partial(_c_proj_kernel, tiles_per_seq=tiles_per_seq), grid=(rows // tm,),
        in_specs=[pl.BlockSpec((tm, D_MODEL), lambda i: (i, 0)),
                  pl.BlockSpec((1, D_MODEL), lambda i: (0, 0)),
                  pl.BlockSpec(w.shape, lambda i: (0, 0)),
                  pl.BlockSpec(w_tail.shape, lambda i: (0, 0)),
                  pl.BlockSpec((1, C_TAIL), lambda i: (0, 0)),
                  pl.BlockSpec(tri.shape, lambda i: (0, 0))],
        out_specs=[big, big, big, small, small],
        out_shape=[jax.ShapeDtypeStruct((rows, C_COLS), BF16)] + [jax.ShapeDtypeStruct((rows, C_COLS), F32)] * 2
        + [jax.ShapeDtypeStruct((rows, C_TAIL), F32)] * 2,
        scratch_shapes=[pltpu.VMEM((8, C_TAIL), F32)],
        compiler_params=_params("arbitrary"), name="forget_in_proj")(
            x, g.reshape(1, D_MODEL), w, w_tail, bias, tri)


C_TILE = 256


def _c_attn_kernel(q_ref, k_ref, v_ref, dq_ref, dk_ref, o_ref, m_ref, l_ref, acc_ref):
    hp = pl.program_id(1)
    i = pl.program_id(2)
    t = C_TILE
    q = q_ref[0]
    dq_tile = dq_ref[0]
    lane = lax.broadcasted_iota(I32, (t, V7X_LANES), 1)
    lo = lane < HALF
    qm = [_pick_head(q, 0), _pick_head(q, 1)]
    dq = [jnp.sum(jnp.where(lane == 2 * hp + half, dq_tile, 0.0), axis=1, keepdims=True) for half in range(2)]
    causal = lax.broadcasted_iota(I32, (t, t), 0) >= lax.broadcasted_iota(I32, (t, t), 1)
    m_ref[...] = jnp.full_like(m_ref, NEG_INF)
    l_ref[...] = jnp.zeros_like(l_ref)
    acc_ref[...] = jnp.zeros_like(acc_ref)

    def tile(c, diagonal):
        start = pl.multiple_of(c * t, t)
        k = k_ref[0, pl.ds(start, t), :].astype(BF16)
        v = v_ref[0, pl.ds(start, t), :].astype(BF16)
        dk = dk_ref[0, 0, c]
        for half in range(2):
            s = _dot_nt(qm[half], k) * SCALE + dq[half] - dk[half:half + 1, :]
            if diagonal:
                s = jnp.where(causal, s, NEG_INF)
            m_old = m_ref[half]
            m_new = jnp.maximum(m_old, jnp.max(s, axis=1, keepdims=True))
            p = jnp.exp(s - m_new[:, 0:1])
            if diagonal:
                p = jnp.where(causal, p, 0.0)
            alpha = jnp.exp(m_old - m_new)
            l_ref[half] = alpha * l_ref[half] + jnp.sum(p, axis=1, keepdims=True)
            acc_ref[half] = alpha * acc_ref[half] + _dot(p.astype(BF16), v)
            m_ref[half] = m_new

    def body(c, carry):
        tile(c, False)
        return carry

    lax.fori_loop(0, i, body, 0)
    tile(i, True)
    o_ref[0] = jnp.where(lo, acc_ref[0] / l_ref[0], acc_ref[1] / l_ref[1])


def _c_attn(q, k, v, dcum, batch, seq):
    t = C_TILE
    nt = seq // t
    pairs = C_HEADS // 2
    q3, k3, v3 = (a.reshape(batch, seq, C_COLS) for a in (q, k, v))
    dq = dcum.reshape(batch, seq, C_TAIL)
    dk = dq[:, :, :C_HEADS].transpose(0, 2, 1).reshape(batch, pairs, 2, nt, t).transpose(0, 1, 3, 2, 4)
    qspec = pl.BlockSpec((1, t, V7X_LANES), lambda b, h, i: (b, i, h))
    kspec = pl.BlockSpec((1, seq, V7X_LANES), lambda b, h, i: (b, 0, h))
    o = pl.pallas_call(
        _c_attn_kernel, grid=(batch, pairs, nt),
        in_specs=[qspec, kspec, kspec,
                  pl.BlockSpec((1, t, C_TAIL), lambda b, h, i: (b, i, 0)),
                  pl.BlockSpec((1, 1, nt, 2, t), lambda b, h, i: (b, h, 0, 0, 0))],
        out_specs=qspec, out_shape=jax.ShapeDtypeStruct((batch, seq, C_COLS), F32),
        scratch_shapes=[pltpu.VMEM((2, t, V7X_LANES), F32)] * 3,
        compiler_params=_params("arbitrary", "arbitrary", "arbitrary"), name="forget_attention")(
            q3, k3, v3, dq, dk)
    return o.reshape(batch * seq, C_COLS)


def _c_step_kernel(pt_ref, q_ref, kn_ref, vn_ref, lfn_ref, k_ref, v_ref, lf_ref, seg_ref, segt_ref, upper_ref,
                   o_ref, m_ref, l_ref, acc_ref, carry_ref):
    j = pl.program_id(1)
    rows = 8
    q = jnp.broadcast_to(q_ref[0].astype(F32), (rows, C_COLS))

    def head_sums(prod):
        hi = prod.astype(BF16)
        mid = (prod - hi.astype(F32)).astype(BF16)
        return _dot(hi, seg_ref[...]) + _dot(mid, seg_ref[...])

    def expand(x):
        hi, mid, lo = _split3(x)
        return _dot(hi, segt_ref[...]) + _dot(mid, segt_ref[...]) + _dot(lo, segt_ref[...])

    @pl.when(j == 0)
    def _():
        kn = jnp.broadcast_to(kn_ref[0], (rows, C_COLS))
        m_ref[...] = head_sums(kn * q) * SCALE
        l_ref[...] = jnp.ones_like(l_ref)
        acc_ref[...] = jnp.broadcast_to(vn_ref[0], (rows, C_COLS))
        carry_ref[...] = jnp.broadcast_to(lfn_ref[0], (rows, C_HEADS))

    k = k_ref[0]
    v = v_ref[0]
    lf = lf_ref[0]
    decay = _dot3(upper_ref[...], lf) + carry_ref[0:1, :]
    s = head_sums(k * q[0:1, :]) * SCALE + decay
    m_old = m_ref[...]
    m_new = jnp.maximum(m_old, jnp.max(s, axis=0, keepdims=True))
    p = jnp.exp(s - m_new[0:1, :])
    alpha = jnp.exp(m_old - m_new)
    l_ref[...] = alpha * l_ref[...] + jnp.sum(p, axis=0, keepdims=True)
    pv = jnp.sum(_dot(p.astype(BF16), segt_ref[...]) * v, axis=0, keepdims=True)
    acc_ref[...] = expand(alpha) * acc_ref[...] + pv
    m_ref[...] = m_new
    carry_ref[...] = carry_ref[...] + jnp.sum(lf, axis=0, keepdims=True)

    @pl.when(j == pl.num_programs(1) - 1)
    def _():
        o_ref[0] = (acc_ref[...] / expand(l_ref[...]))[0:1]


def _c_step_attn(q, k_new, v_new, lf_new, cache_k, cache_v, cache_logf, page_table):
    n, n_pages = page_table.shape
    page = cache_k.shape[1]
    seg = np.zeros((C_COLS, C_HEADS), np.float32)
    seg[np.arange(C_COLS), np.arange(C_COLS) // HEAD_DIM] = 1.0
    upper = np.triu(np.ones((page, page), np.float32), 1)
    row = lambda cols: pl.BlockSpec((1, 1, cols), lambda b, j, pt: (b, 0, 0))
    paged = lambda cols: pl.BlockSpec((1, page, cols), lambda b, j, pt: (pt[b, n_pages - 1 - j], 0, 0))
    const = lambda a: pl.BlockSpec(a.shape, lambda b, j, pt: (0, 0))
    consts = [jnp.asarray(seg, BF16), jnp.asarray(seg.T, BF16), jnp.asarray(upper, BF16)]
    o = pl.pallas_call(
        _c_step_kernel,
        grid_spec=pltpu.PrefetchScalarGridSpec(
            num_scalar_prefetch=1, grid=(n, n_pages),
            in_specs=[row(C_COLS), row(C_COLS), row(C_COLS), row(C_HEADS),
                      paged(C_COLS), paged(C_COLS), paged(C_HEADS)] + [const(c) for c in consts],
            out_specs=row(C_COLS),
            scratch_shapes=[pltpu.VMEM((8, C_HEADS), F32), pltpu.VMEM((8, C_HEADS), F32),
                            pltpu.VMEM((8, C_COLS), F32), pltpu.VMEM((8, C_HEADS), F32)]),
        out_shape=jax.ShapeDtypeStruct((n, 1, C_COLS), F32),
        compiler_params=_params("arbitrary", "arbitrary"), name="forget_paged_step")(
            page_table, q.reshape(n, 1, C_COLS), k_new.reshape(n, 1, C_COLS), v_new.reshape(n, 1, C_COLS),
            lf_new[:, :C_HEADS].reshape(n, 1, C_HEADS),
            cache_k.reshape(-1, page, C_COLS), cache_v.reshape(-1, page, C_COLS), cache_logf, *consts)
    return o.reshape(n, C_COLS)


def _mixer_c(xp, xs, g, w_in, forget_bias, w_out, caches, page_table, batch, seq):
    w = w_in[:, :3 * C_COLS].astype(BF16)
    w_tail = jnp.pad(w_in[:, 3 * C_COLS:], ((0, 0), (0, C_TAIL - C_HEADS))).astype(BF16)
    bias = jnp.pad(forget_bias, (0, C_TAIL - C_HEADS)).reshape(1, C_TAIL)
    w_out = w_out.astype(BF16)
    cache_k, cache_v, cache_logf = caches

    q, k, v, logf, dcum = _c_proj(xp, g, w, w_tail, bias, seq)
    o = _c_attn(q, k, v, dcum, batch, seq)
    yp = _out_proj(xp, o, w_out)
    state = [k.reshape(batch, seq, C_HEADS, HEAD_DIM), v.reshape(batch, seq, C_HEADS, HEAD_DIM),
             logf[:, :C_HEADS].reshape(batch, seq, C_HEADS)]

    n = xs.shape[0]
    sq, sk, sv, slogf, _ = _c_proj(xs, g, w, w_tail, bias, 1)
    so = _c_step_attn(sq, sk, sv, slogf, cache_k, cache_v, cache_logf, page_table)
    ys = _out_proj(xs, so, w_out)
    state += [sk.reshape(n, 1, C_HEADS, HEAD_DIM), sv.reshape(n, 1, C_HEADS, HEAD_DIM),
              slogf[:, :C_HEADS].reshape(n, 1, C_HEADS)]
    return yp, ys, state


B_TAIL = V7X_LANES
B_SLAB = 2 * B_KV_COLS


def _b_proj_kernel(x_ref, g_ref, w_ref, wt_ref, b_ref, q_ref, cmp_ref, sel_ref, win_ref, gate_ref):
    h = _rms(x_ref[...], g_ref[...]).astype(BF16)
    q_ref[...] = _dot(h, w_ref[:, 0:B_Q_COLS]).astype(BF16)
    c0 = B_Q_COLS
    cmp_ref[...] = _dot(h, w_ref[:, c0:c0 + B_SLAB])
    sel_ref[...] = _dot(h, w_ref[:, c0 + B_SLAB:c0 + 2 * B_SLAB])
    win_ref[...] = _dot(h, w_ref[:, c0 + 2 * B_SLAB:c0 + 3 * B_SLAB])
    gate_ref[...] = jax.nn.sigmoid(_dot(h, wt_ref[...]) + b_ref[...])


def _b_proj(x, g, w, w_tail, bias):
    rows = x.shape[0]
    tm = _row_tile(rows)
    spec = lambda cols: pl.BlockSpec((tm, cols), lambda i: (i, 0))
    shape = lambda cols, dt: jax.ShapeDtypeStruct((rows, cols), dt)
    return pl.pallas_call(
        _b_proj_kernel, grid=(rows // tm,),
        in_specs=[spec(D_MODEL), pl.BlockSpec((1, D_MODEL), lambda i: (0, 0)),
                  pl.BlockSpec(w.shape, lambda i: (0, 0)), pl.BlockSpec(w_tail.shape, lambda i: (0, 0)),
                  pl.BlockSpec((1, B_TAIL), lambda i: (0, 0))],
        out_specs=[spec(B_Q_COLS)] + [spec(B_SLAB)] * 3 + [spec(B_TAIL)],
        out_shape=[shape(B_Q_COLS, BF16)] + [shape(B_SLAB, F32)] * 3 + [shape(B_TAIL, F32)],
        compiler_params=_params("arbitrary"), name="sparse_in_proj_step")(x, g.reshape(1, D_MODEL), w, w_tail, bias)


B_GROUP_Q = 2 * V7X_LANES
B_GQ_COLS = B_KV_HEADS * B_GROUP_Q
B_GKV_COLS = B_KV_HEADS * V7X_LANES
B_GGATE_COLS = B_KV_HEADS * V7X_LANES


def _b_proj_grouped_kernel(x_ref, g_ref, w_ref, wt_ref, b_ref, q_ref, cmp_ref, sel_ref, win_ref, selg_ref, wing_ref,
                           gate_ref):
    h = _rms(x_ref[...], g_ref[...]).astype(BF16)
    c = 0
    q_ref[...] = _dot(h, w_ref[:, c:c + B_GQ_COLS]).astype(BF16)
    c += B_GQ_COLS
    for ref in (cmp_ref, sel_ref, win_ref):
        ref[...] = _dot(h, w_ref[:, c:c + B_SLAB])
        c += B_SLAB
    for ref in (selg_ref, wing_ref):
        ref[...] = _dot(h, w_ref[:, c:c + B_GKV_COLS]).astype(BF16)
        c += B_GKV_COLS
    gate_ref[...] = jax.nn.sigmoid(_dot(h, wt_ref[...]) + b_ref[...])


def _b_grouped_weights(w_in, gate_bias):
    q = w_in[:, :B_Q_COLS].reshape(D_MODEL, B_KV_HEADS, B_REP * HEAD_DIM)
    q = jnp.pad(q, ((0, 0), (0, 0), (0, B_GROUP_Q - B_REP * HEAD_DIM))).reshape(D_MODEL, B_GQ_COLS)
    main = w_in[:, B_Q_COLS:B_Q_COLS + 3 * B_SLAB]

    def grouped(slab):
        kv = slab.reshape(D_MODEL, 2, B_KV_HEADS, HEAD_DIM)
        return kv.transpose(0, 2, 1, 3).reshape(D_MODEL, B_GKV_COLS)

    w = jnp.concatenate([q, main, grouped(main[:, B_SLAB:2 * B_SLAB]), grouped(main[:, 2 * B_SLAB:])], axis=1)

    def gates(a):
        a = a.reshape(a.shape[:-1] + (3, B_KV_HEADS, B_REP))
        a = jnp.moveaxis(a, -2, -3).reshape(a.shape[:-3] + (B_KV_HEADS, 3 * B_REP))
        pad = [(0, 0)] * (a.ndim - 1) + [(0, V7X_LANES - 3 * B_REP)]
        return jnp.pad(a, pad).reshape(a.shape[:-2] + (B_GGATE_COLS,))

    tail = w_in[:, B_Q_COLS + 3 * B_SLAB:]
    return w.astype(BF16), gates(tail).astype(BF16), gates(gate_bias).reshape(1, B_GGATE_COLS)


def _b_proj_grouped(x, g, w, w_tail, bias):
    rows = x.shape[0]
    tm = _row_tile(rows)
    spec = lambda cols: pl.BlockSpec((tm, cols), lambda i: (i, 0))
    shape = lambda cols, dt: jax.ShapeDtypeStruct((rows, cols), dt)
    return pl.pallas_call(
        _b_proj_grouped_kernel, grid=(rows // tm,),
        in_specs=[spec(D_MODEL), pl.BlockSpec((1, D_MODEL), lambda i: (0, 0)),
                  pl.BlockSpec(w.shape, lambda i: (0, 0)), pl.BlockSpec(w_tail.shape, lambda i: (0, 0)),
                  pl.BlockSpec((1, B_GGATE_COLS), lambda i: (0, 0))],
        out_specs=[spec(B_GQ_COLS)] + [spec(B_SLAB)] * 3 + [spec(B_GKV_COLS)] * 2 + [spec(B_GGATE_COLS)],
        out_shape=[shape(B_GQ_COLS, BF16)] + [shape(B_SLAB, F32)] * 3 + [shape(B_GKV_COLS, BF16)] * 2
        + [shape(B_GGATE_COLS, F32)],
        compiler_params=_params("arbitrary"), name="sparse_in_proj_prompt")(x, g.reshape(1, D_MODEL), w, w_tail, bias)


B_PHI_ROWS = 2 * B_KV_HEADS * B_PHI_HIDDEN
B_CHUNK_COLS = B_CMP_STRIDE * B_SLAB


def _compress_weights(phi_pos, phi_w1, phi_w2, pad_heads):
    half = B_CMP_BLOCK // B_CMP_STRIDE
    eye = jnp.eye(B_KV_HEADS, dtype=F32)
    w1r = phi_w1.reshape(2, half, B_CMP_STRIDE, HEAD_DIM, B_PHI_HIDDEN)
    first = jnp.einsum('cpsde,gh->cspgehd', w1r, eye).reshape(2, B_CMP_STRIDE, B_PHI_ROWS, B_KV_COLS)
    pos_term = jnp.einsum('cpsd,cpsde->cpe', phi_pos.reshape(2, half, B_CMP_STRIDE, HEAD_DIM), w1r,
                          precision=lax.Precision.HIGHEST)
    pos = jnp.broadcast_to(pos_term[:, :, None, :, None], (2, half, B_KV_HEADS, B_PHI_HIDDEN, V7X_LANES))
    second = jnp.einsum('ced,gh->cgdhe', phi_w2, eye)
    if pad_heads:
        second = jnp.pad(second, ((0, 0), (0, 0), (0, V7X_LANES - HEAD_DIM), (0, 0), (0, 0)))
    second = second.reshape(2, -1, B_KV_HEADS * B_PHI_HIDDEN)
    return first.astype(BF16), pos.reshape(2, B_PHI_ROWS, V7X_LANES), second.astype(BF16)


def _compress_first(load, first_ref, c):
    acc = None
    for s in range(B_CMP_STRIDE):
        part = _dot_nt(first_ref[c, s], load(s, c).astype(BF16))
        acc = part if acc is None else acc + part
    return acc


def _compress_second(acc, pos_ref, second_ref, c):
    n_chunks = acc.shape[1]
    acc = acc + pos_ref[c][:, 0:1]
    rows = B_PHI_ROWS // 2
    pre = acc[:rows] + pltpu.roll(acc[rows:], n_chunks - 1, 1)
    return _dot(second_ref[c], jax.nn.gelu(pre).astype(BF16))


def _b_compress_prompt_kernel(x_ref, first_ref, pos_ref, second_ref, o_ref):
    def load(s, c):
        c0 = s * B_SLAB + c * B_KV_COLS
        return x_ref[0, :, c0:c0 + B_KV_COLS]

    for c in range(2):
        o_ref[0, c] = _compress_second(_compress_first(load, first_ref, c), pos_ref, second_ref, c).astype(BF16)


def _b_compress_prompt(cmp, weights, batch, seq):
    n_chunks = seq // B_CMP_STRIDE
    first, pos, second = weights
    out_rows = second.shape[1]
    whole = lambda a: pl.BlockSpec(a.shape, lambda b: (0,) * a.ndim)
    return pl.pallas_call(
        _b_compress_prompt_kernel, grid=(batch,),
        in_specs=[pl.BlockSpec((1, n_chunks, B_CHUNK_COLS), lambda b: (b, 0, 0)),
                  whole(first), whole(pos), whole(second)],
        out_specs=pl.BlockSpec((1, 2, out_rows, n_chunks), lambda b: (b, 0, 0, 0)),
        out_shape=jax.ShapeDtypeStruct((batch, 2, out_rows, n_chunks), BF16),
        compiler_params=_params("arbitrary"), name="sparse_compress_prompt")(
            cmp.reshape(batch, n_chunks, B_CHUNK_COLS), first, pos, second).reshape(
                batch, 2, B_KV_HEADS, V7X_LANES, n_chunks)


def _b_compress_step_kernel(pt_ref, cache_ref, first_ref, pos_ref, second_ref, o_ref, buf_ref, acc_ref, sem_ref,
                            *, pages_per_half, chunks_per_page):
    n = pl.program_id(0)
    hf = pl.program_id(1)
    step = n * 2 + hf
    total = pl.num_programs(0) * 2
    slot = step % 2
    half_chunks = pages_per_half * chunks_per_page

    def copies(st, sl):
        nn = st // 2
        hh = st % 2
        return [pltpu.make_async_copy(cache_ref.at[pt_ref[nn, hh * pages_per_half + j]],
                                      buf_ref.at[sl, pl.ds(j * chunks_per_page, chunks_per_page), :],
                                      sem_ref.at[sl]) for j in range(pages_per_half)]

    @pl.when(step == 0)
    def _():
        for cp in copies(0, 0):
            cp.start()

    @pl.when(step + 1 < total)
    def _():
        for cp in copies(step + 1, 1 - slot):
            cp.start()

    for cp in copies(step, slot):
        cp.wait()

    def load(s, c):
        c0 = s * B_SLAB + c * B_KV_COLS
        return buf_ref[slot, :, c0:c0 + B_KV_COLS]

    for c in range(2):
        part = _compress_first(load, first_ref, c)

        @pl.when(hf == 0)
        def _():
            acc_ref[c, :, 0:half_chunks] = part

        @pl.when(hf == 1)
        def _():
            acc_ref[c, :, half_chunks:2 * half_chunks] = part

    @pl.when(hf == 1)
    def _():
        for c in range(2):
            o_ref[0, c] = _compress_second(acc_ref[c], pos_ref, second_ref, c).astype(BF16)


def _b_compress_step(cache_cmp, page_table, weights):
    n, n_pages = page_table.shape
    page = cache_cmp.shape[1]
    chunks_per_page = page // B_CMP_STRIDE
    n_chunks = n_pages * chunks_per_page
    pages_per_half = n_pages // 2
    first, pos, second = weights
    whole = lambda a: pl.BlockSpec(a.shape, lambda b, h, pt: (0,) * a.ndim)
    return pl.pallas_call(
        functools.partial(_b_compress_step_kernel, pages_per_half=pages_per_half, chunks_per_page=chunks_per_page),
        grid_spec=pltpu.PrefetchScalarGridSpec(
            num_scalar_prefetch=1, grid=(n, 2),
            in_specs=[pl.BlockSpec(memory_space=pl.ANY), whole(first), whole(pos), whole(second)],
            out_specs=pl.BlockSpec((1, 2, B_KV_COLS, n_chunks), lambda b, h, pt: (b, 0, 0, 0)),
            scratch_shapes=[pltpu.VMEM((2, n_chunks // 2, B_CHUNK_COLS), F32),
                            pltpu.VMEM((2, B_PHI_ROWS, n_chunks), F32),
                            pltpu.SemaphoreType.DMA((2,))]),
        out_shape=jax.ShapeDtypeStruct((n, 2, B_KV_COLS, n_chunks), BF16),
        compiler_params=_params("arbitrary", "arbitrary"), name="sparse_compress_step")(
            page_table, cache_cmp.reshape(-1, chunks_per_page, B_CHUNK_COLS), first, pos, second)


def _dist_bias(table, n):
    return _bias_of_dist(table, np.arange(n))


B_TILE = 128
B_FAR_TILES = REL_MAX_DIST // B_TILE + 2


def _toeplitz_tiles(table):
    t = B_TILE
    heads = table.shape[1]
    span = B_FAR_TILES * t
    w = jnp.concatenate([jnp.zeros((heads, t - 1), F32), _dist_bias(table, span)], axis=1)
    lw = w.shape[1]
    skew = jnp.tile(w, (1, t + 1))[:, :t * (lw + 1)].reshape(heads, t, lw + 1)[:, :, :span]
    return skew[:, ::-1, :].reshape(heads, t, B_FAR_TILES, t).transpose(0, 2, 3, 1)


def _cmp_bias_tiles(table, seq, n_chunks):
    heads = table.shape[1]
    off = B_CMP_STRIDE * (n_chunks - 1) + B_CMP_BLOCK - 1
    w = jnp.concatenate([jnp.zeros((heads, off), F32), _dist_bias(table, seq)], axis=1)
    lw = w.shape[1]
    step = lw + B_CMP_STRIDE
    skew = jnp.tile(w, (1, n_chunks + 1))[:, :n_chunks * step].reshape(heads, n_chunks, step)[:, :, :seq]
    return skew[:, ::-1, :].reshape(heads, n_chunks, seq // B_TILE, B_TILE).transpose(2, 0, 3, 1)


def _overlap_t(n_sel_padded, n_cmp_padded, n_sel, n_cmp):
    c_start = np.arange(n_cmp_padded)[None, :] * B_CMP_STRIDE
    s_start = np.arange(n_sel_padded)[:, None] * B_SEL_BLOCK
    ov = (c_start < s_start + B_SEL_BLOCK) & (c_start + B_CMP_BLOCK > s_start)
    ov &= (np.arange(n_cmp_padded)[None, :] < n_cmp) & (np.arange(n_sel_padded)[:, None] < n_sel)
    return ov.astype(np.float32)


B_SEL_CHUNK = 512
B_SEL_PAD = 128
B_PAD_SCORE = -3e30


def _b_head_layout(g, r):
    h = g * B_REP + r
    kv0 = (g // 2) * V7X_LANES
    return h, h // 2, h % 2, slice(kv0, kv0 + V7X_LANES), g % 2


def _b_attn_kernel(q_ref, gate_ref, kvt_ref, tcmp_ref, sel_ref, win_ref, ttab_ref, ovt_ref, o_ref,
                   m_ref, l_ref, acc_ref, imp_ref, *, n_sel):
    g = pl.program_id(1)
    i = pl.program_id(2)
    t = B_TILE
    ch = B_SEL_CHUNK
    tiles_per_chunk = ch // t
    q = q_ref[0]
    gate = gate_ref[0]
    n_cmp_pad = kvt_ref.shape[4]
    qpos_c = i * t + lax.broadcasted_iota(I32, (t, n_cmp_pad), 0)
    cidx = lax.broadcasted_iota(I32, (t, n_cmp_pad), 1)
    mask_c = qpos_c >= cidx * B_CMP_STRIDE + (B_CMP_BLOCK - 1)
    blk = lax.broadcasted_iota(I32, (B_SEL_PAD, t), 0)
    cur = (i * t + lax.broadcasted_iota(I32, (B_SEL_PAD, t), 1)) // B_SEL_BLOCK
    forced = (blk == 0) | (blk == cur) | (blk == cur - 1)
    qpos_s = i * t + lax.broadcasted_iota(I32, (t, ch), 0)
    klane = lax.broadcasted_iota(I32, (t, ch), 1)
    eb = lax.broadcasted_iota(I32, (B_SEL_PAD, ch), 0)
    ek = lax.broadcasted_iota(I32, (B_SEL_PAD, ch), 1) // B_SEL_BLOCK
    n_win_tiles = B_WINDOW // t + 1
    wq = lax.broadcasted_iota(I32, (t, n_win_tiles * t), 0)
    wk = lax.broadcasted_iota(I32, (t, n_win_tiles * t), 1)
    dist_w = B_WINDOW + wq - wk
    mask_w = (dist_w >= 0) & (dist_w < B_WINDOW) & (wk // t >= n_win_tiles - 1 - i)
    n_chunks = (i + tiles_per_chunk) // tiles_per_chunk
    qms = [_align_head(_pick_head(q[:, (r // 2) * V7X_LANES:(r // 2 + 1) * V7X_LANES], r % 2), r % 2, 0)
           for r in range(B_REP)]

    kct = kvt_ref[0, 0, 0]
    vct = kvt_ref[0, 1, 0]
    p_sum = jnp.zeros((t, n_cmp_pad), F32)
    o_cmp = []
    for r in range(B_REP):
        s = _dot(qms[r], kct) * SCALE + tcmp_ref[0, r]
        p, _, den = _masked_softmax(s, mask_c)
        p = p / den
        p_sum = p_sum + p
        o_cmp.append(_dot_nt(p.astype(BF16), vct))

    hi, mid, lo3 = _split3(p_sum)
    imp = _dot_nt(ovt_ref[...], hi) + _dot_nt(ovt_ref[...], mid) + _dot_nt(ovt_ref[...], lo3)
    imp = jnp.where(forced, B_FORCE, jnp.where(blk <= cur, imp, NEG_INF))
    imp = jnp.where(blk < n_sel, imp, B_PAD_SCORE)
    imp_ref[...] = imp

    def rank_body(j, rank):
        row = imp_ref[pl.ds(j, 1), :]
        ahead = (row > imp) | ((row == imp) & (j < blk))
        return rank + jnp.where(ahead, 1.0, 0.0)

    rank = lax.fori_loop(0, n_sel, rank_body, jnp.zeros((B_SEL_PAD, t), F32))
    chosen = jnp.transpose(jnp.where(rank < B_TOPN, 1.0, 0.0)).astype(BF16)

    m_ref[...] = jnp.full_like(m_ref, NEG_INF)
    l_ref[...] = jnp.zeros_like(l_ref)
    acc_ref[...] = jnp.zeros_like(acc_ref)

    def chunk_body(c, carry):
        start = pl.multiple_of(c * ch, ch)
        kv = sel_ref[0, pl.ds(start, ch), :]
        expand = jnp.where(eb == ek + c * (ch // B_SEL_BLOCK), 1.0, 0.0).astype(BF16)
        mask = (_dot(chosen, expand) > 0.5) & (klane + c * ch <= qpos_s)
        deltas = [jnp.clip(i - c * tiles_per_chunk - j, 0, B_FAR_TILES - 1) for j in range(tiles_per_chunk)]
        for r in range(B_REP):
            bias = jnp.concatenate([ttab_ref[g * B_REP + r, d] for d in deltas], axis=1)
            s = jnp.where(mask, _dot_nt(qms[r], kv) * SCALE + bias, NEG_INF)
            m_old = m_ref[r]
            m_new = jnp.maximum(m_old, jnp.max(s, axis=1, keepdims=True))
            p = jnp.where(mask, jnp.exp(s - m_new[:, 0:1]), 0.0)
            alpha = jnp.exp(m_old - m_new)
            l_ref[r] = alpha * l_ref[r] + jnp.sum(p, axis=1, keepdims=True)
            acc_ref[r] = alpha * acc_ref[r] + _dot(p.astype(BF16), kv)
            m_ref[r] = m_new
        return carry

    lax.fori_loop(0, n_chunks, chunk_body, 0)

    starts = [pl.multiple_of(jnp.maximum(i - (n_win_tiles - 1) + j, 0) * t, t) for j in range(n_win_tiles)]
    kvw = jnp.concatenate([win_ref[0, pl.ds(st, t), :] for st in starts], axis=0)
    slabs = [jnp.zeros((t, V7X_LANES), F32) for _ in range(B_GROUP_Q // V7X_LANES)]
    for r in range(B_REP):
        bias = jnp.concatenate([ttab_ref[g * B_REP + r, n_win_tiles - 1 - j] for j in range(n_win_tiles)], axis=1)
        s = _dot_nt(qms[r], kvw) * SCALE + bias
        p, _, den = _masked_softmax(s, mask_w)
        o_win = _dot(p.astype(BF16), kvw) / den
        o_sel = acc_ref[r] / l_ref[r]
        o_v = gate[:, B_REP + r:B_REP + r + 1] * o_sel + gate[:, 2 * B_REP + r:2 * B_REP + r + 1] * o_win
        o = gate[:, r:r + 1] * _pick_head(o_cmp[r], 0) + _align_head(_pick_head(o_v, 1), 1, 0)
        slabs[r // 2] = slabs[r // 2] + _align_head(o, 0, r % 2)
    for sidx, slab in enumerate(slabs):
        o_ref[0, :, sidx * V7X_LANES:(sidx + 1) * V7X_LANES] = slab


def _b_attn(q, gate, kvt, selg, wing, table, batch, seq):
    t = B_TILE
    n_chunks = seq // B_CMP_STRIDE
    n_sel = seq // B_SEL_BLOCK
    assert n_sel <= B_SEL_PAD and seq % B_SEL_CHUNK == 0
    tcmp = _cmp_bias_tiles(table, seq, n_chunks)
    ttab = _toeplitz_tiles(table)
    ovt = jnp.asarray(_overlap_t(B_SEL_PAD, n_chunks, n_sel, n_chunks - 1), BF16)
    whole = lambda a: pl.BlockSpec(a.shape, lambda b, g, i: (0,) * a.ndim)
    seq_block = pl.BlockSpec((1, seq, V7X_LANES), lambda b, g, i: (b, 0, g))
    o = pl.pallas_call(
        functools.partial(_b_attn_kernel, n_sel=n_sel), grid=(batch, B_KV_HEADS, seq // t),
        in_specs=[pl.BlockSpec((1, t, B_GROUP_Q), lambda b, g, i: (b, i, g)),
                  pl.BlockSpec((1, t, V7X_LANES), lambda b, g, i: (b, i, g)),
                  pl.BlockSpec((1, 2, 1, V7X_LANES, n_chunks), lambda b, g, i: (b, 0, g, 0, 0)),
                  pl.BlockSpec((1, B_REP, t, n_chunks), lambda b, g, i: (i, g, 0, 0)),
                  seq_block, seq_block, whole(ttab), whole(ovt)],
        out_specs=pl.BlockSpec((1, t, B_GROUP_Q), lambda b, g, i: (b, i, g)),
        out_shape=jax.ShapeDtypeStruct((batch, seq, B_GQ_COLS), F32),
        scratch_shapes=[pltpu.VMEM((B_REP, t, V7X_LANES), F32)] * 3 + [pltpu.VMEM((B_SEL_PAD, t), F32)],
        compiler_params=_params("arbitrary", "arbitrary", "arbitrary"), name="sparse_prompt_attention")(
            q.reshape(batch, seq, B_GQ_COLS), gate.reshape(batch, seq, B_GGATE_COLS), kvt, tcmp,
            selg.reshape(batch, seq, B_GKV_COLS), wing.reshape(batch, seq, B_GKV_COLS), ttab, ovt)
    return o.reshape(batch * seq, B_GQ_COLS)


B_STEP_SEL_PAD = 256
B_STEP_ROWS = 8


def _b_step_query(q, g, r):
    h, q_slab, q_half, _, kv_half = _b_head_layout(g, r)
    qs = q[:, q_slab * V7X_LANES:(q_slab + 1) * V7X_LANES]
    return _align_head(_pick_head(qs, q_half), q_half, kv_half)


def _b_step_cmp_kernel(q_ref, kvt_ref, bias_ref, ov_ref, oc_ref, ids_ref, *, n_cmp, n_sel, cur):
    rows = B_STEP_ROWS
    pad = B_STEP_SEL_PAD
    n_cmp_pad = kvt_ref.shape[3]
    q = jnp.broadcast_to(q_ref[0], (rows, B_Q_COLS))
    mask_c = lax.broadcasted_iota(I32, (rows, n_cmp_pad), 1) < n_cmp
    blk = lax.broadcasted_iota(I32, (1, pad), 1)
    forced = (blk == 0) | (blk == cur) | (blk == cur - 1)
    b_idx = lax.broadcasted_iota(I32, (pad, pad), 0)
    j_idx = lax.broadcasted_iota(I32, (pad, pad), 1)
    slot = lax.broadcasted_iota(I32, (pad, V7X_LANES), 1).astype(F32)
    b_val = lax.broadcasted_iota(I32, (pad, V7X_LANES), 0).astype(F32)
    slabs = [jnp.zeros((rows, V7X_LANES), F32) for _ in range(B_Q_COLS // V7X_LANES)]
    for g in range(B_KV_HEADS):
        _, _, _, kv_sl, kv_half = _b_head_layout(g, 0)
        kct = kvt_ref[0, 0, kv_sl, :]
        vct = kvt_ref[0, 1, kv_sl, :]
        p_sum = jnp.zeros((rows, n_cmp_pad), F32)
        for r in range(B_REP):
            h, q_slab, q_half, _, _ = _b_head_layout(g, r)
            s = _dot(_b_step_query(q, g, r), kct) * SCALE + bias_ref[h:h + 1, :]
            p, _, den = _masked_softmax(s, mask_c)
            p = p / den
            p_sum = p_sum + p
            o = _dot_nt(p.astype(BF16), vct)
            slabs[q_slab] = slabs[q_slab] + _align_head(_pick_head(o, kv_half), kv_half, q_half)
        hi, mid, lo3 = _split3(p_sum)
        imp = (_dot(hi, ov_ref[...]) + _dot(mid, ov_ref[...]) + _dot(lo3, ov_ref[...]))[0:1]
        imp = jnp.where(forced, B_FORCE, jnp.where(blk <= cur, imp, NEG_INF))
        imp = jnp.where(blk < n_sel, imp, B_PAD_SCORE)
        other = jnp.broadcast_to(imp, (pad, pad))
        mine = jnp.transpose(other)
        ahead = (other > mine) | ((other == mine) & (j_idx < b_idx))
        rank = jnp.sum(jnp.where(ahead, 1.0, 0.0), axis=1, keepdims=True)
        ids = jnp.sum(jnp.where(rank == slot, b_val, 0.0), axis=0, keepdims=True)
        ids_ref[0, g:g + 1, :] = ids.astype(I32)
    oc_ref[0] = jnp.concatenate(slabs, axis=1)[0:1]


def _b_step_sel_kernel(ids_ref, pt_ref, q_ref, gate_ref, oc_ref, seln_ref, winn_ref, wbuf_ref,
                       s0_ref, s1_ref, s2_ref, s3_ref, bblk_ref, bwin_ref, o_ref, m_ref, l_ref, acc_ref,
                       *, n_past_blocks):
    n = pl.program_id(0)
    kb = pl.program_id(1)
    rows = B_STEP_ROWS
    q = jnp.broadcast_to(q_ref[0], (rows, B_Q_COLS))
    seln = jnp.broadcast_to(seln_ref[0], (rows, B_SLAB))
    sel_blocks = (s0_ref, s1_ref, s2_ref, s3_ref)
    self_bias = bblk_ref[n_past_blocks]

    def v_slice(kv_sl):
        return slice(B_KV_COLS + kv_sl.start, B_KV_COLS + kv_sl.stop)

    def self_score(qm, k_new, h):
        return jnp.sum(qm.astype(F32) * k_new, axis=1, keepdims=True) * SCALE + self_bias[h:h + 1, 0:1]

    @pl.when(kb == 0)
    def _():
        for g in range(B_KV_HEADS):
            for r in range(B_REP):
                h, _, _, kv_sl, _ = _b_head_layout(g, r)
                qm = _b_step_query(q, g, r)
                m_ref[h] = jnp.broadcast_to(self_score(qm, seln[:, kv_sl], h), (rows, V7X_LANES))
                l_ref[h] = jnp.ones((rows, V7X_LANES), F32)
                acc_ref[h] = seln[:, v_slice(kv_sl)]

    for g in range(B_KV_HEADS):
        blkid = ids_ref[(n * B_KV_HEADS + g) * V7X_LANES + kb]
        valid = blkid < n_past_blocks
        bias_blk = bblk_ref[jnp.minimum(blkid, n_past_blocks)]
        data = sel_blocks[g][0]
        _, _, _, kv_sl, _ = _b_head_layout(g, 0)
        k = data[:, kv_sl].astype(BF16)
        v = data[:, v_slice(kv_sl)].astype(BF16)
        for r in range(B_REP):
            h = g * B_REP + r
            s = _dot_nt(_b_step_query(q, g, r), k) * SCALE + bias_blk[h:h + 1, :]
            s = jnp.where(valid, s, NEG_INF)
            m_old = m_ref[h]
            m_new = jnp.maximum(m_old, jnp.max(s, axis=1, keepdims=True))
            p = jnp.where(valid, jnp.exp(s - m_new[:, 0:1]), 0.0)
            alpha = jnp.exp(m_old - m_new)
            l_ref[h] = alpha * l_ref[h] + jnp.sum(p, axis=1, keepdims=True)
            acc_ref[h] = alpha * acc_ref[h] + _dot(p.astype(BF16), v)
            m_ref[h] = m_new

    @pl.when(kb == pl.num_programs(1) - 1)
    def _():
        gate = jnp.broadcast_to(gate_ref[0], (rows, B_TAIL))
        oc = jnp.broadcast_to(oc_ref[0], (rows, B_Q_COLS))
        winn = jnp.broadcast_to(winn_ref[0], (rows, B_SLAB))
        wb = wbuf_ref.shape[1]
        wlane = lax.broadcasted_iota(I32, (rows, wb), 1)
        wmask = (wb - wlane >= 0) & (wb - wlane < B_WINDOW)
        slabs = [jnp.zeros((rows, V7X_LANES), F32) for _ in range(B_Q_COLS // V7X_LANES)]
        for g in range(B_KV_HEADS):
            _, _, _, kv_sl, kv_half = _b_head_layout(g, 0)
            kw = wbuf_ref[0, :, kv_sl].astype(BF16)
            vw = wbuf_ref[0, :, v_slice(kv_sl)].astype(BF16)
            for r in range(B_REP):
                h, q_slab, q_half, _, _ = _b_head_layout(g, r)
                qm = _b_step_query(q, g, r)
                s = jnp.where(wmask, _dot_nt(qm, kw) * SCALE + bwin_ref[h:h + 1, :], NEG_INF)
                s_self = self_score(qm, winn[:, kv_sl], h)
                m = jnp.maximum(jnp.max(s, axis=1, keepdims=True), s_self)
                pb = jnp.where(wmask, jnp.exp(s - m), 0.0)
                ps = jnp.exp(s_self - m)
                den = jnp.sum(pb, axis=1, keepdims=True) + ps
                o_win = (_dot(pb.astype(BF16), vw) + ps * winn[:, v_slice(kv_sl)]) / den
                o_sel = acc_ref[h] / l_ref[h]
                o_kv = (gate[:, B_HEADS + h:B_HEADS + h + 1] * o_sel
                        + gate[:, 2 * B_HEADS + h:2 * B_HEADS + h + 1] * o_win)
                oc_slab = oc[:, q_slab * V7X_LANES:(q_slab + 1) * V7X_LANES]
                slabs[q_slab] = (slabs[q_slab] + _align_head(_pick_head(o_kv, kv_half), kv_half, q_half)
                                 + gate[:, h:h + 1] * _pick_head(oc_slab, q_half))
        o_ref[0] = jnp.concatenate(slabs, axis=1)[0:1]


def _b_step_attn(q, gate, kvt, sel_new, win_new, cache_sel, cache_win, page_table, table):
    n, n_pages = page_table.shape
    page = cache_sel.shape[1]
    past = n_pages * page
    wb = cache_win.shape[1]
    assert wb == B_WINDOW and past % B_SEL_BLOCK == 0 and page % B_SEL_BLOCK == 0
    n_cmp_pad = past // B_CMP_STRIDE
    n_cmp = (past + 1) // B_CMP_STRIDE - (B_CMP_BLOCK // B_CMP_STRIDE) + 1
    n_sel = -(-(past + 1) // B_SEL_BLOCK)
    n_past_blocks = past // B_SEL_BLOCK
    blocks_per_page = page // B_SEL_BLOCK
    assert n_sel <= B_STEP_SEL_PAD and n_cmp <= n_cmp_pad

    bias_c = _bias_of_dist(table, np.maximum(past - (B_CMP_STRIDE * np.arange(n_cmp_pad) + B_CMP_BLOCK - 1), 0))
    ov = jnp.asarray(_overlap_t(B_STEP_SEL_PAD, n_cmp_pad, n_sel, n_cmp).T, BF16)
    row = lambda cols: pl.BlockSpec((1, 1, cols), lambda b: (b, 0, 0))
    oc, ids = pl.pallas_call(
        functools.partial(_b_step_cmp_kernel, n_cmp=n_cmp, n_sel=n_sel, cur=past // B_SEL_BLOCK), grid=(n,),
        in_specs=[row(B_Q_COLS), pl.BlockSpec((1, 2, B_KV_COLS, n_cmp_pad), lambda b: (b, 0, 0, 0)),
                  pl.BlockSpec(bias_c.shape, lambda b: (0, 0)), pl.BlockSpec(ov.shape, lambda b: (0, 0))],
        out_specs=[row(B_Q_COLS), pl.BlockSpec((1, B_KV_HEADS, V7X_LANES), lambda b: (b, 0, 0))],
        out_shape=[jax.ShapeDtypeStruct((n, 1, B_Q_COLS), F32),
                   jax.ShapeDtypeStruct((n, B_KV_HEADS, V7X_LANES), I32)],
        compiler_params=_params("arbitrary"), name="sparse_step_compressed")(
            q.reshape(n, 1, B_Q_COLS), kvt, bias_c, ov)

    blk_pos = B_SEL_BLOCK * np.arange(n_past_blocks + 1)[:, None] + np.arange(B_SEL_BLOCK)[None, :]
    bblk = _bias_of_dist(table, np.maximum(past - blk_pos, 0)).transpose(1, 0, 2)
    bwin = _bias_of_dist(table, wb - np.arange(wb))
    srow = lambda cols: pl.BlockSpec((1, 1, cols), lambda b, kb, ids, pt: (b, 0, 0))

    def sel_spec(g):
        def index(b, kb, ids, pt):
            blk = jnp.minimum(ids[(b * B_KV_HEADS + g) * V7X_LANES + kb], n_past_blocks - 1)
            return (pt[b, blk // blocks_per_page] * blocks_per_page + blk % blocks_per_page, 0, 0)
        return pl.BlockSpec((1, B_SEL_BLOCK, B_SLAB), index)

    sel_view = cache_sel.reshape(-1, B_SEL_BLOCK, B_SLAB)
    o = pl.pallas_call(
        functools.partial(_b_step_sel_kernel, n_past_blocks=n_past_blocks),
        grid_spec=pltpu.PrefetchScalarGridSpec(
            num_scalar_prefetch=2, grid=(n, B_TOPN),
            in_specs=[srow(B_Q_COLS), srow(B_TAIL), srow(B_Q_COLS), srow(B_SLAB), srow(B_SLAB),
                      pl.BlockSpec((1, wb, B_SLAB), lambda b, kb, ids, pt: (b, 0, 0))]
            + [sel_spec(g) for g in range(B_KV_HEADS)]
            + [pl.BlockSpec(bblk.shape, lambda b, kb, ids, pt: (0, 0, 0)),
               pl.BlockSpec(bwin.shape, lambda b, kb, ids, pt: (0, 0))],
            out_specs=srow(B_Q_COLS),
            scratch_shapes=[pltpu.VMEM((B_HEADS, B_STEP_ROWS, V7X_LANES), F32)] * 3),
        out_shape=jax.ShapeDtypeStruct((n, 1, B_Q_COLS), F32),
        compiler_params=_params("arbitrary", "arbitrary"), name="sparse_step_selected")(
            ids.reshape(-1), page_table, q.reshape(n, 1, B_Q_COLS), gate.reshape(n, 1, B_TAIL), oc,
            sel_new.reshape(n, 1, B_SLAB), win_new.reshape(n, 1, B_SLAB), cache_win.reshape(n, wb, B_SLAB),
            sel_view, sel_view, sel_view, sel_view, bblk, bwin)
    return o.reshape(n, B_Q_COLS)


def _mixer_b(xp, xs, g, w_in, gate_bias, phi_pos, phi_w1, phi_w2, w_out, table, caches, page_table, batch, seq):
    cache_cmp, cache_sel, cache_win = caches
    five = lambda a, rows: a.reshape(-1, rows, 2, B_KV_HEADS, HEAD_DIM)

    w, w_tail, bias = _b_grouped_weights(w_in, gate_bias)
    q, cmp, sel, win, selg, wing, gate = _b_proj_grouped(xp, g, w, w_tail, bias)
    kvt = _b_compress_prompt(cmp, _compress_weights(phi_pos, phi_w1, phi_w2, True), batch, seq)
    o = _b_attn(q, gate, kvt, selg, wing, table, batch, seq)
    w_out_grouped = jnp.pad(w_out.reshape(B_KV_HEADS, B_REP * HEAD_DIM, D_MODEL),
                            ((0, 0), (0, B_GROUP_Q - B_REP * HEAD_DIM), (0, 0))).reshape(B_GQ_COLS, D_MODEL)
    yp = _out_proj(xp, o, w_out_grouped.astype(BF16))
    keep = min(B_WINDOW, seq)
    state = [five(cmp, seq), five(sel, seq), five(win, seq)[:, seq - keep:]]

    n_main = B_Q_COLS + 3 * B_SLAB
    w = w_in[:, :n_main].astype(BF16)
    w_tail = jnp.pad(w_in[:, n_main:], ((0, 0), (0, B_TAIL - B_GATE_COLS))).astype(BF16)
    bias = jnp.pad(gate_bias, (0, B_TAIL - B_GATE_COLS)).reshape(1, B_TAIL)
    sq, scmp, ssel, swin, sgate = _b_proj(xs, g, w, w_tail, bias)
    skvt = _b_compress_step(cache_cmp, page_table, _compress_weights(phi_pos, phi_w1, phi_w2, False))
    so = _b_step_attn(sq, sgate, skvt, ssel, swin, cache_sel, cache_win, page_table, table)
    ys = _out_proj(xs, so, w_out.astype(BF16))
    wb = cache_win.shape[1]
    win_all = jnp.concatenate([cache_win, five(swin, 1)], axis=1)
    state += [five(scmp, 1), five(ssel, 1), win_all[:, wb + 1 - min(B_WINDOW, wb + 1):]]
    return yp, ys, state


def kernel(x_prompt, x_sample, cache_l0_w128, cache_l0_w512, cache_l0_w2048, cache_l1_cmp, cache_l1_sel, cache_l1_win, cache_l2_k, cache_l2_v, cache_l2_logf, cache_l3_w128, cache_l3_w512, cache_l3_w2048, page_table, norm_g, ffn_w_gate, ffn_w_up, ffn_w_down, final_norm_g, rel_bias_table, a_w_in, a_w_out, b_w_in, b_gate_bias, b_phi_pos, b_phi_w1, b_phi_w2, b_w_out, c_w_in, c_forget_bias, c_w_out):
    batch, seq, _ = x_prompt.shape
    depth = norm_g.shape[0]
    layer_caches = ((cache_l0_w128, cache_l0_w512, cache_l0_w2048), (cache_l1_cmp, cache_l1_sel, cache_l1_win),
                    (cache_l2_k, cache_l2_v, cache_l2_logf), (cache_l3_w128, cache_l3_w512, cache_l3_w2048))
    xp = x_prompt.reshape(batch * seq, D_MODEL)
    xs = x_sample.reshape(-1, D_MODEL)
    table = rel_bias_table
    new_state = []
    for i in range(depth):
        kind, j = i % N_MIXERS, i // N_MIXERS
        last = i == depth - 1
        f1 = (norm_g[i, 0], ffn_w_gate[i, 0].astype(BF16), ffn_w_up[i, 0].astype(BF16), ffn_w_down[i, 0].astype(BF16))
        f2 = (norm_g[i, 2], ffn_w_gate[i, 1].astype(BF16), ffn_w_up[i, 1].astype(BF16), ffn_w_down[i, 1].astype(BF16))
        xp, xs = _ffn(xp, *f1), _ffn(xs, *f1)
        if kind == 0:
            xp, xs, state = _mixer_a(xp, xs, norm_g[i, 1], a_w_in[j], a_w_out[j], table, layer_caches[i], batch, seq)
        elif kind == 1:
            xp, xs, state = _mixer_b(xp, xs, norm_g[i, 1], b_w_in[j], b_gate_bias[j], b_phi_pos[j], b_phi_w1[j],
                                     b_phi_w2[j], b_w_out[j], table, layer_caches[i], page_table, batch, seq)
        else:
            xp, xs, state = _mixer_c(xp, xs, norm_g[i, 1], c_w_in[j], c_forget_bias[j], c_w_out[j],
                                     layer_caches[i], page_table, batch, seq)
        new_state.extend(state)
        final_g = final_norm_g if last else None
        xp, xs = _ffn(xp, *f2, final_g), _ffn(xs, *f2, final_g)
    return (xp.reshape(batch, seq, D_MODEL), xs.reshape(-1, 1, D_MODEL), *new_state)
```

```python
import functools
import math

import numpy as np
import jax
import jax.numpy as jnp
from jax import lax
from jax.experimental import pallas as pl
from jax.experimental.pallas import tpu as pltpu

F32 = jnp.float32
BF16 = jnp.bfloat16
I32 = jnp.int32

D_MODEL = 1024
HEAD_DIM = 64
D_FF = 2816
RMS_EPS = 1e-6
NEG_INF = -1e30
SCALE = HEAD_DIM ** -0.5
N_MIXERS = 3

NUM_BUCKETS = 32
REL_MAX_DIST = 2048

A_GROUPS = ((128, 1), (512, 4), (2048, 16))
A_GROUP_HEADS = 4
A_GROUP_COLS = A_GROUP_HEADS * HEAD_DIM
A_HEADS = A_GROUP_HEADS * len(A_GROUPS)
A_BAND = 128

B_HEADS = 12
B_KV_HEADS = 4
B_REP = B_HEADS // B_KV_HEADS
B_CMP_BLOCK = 32
B_CMP_STRIDE = 16
B_SEL_BLOCK = 64
B_TOPN = 16
B_WINDOW = 512
B_PHI_HIDDEN = 128
B_FORCE = 1e4
B_Q_COLS = B_HEADS * HEAD_DIM
B_KV_COLS = B_KV_HEADS * HEAD_DIM
B_GATE_COLS = 3 * B_HEADS

C_HEADS = 16
C_COLS = C_HEADS * HEAD_DIM

V7X_LANES = 128
V7X_VMEM_LIMIT_BYTES = 56 * 1024 * 1024
ROW_TILE = 512
FF_CHUNK = 256
HALF = HEAD_DIM


def _params(*sem):
    return pltpu.CompilerParams(dimension_semantics=sem, vmem_limit_bytes=V7X_VMEM_LIMIT_BYTES)


def _dot(a, b):
    return jnp.dot(a, b, preferred_element_type=F32)


def _dot_nt(a, b):
    return lax.dot_general(a, b, (((1,), (1,)), ((), ())), preferred_element_type=F32)


def _split3(x):
    hi = x.astype(BF16)
    r1 = x - hi.astype(F32)
    mid = r1.astype(BF16)
    lo = (r1 - mid.astype(F32)).astype(BF16)
    return hi, mid, lo


def _dot3(a_bf16_exact, x):
    hi, mid, lo = _split3(x)
    return _dot(a_bf16_exact, hi) + _dot(a_bf16_exact, mid) + _dot(a_bf16_exact, lo)


def _rms(x, g):
    return x * lax.rsqrt(jnp.mean(x * x, axis=-1, keepdims=True) + RMS_EPS) * g


def _lane_lo(shape):
    return (lax.broadcasted_iota(I32, shape, len(shape) - 1) % V7X_LANES) < HALF


def _pick_head(slab, half):
    lo = _lane_lo(slab.shape)
    return jnp.where(lo if half == 0 else jnp.logical_not(lo), slab, jnp.zeros_like(slab))


def _align_head(slab, src_half, dst_half):
    if src_half == dst_half:
        return slab
    return pltpu.roll(slab, HALF, 1)


def _masked_softmax(s, mask):
    s = jnp.where(mask, s, NEG_INF)
    m = jnp.max(s, axis=-1, keepdims=True)
    p = jnp.where(mask, jnp.exp(s - m), 0.0)
    den = jnp.sum(p, axis=-1, keepdims=True)
    return p, m, jnp.where(den > 0, den, 1.0)


def _rel_bucket_np(dist):
    exact = NUM_BUCKETS // 2
    d = np.maximum(dist, 0)
    logd = (np.log(np.maximum(d, 1).astype(np.float32) / np.float32(exact))
            / np.float32(math.log(REL_MAX_DIST / exact))).astype(np.float32)
    far = np.minimum(exact + (logd * np.float32(NUM_BUCKETS - exact)).astype(np.int32), NUM_BUCKETS - 1)
    return np.where(d < exact, d, far).astype(np.int32)


def _bucket_starts():
    b = _rel_bucket_np(np.arange(2 * REL_MAX_DIST + 1))
    assert np.all(np.diff(b) >= 0) and b[-1] == NUM_BUCKETS - 1
    return [int(np.argmax(b >= k)) for k in range(NUM_BUCKETS)]


_BUCKET_STARTS = _bucket_starts()


def _bias_of_dist(table, dist, head_axis=0):
    d = jnp.expand_dims(jnp.asarray(dist, I32), head_axis)
    t = table.astype(F32)
    shape = [1] * d.ndim
    shape[head_axis] = t.shape[1]
    full = tuple(t.shape[1] if a == head_axis else n for a, n in enumerate(d.shape))
    out = jnp.broadcast_to(t[0].reshape(shape), full)
    for k in range(1, NUM_BUCKETS):
        out = jnp.where(d >= _BUCKET_STARTS[k], t[k].reshape(shape), out)
    return out


def _col_dot(kt, q_col):
    return jnp.sum(kt * q_col, axis=0, keepdims=True)


def _row_max(x):
    return jnp.max(x, axis=-1, keepdims=True)


def _row_sum(x):
    return jnp.sum(x, axis=-1, keepdims=True)


def _ffn_kernel(x_ref, g_ref, wg_ref, wu_ref, wd_ref, *rest, final):
    x = x_ref[...]
    h = _rms(x, g_ref[...]).astype(BF16)
    acc = jnp.zeros_like(x)
    for c in range(D_FF // FF_CHUNK):
        sl = slice(c * FF_CHUNK, (c + 1) * FF_CHUNK)
        gate = _dot(h, wg_ref[:, sl])
        up = _dot(h, wu_ref[:, sl])
        act = (gate * jax.nn.sigmoid(gate) * up).astype(BF16)
        acc = acc + _dot(act, wd_ref[sl, :])
    y = x + 0.5 * acc
    if final:
        gf_ref, o_ref = rest
        o_ref[...] = _rms(y, gf_ref[...])
    else:
        (o_ref,) = rest
        o_ref[...] = y


def _row_tile(rows):
    return ROW_TILE if rows % ROW_TILE == 0 else rows


def _ffn(x, g, wg, wu, wd, final_g=None):
    rows = x.shape[0]
    tm = _row_tile(rows)
    row = pl.BlockSpec((tm, D_MODEL), lambda i: (i, 0))
    vec = pl.BlockSpec((1, D_MODEL), lambda i: (0, 0))
    whole = lambda a: pl.BlockSpec(a.shape, lambda i: (0,) * a.ndim)
    args = [x, g.reshape(1, D_MODEL), wg, wu, wd]
    specs = [row, vec, whole(wg), whole(wu), whole(wd)]
    if final_g is not None:
        args.append(final_g.reshape(1, D_MODEL))
        specs.append(vec)
    return pl.pallas_call(
        functools.partial(_ffn_kernel, final=final_g is not None),
        grid=(rows // tm,), in_specs=specs, out_specs=row,
        out_shape=jax.ShapeDtypeStruct(x.shape, F32),
        compiler_params=_params("arbitrary"), name="macaron_swiglu")(*args)


def _out_proj_kernel(x_ref, o_ref, w_ref, y_ref):
    y_ref[...] = x_ref[...] + _dot(o_ref[...].astype(BF16), w_ref[...])


def _out_proj(x, o, w):
    rows = x.shape[0]
    tm = _row_tile(rows)
    cols = o.shape[1]
    return pl.pallas_call(
        _out_proj_kernel, grid=(rows // tm,),
        in_specs=[pl.BlockSpec((tm, D_MODEL), lambda i: (i, 0)),
                  pl.BlockSpec((tm, cols), lambda i: (i, 0)),
                  pl.BlockSpec(w.shape, lambda i: (0, 0))],
        out_specs=pl.BlockSpec((tm, D_MODEL), lambda i: (i, 0)),
        out_shape=jax.ShapeDtypeStruct(x.shape, F32),
        compiler_params=_params("arbitrary"), name="mixer_out_proj")(x, o, w)


def _a_proj_kernel(x_ref, g_ref, w_ref, q0_ref, q1_ref, q2_ref, kv0_ref, kv1_ref, kv2_ref):
    h = _rms(x_ref[...], g_ref[...]).astype(BF16)
    nq = A_HEADS * HEAD_DIM
    for gi, (q_ref, kv_ref) in enumerate(((q0_ref, kv0_ref), (q1_ref, kv1_ref), (q2_ref, kv2_ref))):
        c0 = gi * A_GROUP_COLS
        q_ref[...] = _dot(h, w_ref[:, c0:c0 + A_GROUP_COLS]).astype(BF16)
        kv_ref[:, :A_GROUP_COLS] = _dot(h, w_ref[:, nq + c0:nq + c0 + A_GROUP_COLS])
        kv_ref[:, A_GROUP_COLS:] = _dot(h, w_ref[:, 2 * nq + c0:2 * nq + c0 + A_GROUP_COLS])


def _a_proj(x, g, w):
    rows = x.shape[0]
    tm = _row_tile(rows)
    qspec = pl.BlockSpec((tm, A_GROUP_COLS), lambda i: (i, 0))
    kvspec = pl.BlockSpec((tm, 2 * A_GROUP_COLS), lambda i: (i, 0))
    return pl.pallas_call(
        _a_proj_kernel, grid=(rows // tm,),
        in_specs=[pl.BlockSpec((tm, D_MODEL), lambda i: (i, 0)),
                  pl.BlockSpec((1, D_MODEL), lambda i: (0, 0)),
                  pl.BlockSpec(w.shape, lambda i: (0, 0))],
        out_specs=[qspec] * 3 + [kvspec] * 3,
        out_shape=[jax.ShapeDtypeStruct((rows, A_GROUP_COLS), BF16)] * 3
        + [jax.ShapeDtypeStruct((rows, 2 * A_GROUP_COLS), F32)] * 3,
        compiler_params=_params("arbitrary"), name="dilated_in_proj")(x, g.reshape(1, D_MODEL), w)


def _a_attn_kernel(q_ref, kvc_ref, kvp_ref, bias_ref, o_ref, l_ref):
    blk = pl.program_id(2)
    band = A_BAND
    q = q_ref[0]
    kvc = kvc_ref[0]
    kvp = kvp_ref[0]
    qi = lax.broadcasted_iota(I32, (band, 2 * band), 0)
    kj = lax.broadcasted_iota(I32, (band, 2 * band), 1)
    off = qi + band - kj
    mask = (off >= 0) & (off <= band) & ((kj >= band) | (blk > 0))
    lo = _lane_lo((band, V7X_LANES))
    for hp in range(A_GROUP_HEADS // 2):
        ksl = slice(hp * V7X_LANES, (hp + 1) * V7X_LANES)
        vsl = slice(A_GROUP_COLS + hp * V7X_LANES, A_GROUP_COLS + (hp + 1) * V7X_LANES)
        k = jnp.concatenate([kvp[:, ksl], kvc[:, ksl]], axis=0).astype(BF16)
        v = jnp.concatenate([kvp[:, vsl], kvc[:, vsl]], axis=0).astype(BF16)
        qs = q[:, ksl]
        outs, lses = [], []
        for half in range(2):
            s = _dot_nt(_pick_head(qs, half), k) * SCALE + bias_ref[2 * hp + half]
            p, m, den = _masked_softmax(s, mask)
            outs.append(_dot(p.astype(BF16), v) / den)
            lses.append(jnp.broadcast_to(m + jnp.log(den), (band, V7X_LANES)))
        o_ref[0, :, ksl] = jnp.where(lo, outs[0], outs[1])
        l_ref[0, :, ksl] = jnp.where(lo, lses[0], lses[1])


def _a_attn_group(q, kv, bias, batch, seq, dil):
    sub = seq // dil
    nb = sub // A_BAND
    qv = q.reshape(batch, sub, dil * A_GROUP_COLS)
    kvv = kv.reshape(batch, sub, dil * 2 * A_GROUP_COLS)
    qspec = pl.BlockSpec((1, A_BAND, A_GROUP_COLS), lambda b, r, i: (b, i, r))
    o, lse = pl.pallas_call(
        _a_attn_kernel, grid=(batch, dil, nb),
        in_specs=[qspec,
                  pl.BlockSpec((1, A_BAND, 2 * A_GROUP_COLS), lambda b, r, i: (b, i, r)),
                  pl.BlockSpec((1, A_BAND, 2 * A_GROUP_COLS), lambda b, r, i: (b, jnp.maximum(i - 1, 0), r)),
                  pl.BlockSpec(bias.shape, lambda b, r, i: (0, 0, 0))],
        out_specs=[qspec, qspec],
        out_shape=[jax.ShapeDtypeStruct(qv.shape, F32)] * 2,
        compiler_params=_params("arbitrary", "arbitrary", "arbitrary"), name="dilated_band_attention")(
            qv, kvv, kvv, bias)
    return o.reshape(batch * seq, A_GROUP_COLS), lse.reshape(batch * seq, A_GROUP_COLS)


def _a_band_bias(table, g, dil):
    off = np.arange(A_BAND)[:, None] + A_BAND - np.arange(2 * A_BAND)[None, :]
    dist = np.clip(off, 0, A_BAND) * dil
    return _bias_of_dist(table[:, g * A_GROUP_HEADS:(g + 1) * A_GROUP_HEADS], dist)


def _a_out_kernel(x_ref, o0_ref, o1_ref, o2_ref, l0_ref, l1_ref, l2_ref, w_ref, y_ref):
    ls = [l0_ref[...], l1_ref[...], l2_ref[...]]
    m = jnp.maximum(jnp.maximum(ls[0], ls[1]), ls[2])
    es = [jnp.exp(l - m) for l in ls]
    den = es[0] + es[1] + es[2]
    y = x_ref[...]
    for gi, o_ref in enumerate((o0_ref, o1_ref, o2_ref)):
        og = (o_ref[...] * (es[gi] / den)).astype(BF16)
        y = y + _dot(og, w_ref[gi * A_GROUP_COLS:(gi + 1) * A_GROUP_COLS, :])
    y_ref[...] = y


def _a_out(x, outs, lses, w):
    rows = x.shape[0]
    tm = _row_tile(rows)
    gspec = pl.BlockSpec((tm, A_GROUP_COLS), lambda i: (i, 0))
    xspec = pl.BlockSpec((tm, D_MODEL), lambda i: (i, 0))
    return pl.pallas_call(
        _a_out_kernel, grid=(rows // tm,),
        in_specs=[xspec] + [gspec] * 6 + [pl.BlockSpec(w.shape, lambda i: (0, 0))],
        out_specs=xspec, out_shape=jax.ShapeDtypeStruct(x.shape, F32),
        compiler_params=_params("arbitrary"), name="dilated_combine_out_proj")(x, *outs, *lses, w)


def _a_step_kernel(q0_ref, q1_ref, q2_ref, n0_ref, n1_ref, n2_ref, b0_ref, b1_ref, b2_ref,
                   bias0_ref, bias1_ref, bias2_ref, self_ref, o_ref):
    groups = ((q0_ref, n0_ref, b0_ref, bias0_ref), (q1_ref, n1_ref, b1_ref, bias1_ref),
              (q2_ref, n2_ref, b2_ref, bias2_ref))
    outs, lses = [], []
    for gi, (q_ref, n_ref, b_ref, bias_ref) in enumerate(groups):
        dil = A_GROUPS[gi][1]
        win = b_ref.shape[4]
        mask = lax.broadcasted_iota(I32, (1, win), 1) % dil == 0
        q = q_ref[0].astype(F32)
        new = n_ref[0]
        for j in range(A_GROUP_HEADS):
            head = gi * A_GROUP_HEADS + j
            rows = slice(j * HEAD_DIM, (j + 1) * HEAD_DIM)
            v_rows = slice(A_GROUP_COLS + j * HEAD_DIM, A_GROUP_COLS + (j + 1) * HEAD_DIM)
            s = jnp.where(mask, _col_dot(b_ref[0, 0, j], q[rows]) * SCALE + bias_ref[j:j + 1, :], NEG_INF)
            s_self = _col_dot(new[rows], q[rows]) * SCALE + self_ref[head:head + 1, 0:1]
            m = jnp.maximum(_row_max(s), s_self)
            pb = jnp.where(mask, jnp.exp(s - m), 0.0)
            ps = jnp.exp(s_self - m)
            den = _row_sum(pb) + ps
            outs.append((_row_sum(b_ref[0, 1, j] * pb) + ps * new[v_rows]) / den)
            lses.append(m + jnp.log(den))
    n_groups = len(groups)
    for j in range(A_GROUP_HEADS):
        ls = [lses[gi * A_GROUP_HEADS + j] for gi in range(n_groups)]
        m = functools.reduce(jnp.maximum, ls)
        es = [jnp.exp(l - m) for l in ls]
        den = functools.reduce(lambda a, b: a + b, es)
        for gi in range(n_groups):
            head = gi * A_GROUP_HEADS + j
            o_ref[0, head * HEAD_DIM:(head + 1) * HEAD_DIM, :] = outs[head] * (es[gi] / den)


def _a_step_attn(qs, news, bufs, table):
    n = qs[0].shape[0]
    cols = lambda a: a.reshape(n, a.shape[1], 1)
    views = [b.transpose(0, 2, 3, 4, 1) for b in bufs]
    biases = []
    for g, (win, _) in enumerate(A_GROUPS):
        assert bufs[g].shape[1] == win, "the step kernel reads a full window buffer"
        biases.append(_bias_of_dist(table[:, g * A_GROUP_HEADS:(g + 1) * A_GROUP_HEADS], win - np.arange(win)))
    self_bias = jnp.broadcast_to(_bias_of_dist(table, np.zeros((1,), np.int64)), (A_HEADS, V7X_LANES))
    col_spec = lambda c: pl.BlockSpec((1, c, 1), lambda b: (b, 0, 0))
    whole = lambda a: pl.BlockSpec(a.shape, lambda b: (0,) * a.ndim)
    o = pl.pallas_call(
        _a_step_kernel, grid=(n,),
        in_specs=[col_spec(A_GROUP_COLS)] * 3 + [col_spec(2 * A_GROUP_COLS)] * 3
        + [pl.BlockSpec((1,) + v.shape[1:], lambda b: (b, 0, 0, 0, 0)) for v in views]
        + [whole(b) for b in biases] + [whole(self_bias)],
        out_specs=col_spec(A_HEADS * HEAD_DIM),
        out_shape=jax.ShapeDtypeStruct((n, A_HEADS * HEAD_DIM, 1), F32),
        compiler_params=_params("arbitrary"), name="dilated_step_attention")(
            *[cols(q) for q in qs], *[cols(x) for x in news], *views, *biases, self_bias)
    return o.reshape(n, A_HEADS * HEAD_DIM)


def _mixer_a(xp, xs, g, w_in, w_out, table, caches, batch, seq):
    w_in = w_in.astype(BF16)
    w_out = w_out.astype(BF16)
    pq0, pq1, pq2, pkv0, pkv1, pkv2 = _a_proj(xp, g, w_in)
    outs, lses, state = [], [], []
    for gi, ((win, dil), q, kv) in enumerate(zip(A_GROUPS, (pq0, pq1, pq2), (pkv0, pkv1, pkv2))):
        o, l = _a_attn_group(q, kv, _a_band_bias(table, gi, dil), batch, seq, dil)
        outs.append(o)
        lses.append(l)
        keep = min(win, seq)
        state.append(kv.reshape(batch, seq, 2, A_GROUP_HEADS, HEAD_DIM)[:, seq - keep:])
    yp = _a_out(xp, outs, lses, w_out)

    n = xs.shape[0]
    sq0, sq1, sq2, skv0, skv1, skv2 = _a_proj(xs, g, w_in)
    so = _a_step_attn((sq0, sq1, sq2), (skv0, skv1, skv2), caches, table)
    ys = _out_proj(xs, so, w_out)
    for (win, _), kv, buf in zip(A_GROUPS, (skv0, skv1, skv2), caches):
        new = kv.reshape(n, 1, 2, A_GROUP_HEADS, HEAD_DIM)
        state.append(jnp.concatenate([buf, new], axis=1)[:, buf.shape[1] + 1 - min(win, buf.shape[1] + 1):])
    return yp, ys, state


C_TAIL = V7X_LANES


def _log_sigmoid(z):
    return -(jnp.maximum(-z, 0.0) + jnp.log1p(jnp.exp(-jnp.abs(z))))


def _c_proj_kernel(x_ref, g_ref, w_ref, wt_ref, b_ref, tri_ref, q_ref, k_ref, v_ref, lf_ref, dc_ref, carry_ref,
                   *, tiles_per_seq):
    i = pl.program_id(0)
    h = _rms(x_ref[...], g_ref[...]).astype(BF16)
    q_ref[...] = _dot(h, w_ref[:, 0:C_COLS]).astype(BF16)
    k_ref[...] = _dot(h, w_ref[:, C_COLS:2 * C_COLS])
    v_ref[...] = _dot(h, w_ref[:, 2 * C_COLS:3 * C_COLS])
    logf = _log_sigmoid(_dot(h, wt_ref[...]) + b_ref[...])
    lf_ref[...] = logf

    @pl.when(i % tiles_per_seq == 0)
    def _():
        carry_ref[...] = jnp.zeros_like(carry_ref)

    cum = _dot3(tri_ref[...], logf) + carry_ref[0:1, :]
    dc_ref[...] = cum
    carry_ref[...] = jnp.broadcast_to(cum[cum.shape[0] - 1:, :], carry_ref.shape)


def _c_proj(x, g, w, w_tail, bias, seq):
    rows = x.shape[0]
    tm = _row_tile(rows)
    tiles_per_seq = max(seq // tm, 1)
    tri = jnp.asarray(np.tril(np.ones((tm, tm), np.float32)), BF16)
    big = pl.BlockSpec((tm, C_COLS), lambda i: (i, 0))
    small = pl.BlockSpec((tm, C_TAIL), lambda i: (i, 0))
    return pl.pallas_call(
        functools.partial(_c_proj_kernel, tiles_per_seq=tiles_per_seq), grid=(rows // tm,),
        in_specs=[pl.BlockSpec((tm, D_MODEL), lambda i: (i, 0)),
                  pl.BlockSpec((1, D_MODEL), lambda i: (0, 0)),
                  pl.BlockSpec(w.shape, lambda i: (0, 0)),
                  pl.BlockSpec(w_tail.shape, lambda i: (0, 0)),
                  pl.BlockSpec((1, C_TAIL), lambda i: (0, 0)),
                  pl.BlockSpec(tri.shape, lambda i: (0, 0))],
        out_specs=[big, big, big, small, small],
        out_shape=[jax.ShapeDtypeStruct((rows, C_COLS), BF16)] + [jax.ShapeDtypeStruct((rows, C_COLS), F32)] * 2
        + [jax.ShapeDtypeStruct((rows, C_TAIL), F32)] * 2,
        scratch_shapes=[pltpu.VMEM((8, C_TAIL), F32)],
        compiler_params=_params("arbitrary"), name="forget_in_proj")(
            x, g.reshape(1, D_MODEL), w, w_tail, bias, tri)


C_TILE = 512


def _c_attn_kernel(q_ref, k_ref, v_ref, dq_ref, dk_ref, o_ref, m_ref, l_ref, acc_ref):
    hp = pl.program_id(1)
    i = pl.program_id(2)
    t = C_TILE
    q = q_ref[0]
    dq_tile = dq_ref[0]
    lane = lax.broadcasted_iota(I32, (t, V7X_LANES), 1)
    lo = lane < HALF
    qm = [_pick_head(q, 0), _pick_head(q, 1)]
    dq = [jnp.sum(jnp.where(lane == 2 * hp + half, dq_tile, 0.0), axis=1, keepdims=True) for half in range(2)]
    causal = lax.broadcasted_iota(I32, (t, t), 0) >= lax.broadcasted_iota(I32, (t, t), 1)
    m_ref[...] = jnp.full_like(m_ref, NEG_INF)
    l_ref[...] = jnp.zeros_like(l_ref)
    acc_ref[...] = jnp.zeros_like(acc_ref)

    def tile(c, diagonal):
        start = pl.multiple_of(c * t, t)
        k = k_ref[0, pl.ds(start, t), :].astype(BF16)
        v = v_ref[0, pl.ds(start, t), :].astype(BF16)
        dk = dk_ref[0, 0, c]
        for half in range(2):
            s = _dot_nt(qm[half], k) * SCALE + dq[half] - dk[half:half + 1, :]
            if diagonal:
                s = jnp.where(causal, s, NEG_INF)
            m_old = m_ref[half]
            m_new = jnp.maximum(m_old, jnp.max(s, axis=1, keepdims=True))
            p = jnp.exp(s - m_new[:, 0:1])
            if diagonal:
                p = jnp.where(causal, p, 0.0)
            alpha = jnp.exp(m_old - m_new)
            l_ref[half] = alpha * l_ref[half] + jnp.sum(p, axis=1, keepdims=True)
            acc_ref[half] = alpha * acc_ref[half] + _dot(p.astype(BF16), v)
            m_ref[half] = m_new

    def body(c, carry):
        tile(c, False)
        return carry

    lax.fori_loop(0, i, body, 0)
    tile(i, True)
    o_ref[0] = jnp.where(lo, acc_ref[0] / l_ref[0], acc_ref[1] / l_ref[1])


def _c_attn(q, k, v, dcum, batch, seq):
    t = C_TILE
    nt = seq // t
    pairs = C_HEADS // 2
    q3, k3, v3 = (a.reshape(batch, seq, C_COLS) for a in (q, k, v))
    dq = dcum.reshape(batch, seq, C_TAIL)
    dk = dq[:, :, :C_HEADS].transpose(0, 2, 1).reshape(batch, pairs, 2, nt, t).transpose(0, 1, 3, 2, 4)
    qspec = pl.BlockSpec((1, t, V7X_LANES), lambda b, h, i: (b, i, h))
    kspec = pl.BlockSpec((1, seq, V7X_LANES), lambda b, h, i: (b, 0, h))
    o = pl.pallas_call(
        _c_attn_kernel, grid=(batch, pairs, nt),
        in_specs=[qspec, kspec, kspec,
                  pl.BlockSpec((1, t, C_TAIL), lambda b, h, i: (b, i, 0)),
                  pl.BlockSpec((1, 1, nt, 2, t), lambda b, h, i: (b, h, 0, 0, 0))],
        out_specs=qspec, out_shape=jax.ShapeDtypeStruct((batch, seq, C_COLS), F32),
        scratch_shapes=[pltpu.VMEM((2, t, V7X_LANES), F32)] * 3,
        compiler_params=_params("arbitrary", "arbitrary", "arbitrary"), name="forget_attention")(
            q3, k3, v3, dq, dk)
    return o.reshape(batch * seq, C_COLS)


C_STEP_PAGES = 2


def _c_step_kernel(pt_ref, q_ref, kn_ref, vn_ref, lfn_ref, *rest):
    page_refs = rest[:3 * C_STEP_PAGES]
    lower_ref, o_ref, qb_ref, m_ref, l_ref, acc_ref, carry_ref = rest[3 * C_STEP_PAGES:]
    j = pl.program_id(1)
    page = qb_ref.shape[1]
    heads = (C_HEADS, HEAD_DIM)

    def per_head(x):
        return jnp.broadcast_to(x[:, None, :], heads + (x.shape[1],)).reshape(C_COLS, x.shape[1])

    @pl.when(j == 0)
    def _():
        q = q_ref[0].astype(F32)
        lane0 = lax.broadcasted_iota(I32, (C_COLS, page), 1) == 0
        qb_ref[...] = jnp.broadcast_to(q, (C_COLS, page))
        s_self = jnp.sum((kn_ref[0] * q).reshape(heads + (1,)), axis=1) * SCALE
        m_ref[...] = jnp.broadcast_to(s_self, (C_HEADS, page))
        l_ref[...] = jnp.where(lax.broadcasted_iota(I32, (C_HEADS, page), 1) == 0, 1.0, 0.0)
        acc_ref[...] = jnp.where(lane0, jnp.broadcast_to(vn_ref[0], (C_COLS, page)), 0.0)
        carry_ref[...] = jnp.broadcast_to(lfn_ref[0], (C_HEADS, page))

    for pg in range(C_STEP_PAGES):
        k_ref, v_ref, lf_ref = page_refs[3 * pg:3 * pg + 3]
        lf = lf_ref[0]
        hi, mid, lo3 = _split3(lf)
        decay = _dot(hi, lower_ref[...]) + _dot(mid, lower_ref[...]) + _dot(lo3, lower_ref[...]) + carry_ref[...]
        s = jnp.sum((k_ref[0] * qb_ref[...]).reshape(heads + (page,)), axis=1) * SCALE + decay
        m_old = m_ref[...]
        m_new = jnp.maximum(m_old, _row_max(s))
        p = jnp.exp(s - m_new)
        alpha = jnp.exp(m_old - m_new)
        l_ref[...] = alpha * l_ref[...] + p
        acc_ref[...] = per_head(alpha) * acc_ref[...] + per_head(p) * v_ref[0]
        m_ref[...] = m_new
        carry_ref[...] = carry_ref[...] + _row_sum(lf)

    @pl.when(j == pl.num_programs(1) - 1)
    def _():
        o_ref[0] = _row_sum(acc_ref[...]) / per_head(_row_sum(l_ref[...]))


def _c_step_attn(q, k_new, v_new, lf_new, cache_k, cache_v, cache_logf, page_table):
    n, n_pages = page_table.shape
    page = cache_k.shape[1]
    assert n_pages % C_STEP_PAGES == 0
    lower = jnp.asarray(np.tril(np.ones((page, page), np.float32), -1), BF16)
    col = lambda c: pl.BlockSpec((1, c, 1), lambda b, j, pt: (b, 0, 0))

    def paged(rows, pg):
        return pl.BlockSpec((1, rows, page), lambda b, j, pt: (pt[b, n_pages - 1 - (j * C_STEP_PAGES + pg)], 0, 0))

    k_view = cache_k.transpose(0, 2, 3, 1).reshape(-1, C_COLS, page)
    v_view = cache_v.transpose(0, 2, 3, 1).reshape(-1, C_COLS, page)
    lf_view = cache_logf.transpose(0, 2, 1)
    page_specs, page_args = [], []
    for pg in range(C_STEP_PAGES):
        page_specs += [paged(C_COLS, pg), paged(C_COLS, pg), paged(C_HEADS, pg)]
        page_args += [k_view, v_view, lf_view]
    o = pl.pallas_call(
        _c_step_kernel,
        grid_spec=pltpu.PrefetchScalarGridSpec(
            num_scalar_prefetch=1, grid=(n, n_pages // C_STEP_PAGES),
            in_specs=[col(C_COLS), col(C_COLS), col(C_COLS), col(C_HEADS)] + page_specs
            + [pl.BlockSpec(lower.shape, lambda b, j, pt: (0, 0))],
            out_specs=col(C_COLS),
            scratch_shapes=[pltpu.VMEM((C_COLS, page), F32), pltpu.VMEM((C_HEADS, page), F32),
                            pltpu.VMEM((C_HEADS, page), F32), pltpu.VMEM((C_COLS, page), F32),
                            pltpu.VMEM((C_HEADS, page), F32)]),
        out_shape=jax.ShapeDtypeStruct((n, C_COLS, 1), F32),
        compiler_params=_params("arbitrary", "arbitrary"), name="forget_paged_step")(
            page_table, q.reshape(n, C_COLS, 1), k_new.reshape(n, C_COLS, 1), v_new.reshape(n, C_COLS, 1),
            lf_new[:, :C_HEADS].reshape(n, C_HEADS, 1), *page_args, lower)
    return o.reshape(n, C_COLS)


def _mixer_c(xp, xs, g, w_in, forget_bias, w_out, caches, page_table, batch, seq):
    w = w_in[:, :3 * C_COLS].astype(BF16)
    w_tail = jnp.pad(w_in[:, 3 * C_COLS:], ((0, 0), (0, C_TAIL - C_HEADS))).astype(BF16)
    bias = jnp.pad(forget_bias, (0, C_TAIL - C_HEADS)).reshape(1, C_TAIL)
    w_out = w_out.astype(BF16)
    cache_k, cache_v, cache_logf = caches

    q, k, v, logf, dcum = _c_proj(xp, g, w, w_tail, bias, seq)
    o = _c_attn(q, k, v, dcum, batch, seq)
    yp = _out_proj(xp, o, w_out)
    state = [k.reshape(batch, seq, C_HEADS, HEAD_DIM), v.reshape(batch, seq, C_HEADS, HEAD_DIM),
             logf[:, :C_HEADS].reshape(batch, seq, C_HEADS)]

    n = xs.shape[0]
    sq, sk, sv, slogf, _ = _c_proj(xs, g, w, w_tail, bias, 1)
    so = _c_step_attn(sq, sk, sv, slogf, cache_k, cache_v, cache_logf, page_table)
    ys = _out_proj(xs, so, w_out)
    state += [sk.reshape(n, 1, C_HEADS, HEAD_DIM), sv.reshape(n, 1, C_HEADS, HEAD_DIM),
              slogf[:, :C_HEADS].reshape(n, 1, C_HEADS)]
    return yp, ys, state


B_TAIL = V7X_LANES
B_SLAB = 2 * B_KV_COLS


def _b_proj_kernel(x_ref, g_ref, w_ref, wt_ref, b_ref, q_ref, cmp_ref, sel_ref, win_ref, gate_ref):
    h = _rms(x_ref[...], g_ref[...]).astype(BF16)
    q_ref[...] = _dot(h, w_ref[:, 0:B_Q_COLS]).astype(BF16)
    c0 = B_Q_COLS
    cmp_ref[...] = _dot(h, w_ref[:, c0:c0 + B_SLAB])
    sel_ref[...] = _dot(h, w_ref[:, c0 + B_SLAB:c0 + 2 * B_SLAB])
    win_ref[...] = _dot(h, w_ref[:, c0 + 2 * B_SLAB:c0 + 3 * B_SLAB])
    gate_ref[...] = jax.nn.sigmoid(_dot(h, wt_ref[...]) + b_ref[...])


def _b_proj(x, g, w, w_tail, bias):
    rows = x.shape[0]
    tm = _row_tile(rows)
    spec = lambda cols: pl.BlockSpec((tm, cols), lambda i: (i, 0))
    shape = lambda cols, dt: jax.ShapeDtypeStruct((rows, cols), dt)
    return pl.pallas_call(
        _b_proj_kernel, grid=(rows // tm,),
        in_specs=[spec(D_MODEL), pl.BlockSpec((1, D_MODEL), lambda i: (0, 0)),
                  pl.BlockSpec(w.shape, lambda i: (0, 0)), pl.BlockSpec(w_tail.shape, lambda i: (0, 0)),
                  pl.BlockSpec((1, B_TAIL), lambda i: (0, 0))],
        out_specs=[spec(B_Q_COLS)] + [spec(B_SLAB)] * 3 + [spec(B_TAIL)],
        out_shape=[shape(B_Q_COLS, BF16)] + [shape(B_SLAB, F32)] * 3 + [shape(B_TAIL, F32)],
        compiler_params=_params("arbitrary"), name="sparse_in_proj_step")(x, g.reshape(1, D_MODEL), w, w_tail, bias)


B_GROUP_Q = 2 * V7X_LANES
B_GQ_COLS = B_KV_HEADS * B_GROUP_Q
B_GKV_COLS = B_KV_HEADS * V7X_LANES
B_GGATE_COLS = B_KV_HEADS * V7X_LANES


def _b_proj_grouped_kernel(x_ref, g_ref, w_ref, wt_ref, b_ref, q_ref, cmp_ref, sel_ref, win_ref, selg_ref, wing_ref,
                           gate_ref):
    h = _rms(x_ref[...], g_ref[...]).astype(BF16)
    c = 0
    q_ref[...] = _dot(h, w_ref[:, c:c + B_GQ_COLS]).astype(BF16)
    c += B_GQ_COLS
    for ref in (cmp_ref, sel_ref, win_ref):
        ref[...] = _dot(h, w_ref[:, c:c + B_SLAB])
        c += B_SLAB
    for ref in (selg_ref, wing_ref):
        ref[...] = _dot(h, w_ref[:, c:c + B_GKV_COLS]).astype(BF16)
        c += B_GKV_COLS
    gate_ref[...] = jax.nn.sigmoid(_dot(h, wt_ref[...]) + b_ref[...])


def _b_grouped_weights(w_in, gate_bias):
    q = w_in[:, :B_Q_COLS].reshape(D_MODEL, B_KV_HEADS, B_REP * HEAD_DIM)
    q = jnp.pad(q, ((0, 0), (0, 0), (0, B_GROUP_Q - B_REP * HEAD_DIM))).reshape(D_MODEL, B_GQ_COLS)
    main = w_in[:, B_Q_COLS:B_Q_COLS + 3 * B_SLAB]

    def grouped(slab):
        kv = slab.reshape(D_MODEL, 2, B_KV_HEADS, HEAD_DIM)
        return kv.transpose(0, 2, 1, 3).reshape(D_MODEL, B_GKV_COLS)

    w = jnp.concatenate([q, main, grouped(main[:, B_SLAB:2 * B_SLAB]), grouped(main[:, 2 * B_SLAB:])], axis=1)

    def gates(a):
        a = a.reshape(a.shape[:-1] + (3, B_KV_HEADS, B_REP))
        a = jnp.moveaxis(a, -2, -3).reshape(a.shape[:-3] + (B_KV_HEADS, 3 * B_REP))
        pad = [(0, 0)] * (a.ndim - 1) + [(0, V7X_LANES - 3 * B_REP)]
        return jnp.pad(a, pad).reshape(a.shape[:-2] + (B_GGATE_COLS,))

    tail = w_in[:, B_Q_COLS + 3 * B_SLAB:]
    return w.astype(BF16), gates(tail).astype(BF16), gates(gate_bias).reshape(1, B_GGATE_COLS)


def _b_proj_grouped(x, g, w, w_tail, bias):
    rows = x.shape[0]
    tm = _row_tile(rows)
    spec = lambda cols: pl.BlockSpec((tm, cols), lambda i: (i, 0))
    shape = lambda cols, dt: jax.ShapeDtypeStruct((rows, cols), dt)
    return pl.pallas_call(
        _b_proj_grouped_kernel, grid=(rows // tm,),
        in_specs=[spec(D_MODEL), pl.BlockSpec((1, D_MODEL), lambda i: (0, 0)),
                  pl.BlockSpec(w.shape, lambda i: (0, 0)), pl.BlockSpec(w_tail.shape, lambda i: (0, 0)),
                  pl.BlockSpec((1, B_GGATE_COLS), lambda i: (0, 0))],
        out_specs=[spec(B_GQ_COLS)] + [spec(B_SLAB)] * 3 + [spec(B_GKV_COLS)] * 2 + [spec(B_GGATE_COLS)],
        out_shape=[shape(B_GQ_COLS, BF16)] + [shape(B_SLAB, F32)] * 3 + [shape(B_GKV_COLS, BF16)] * 2
        + [shape(B_GGATE_COLS, F32)],
        compiler_params=_params("arbitrary"), name="sparse_in_proj_prompt")(x, g.reshape(1, D_MODEL), w, w_tail, bias)


B_PHI_ROWS = 2 * B_KV_HEADS * B_PHI_HIDDEN
B_CHUNK_COLS = B_CMP_STRIDE * B_SLAB


def _compress_weights(phi_pos, phi_w1, phi_w2, pad_heads):
    half = B_CMP_BLOCK // B_CMP_STRIDE
    eye = jnp.eye(B_KV_HEADS, dtype=F32)
    w1r = phi_w1.reshape(2, half, B_CMP_STRIDE, HEAD_DIM, B_PHI_HIDDEN)
    first = jnp.einsum('cpsde,gh->cspgehd', w1r, eye).reshape(2, B_CMP_STRIDE, B_PHI_ROWS, B_KV_COLS)
    pos_term = jnp.einsum('cpsd,cpsde->cpe', phi_pos.reshape(2, half, B_CMP_STRIDE, HEAD_DIM), w1r,
                          precision=lax.Precision.HIGHEST)
    pos = jnp.broadcast_to(pos_term[:, :, None, :, None], (2, half, B_KV_HEADS, B_PHI_HIDDEN, V7X_LANES))
    second = jnp.einsum('ced,gh->cgdhe', phi_w2, eye)
    if pad_heads:
        second = jnp.pad(second, ((0, 0), (0, 0), (0, V7X_LANES - HEAD_DIM), (0, 0), (0, 0)))
    second = second.reshape(2, -1, B_KV_HEADS * B_PHI_HIDDEN)
    return first.astype(BF16), pos.reshape(2, B_PHI_ROWS, V7X_LANES), second.astype(BF16)


def _compress_first(load, first_ref, c):
    acc = None
    for s in range(B_CMP_STRIDE):
        part = _dot_nt(first_ref[c, s], load(s, c).astype(BF16))
        acc = part if acc is None else acc + part
    return acc


def _compress_second(acc, pos_ref, second_ref, c):
    n_chunks = acc.shape[1]
    acc = acc + pos_ref[c][:, 0:1]
    rows = B_PHI_ROWS // 2
    pre = acc[:rows] + pltpu.roll(acc[rows:], n_chunks - 1, 1)
    return _dot(second_ref[c], jax.nn.gelu(pre).astype(BF16))


def _b_compress_prompt_kernel(x_ref, first_ref, pos_ref, second_ref, o_ref):
    def load(s, c):
        c0 = s * B_SLAB + c * B_KV_COLS
        return x_ref[0, :, c0:c0 + B_KV_COLS]

    for c in range(2):
        o_ref[0, c] = _compress_second(_compress_first(load, first_ref, c), pos_ref, second_ref, c).astype(BF16)


def _b_compress_prompt(cmp, weights, batch, seq):
    n_chunks = seq // B_CMP_STRIDE
    first, pos, second = weights
    out_rows = second.shape[1]
    whole = lambda a: pl.BlockSpec(a.shape, lambda b: (0,) * a.ndim)
    return pl.pallas_call(
        _b_compress_prompt_kernel, grid=(batch,),
        in_specs=[pl.BlockSpec((1, n_chunks, B_CHUNK_COLS), lambda b: (b, 0, 0)),
                  whole(first), whole(pos), whole(second)],
        out_specs=pl.BlockSpec((1, 2, out_rows, n_chunks), lambda b: (b, 0, 0, 0)),
        out_shape=jax.ShapeDtypeStruct((batch, 2, out_rows, n_chunks), BF16),
        compiler_params=_params("arbitrary"), name="sparse_compress_prompt")(
            cmp.reshape(batch, n_chunks, B_CHUNK_COLS), first, pos, second).reshape(
                batch, 2, B_KV_HEADS, V7X_LANES, n_chunks)


def _b_compress_step_kernel(pt_ref, cache_ref, first_ref, pos_ref, second_ref, o_ref, buf_ref, acc_ref, sem_ref,
                            *, pages_per_half, chunks_per_page):
    n = pl.program_id(0)
    hf = pl.program_id(1)
    step = n * 2 + hf
    total = pl.num_programs(0) * 2
    slot = step % 2
    half_chunks = pages_per_half * chunks_per_page

    def copies(st, sl):
        nn = st // 2
        hh = st % 2
        return [pltpu.make_async_copy(cache_ref.at[pt_ref[nn, hh * pages_per_half + j]],
                                      buf_ref.at[sl, pl.ds(j * chunks_per_page, chunks_per_page), :],
                                      sem_ref.at[sl]) for j in range(pages_per_half)]

    @pl.when(step == 0)
    def _():
        for cp in copies(0, 0):
            cp.start()

    @pl.when(step + 1 < total)
    def _():
        for cp in copies(step + 1, 1 - slot):
            cp.start()

    for cp in copies(step, slot):
        cp.wait()

    def load(s, c):
        c0 = s * B_SLAB + c * B_KV_COLS
        return buf_ref[slot, :, c0:c0 + B_KV_COLS]

    for c in range(2):
        part = _compress_first(load, first_ref, c)

        @pl.when(hf == 0)
        def _():
            acc_ref[c, :, 0:half_chunks] = part

        @pl.when(hf == 1)
        def _():
            acc_ref[c, :, half_chunks:2 * half_chunks] = part

    @pl.when(hf == 1)
    def _():
        for c in range(2):
            o_ref[0, c] = _compress_second(acc_ref[c], pos_ref, second_ref, c).astype(BF16)


def _b_compress_step(cache_cmp, page_table, weights):
    n, n_pages = page_table.shape
    page = cache_cmp.shape[1]
    chunks_per_page = page // B_CMP_STRIDE
    n_chunks = n_pages * chunks_per_page
    pages_per_half = n_pages // 2
    first, pos, second = weights
    whole = lambda a: pl.BlockSpec(a.shape, lambda b, h, pt: (0,) * a.ndim)
    return pl.pallas_call(
        functools.partial(_b_compress_step_kernel, pages_per_half=pages_per_half, chunks_per_page=chunks_per_page),
        grid_spec=pltpu.PrefetchScalarGridSpec(
            num_scalar_prefetch=1, grid=(n, 2),
            in_specs=[pl.BlockSpec(memory_space=pl.ANY), whole(first), whole(pos), whole(second)],
            out_specs=pl.BlockSpec((1, 2, B_KV_COLS, n_chunks), lambda b, h, pt: (b, 0, 0, 0)),
            scratch_shapes=[pltpu.VMEM((2, n_chunks // 2, B_CHUNK_COLS), F32),
                            pltpu.VMEM((2, B_PHI_ROWS, n_chunks), F32),
                            pltpu.SemaphoreType.DMA((2,))]),
        out_shape=jax.ShapeDtypeStruct((n, 2, B_KV_COLS, n_chunks), BF16),
        compiler_params=_params("arbitrary", "arbitrary"), name="sparse_compress_step")(
            page_table, cache_cmp.reshape(-1, chunks_per_page, B_CHUNK_COLS), first, pos, second)


B_TILE = 128
B_FAR_TILES = REL_MAX_DIST // B_TILE + 2


def _toeplitz_tiles(table):
    shape = (B_FAR_TILES, B_TILE, B_TILE)
    dist = (B_TILE * lax.broadcasted_iota(I32, shape, 0) + lax.broadcasted_iota(I32, shape, 1)
            - lax.broadcasted_iota(I32, shape, 2))
    return _bias_of_dist(table, dist)


def _cmp_bias_tiles(table, seq, n_chunks):
    shape = (seq // B_TILE, B_TILE, n_chunks)
    dist = (B_TILE * lax.broadcasted_iota(I32, shape, 0) + lax.broadcasted_iota(I32, shape, 1)
            - B_CMP_STRIDE * lax.broadcasted_iota(I32, shape, 2) - (B_CMP_BLOCK - 1))
    return _bias_of_dist(table, dist, head_axis=1)


def _overlap_t(n_sel_padded, n_cmp_padded, n_sel, n_cmp):
    c_start = np.arange(n_cmp_padded)[None, :] * B_CMP_STRIDE
    s_start = np.arange(n_sel_padded)[:, None] * B_SEL_BLOCK
    ov = (c_start < s_start + B_SEL_BLOCK) & (c_start + B_CMP_BLOCK > s_start)
    ov &= (np.arange(n_cmp_padded)[None, :] < n_cmp) & (np.arange(n_sel_padded)[:, None] < n_sel)
    return ov.astype(np.float32)


B_SEL_CHUNK = 512
B_SEL_PAD = 128
B_PAD_SCORE = -3e30


def _b_head_layout(g, r):
    h = g * B_REP + r
    kv0 = (g // 2) * V7X_LANES
    return h, h // 2, h % 2, slice(kv0, kv0 + V7X_LANES), g % 2


def _b_attn_kernel(q_ref, gate_ref, kvt_ref, tcmp_ref, sel_ref, win_ref, ttab_ref, ovt_ref, o_ref,
                   l_ref, acc_ref, imp_ref, *, n_sel):
    g = pl.program_id(1)
    i = pl.program_id(2)
    t = B_TILE
    ch = B_SEL_CHUNK
    tiles_per_chunk = ch // t
    q = q_ref[0]
    gate = gate_ref[0]
    n_cmp_pad = kvt_ref.shape[4]
    qpos_c = i * t + lax.broadcasted_iota(I32, (t, n_cmp_pad), 0)
    cidx = lax.broadcasted_iota(I32, (t, n_cmp_pad), 1)
    mask_c = qpos_c >= cidx * B_CMP_STRIDE + (B_CMP_BLOCK - 1)
    blk = lax.broadcasted_iota(I32, (n_sel, t), 0)
    cur = (i * t + lax.broadcasted_iota(I32, (n_sel, t), 1)) // B_SEL_BLOCK
    forced = (blk == 0) | (blk == cur) | (blk == cur - 1)
    qpos_s = i * t + lax.broadcasted_iota(I32, (t, ch), 0)
    klane = lax.broadcasted_iota(I32, (t, ch), 1)
    eb = lax.broadcasted_iota(I32, (B_SEL_PAD, ch), 0)
    ek = lax.broadcasted_iota(I32, (B_SEL_PAD, ch), 1) // B_SEL_BLOCK
    n_win_tiles = B_WINDOW // t + 1
    wq = lax.broadcasted_iota(I32, (t, n_win_tiles * t), 0)
    wk = lax.broadcasted_iota(I32, (t, n_win_tiles * t), 1)
    dist_w = B_WINDOW + wq - wk
    mask_w = (dist_w >= 0) & (dist_w < B_WINDOW) & (wk // t >= n_win_tiles - 1 - i)
    n_chunks = (i + tiles_per_chunk) // tiles_per_chunk
    qms = [_align_head(_pick_head(q[:, (r // 2) * V7X_LANES:(r // 2 + 1) * V7X_LANES], r % 2), r % 2, 0)
           for r in range(B_REP)]

    kct = kvt_ref[0, 0, 0]
    vct = kvt_ref[0, 1, 0]
    p_sum = jnp.zeros((t, n_cmp_pad), F32)
    o_cmp = []
    for r in range(B_REP):
        s = _dot(qms[r], kct) * SCALE + tcmp_ref[0, r]
        p, _, den = _masked_softmax(s, mask_c)
        p = p / den
        p_sum = p_sum + p
        o_cmp.append(_dot_nt(p.astype(BF16), vct))

    hi, mid, lo3 = _split3(p_sum)
    imp = _dot_nt(ovt_ref[...], hi) + _dot_nt(ovt_ref[...], mid) + _dot_nt(ovt_ref[...], lo3)
    imp = jnp.where(forced, B_FORCE, jnp.where(blk <= cur, imp, NEG_INF))
    imp_ref[...] = imp

    def rank_body(j, rank):
        row = imp_ref[pl.ds(j, 1), :]
        ahead = (row > imp) | ((row == imp) & (j < blk))
        return rank + jnp.where(ahead, 1.0, 0.0)

    n_visible = jnp.minimum(n_sel, (i * t + t - 1) // B_SEL_BLOCK + 1)
    rank = lax.fori_loop(0, n_visible, rank_body, jnp.zeros((n_sel, t), F32))
    chosen = jnp.where(rank < B_TOPN, 1.0, 0.0)
    if n_sel < B_SEL_PAD:
        chosen = jnp.concatenate([chosen, jnp.zeros((B_SEL_PAD - n_sel, t), F32)], axis=0)
    chosen = jnp.transpose(chosen).astype(BF16)

    def scores(c, r):
        start = pl.multiple_of(c * ch, ch)
        kv = sel_ref[0, pl.ds(start, ch), :]
        deltas = [jnp.clip(i - c * tiles_per_chunk - j, 0, B_FAR_TILES - 1) for j in range(tiles_per_chunk)]
        bias = jnp.concatenate([ttab_ref[g * B_REP + r, d] for d in deltas], axis=1)
        return _dot_nt(qms[r], kv) * SCALE + bias, kv

    def chunk_mask(c):
        expand = jnp.where(eb == ek + c * (ch // B_SEL_BLOCK), 1.0, 0.0).astype(BF16)
        return (_dot(chosen, expand) > 0.5) & (klane + c * ch <= qpos_s)

    def lane_fold(x, op):
        parts = [x[:, k * V7X_LANES:(k + 1) * V7X_LANES] for k in range(ch // V7X_LANES)]
        return functools.reduce(op, parts)

    def max_body(c, m_lanes):
        mask = chunk_mask(c)
        out = []
        for r in range(B_REP):
            s, _ = scores(c, r)
            out.append(jnp.maximum(m_lanes[r], lane_fold(jnp.where(mask, s, NEG_INF), jnp.maximum)))
        return tuple(out)

    m_lanes = lax.fori_loop(0, n_chunks, max_body, tuple(jnp.full((t, V7X_LANES), NEG_INF, F32) for _ in range(B_REP)))
    m_rows = [_row_max(m) for m in m_lanes]

    l_ref[...] = jnp.zeros_like(l_ref)
    acc_ref[...] = jnp.zeros_like(acc_ref)

    def sum_body(c, carry):
        mask = chunk_mask(c)
        for r in range(B_REP):
            s, kv = scores(c, r)
            p = jnp.where(mask, jnp.exp(s - m_rows[r]), 0.0)
            l_ref[r] = l_ref[r] + lane_fold(p, lambda a, b: a + b)
            acc_ref[r] = acc_ref[r] + _dot(p.astype(BF16), kv)
        return carry

    lax.fori_loop(0, n_chunks, sum_body, 0)

    starts = [pl.multiple_of(jnp.maximum(i - (n_win_tiles - 1) + j, 0) * t, t) for j in range(n_win_tiles)]
    kvw = jnp.concatenate([win_ref[0, pl.ds(st, t), :] for st in starts], axis=0)
    slabs = [jnp.zeros((t, V7X_LANES), F32) for _ in range(B_GROUP_Q // V7X_LANES)]
    for r in range(B_REP):
        bias = jnp.concatenate([ttab_ref[g * B_REP + r, n_win_tiles - 1 - j] for j in range(n_win_tiles)], axis=1)
        s = _dot_nt(qms[r], kvw) * SCALE + bias
        p, _, den = _masked_softmax(s, mask_w)
        o_win = _dot(p.astype(BF16), kvw) / den
        o_sel = acc_ref[r] / _row_sum(l_ref[r])
        o_v = gate[:, B_REP + r:B_REP + r + 1] * o_sel + gate[:, 2 * B_REP + r:2 * B_REP + r + 1] * o_win
        o = gate[:, r:r + 1] * _pick_head(o_cmp[r], 0) + _align_head(_pick_head(o_v, 1), 1, 0)
        slabs[r // 2] = slabs[r // 2] + _align_head(o, 0, r % 2)
    for sidx, slab in enumerate(slabs):
        o_ref[0, :, sidx * V7X_LANES:(sidx + 1) * V7X_LANES] = slab


def _b_attn(q, gate, kvt, selg, wing, table, batch, seq):
    t = B_TILE
    n_chunks = seq // B_CMP_STRIDE
    n_sel = seq // B_SEL_BLOCK
    assert n_sel <= B_SEL_PAD and n_sel % 8 == 0 and seq % B_SEL_CHUNK == 0
    tcmp = _cmp_bias_tiles(table, seq, n_chunks)
    ttab = _toeplitz_tiles(table)
    ovt = jnp.asarray(_overlap_t(n_sel, n_chunks, n_sel, n_chunks - 1), BF16)
    whole = lambda a: pl.BlockSpec(a.shape, lambda b, g, i: (0,) * a.ndim)
    seq_block = pl.BlockSpec((1, seq, V7X_LANES), lambda b, g, i: (b, 0, g))
    o = pl.pallas_call(
        functools.partial(_b_attn_kernel, n_sel=n_sel), grid=(batch, B_KV_HEADS, seq // t),
        in_specs=[pl.BlockSpec((1, t, B_GROUP_Q), lambda b, g, i: (b, i, g)),
                  pl.BlockSpec((1, t, V7X_LANES), lambda b, g, i: (b, i, g)),
                  pl.BlockSpec((1, 2, 1, V7X_LANES, n_chunks), lambda b, g, i: (b, 0, g, 0, 0)),
                  pl.BlockSpec((1, B_REP, t, n_chunks), lambda b, g, i: (i, g, 0, 0)),
                  seq_block, seq_block, whole(ttab), whole(ovt)],
        out_specs=pl.BlockSpec((1, t, B_GROUP_Q), lambda b, g, i: (b, i, g)),
        out_shape=jax.ShapeDtypeStruct((batch, seq, B_GQ_COLS), F32),
        scratch_shapes=[pltpu.VMEM((B_REP, t, V7X_LANES), F32)] * 2 + [pltpu.VMEM((n_sel, t), F32)],
        compiler_params=_params("arbitrary", "arbitrary", "arbitrary"), name="sparse_prompt_attention")(
            q.reshape(batch, seq, B_GQ_COLS), gate.reshape(batch, seq, B_GGATE_COLS), kvt, tcmp,
            selg.reshape(batch, seq, B_GKV_COLS), wing.reshape(batch, seq, B_GKV_COLS), ttab, ovt)
    return o.reshape(batch * seq, B_GQ_COLS)


B_STEP_SEL_PAD = 256
B_STEP_ROWS = 8


def _b_step_cmp_kernel(q_ref, kvt_ref, bias_ref, ov_ref, oc_ref, ids_ref, *, n_cmp, n_sel, cur):
    pad = B_STEP_SEL_PAD
    n_cmp_pad = kvt_ref.shape[3]
    q = q_ref[0].astype(F32)
    mask_c = lax.broadcasted_iota(I32, (1, n_cmp_pad), 1) < n_cmp
    blk = lax.broadcasted_iota(I32, (1, pad), 1)
    forced = (blk == 0) | (blk == cur) | (blk == cur - 1)
    b_idx = lax.broadcasted_iota(I32, (pad, pad), 0)
    j_idx = lax.broadcasted_iota(I32, (pad, pad), 1)
    slot = lax.broadcasted_iota(I32, (pad, V7X_LANES), 1).astype(F32)
    b_val = lax.broadcasted_iota(I32, (pad, V7X_LANES), 0).astype(F32)
    for g in range(B_KV_HEADS):
        kv_rows = slice(g * HEAD_DIM, (g + 1) * HEAD_DIM)
        kct = kvt_ref[0, 0, kv_rows, :].astype(F32)
        vct = kvt_ref[0, 1, kv_rows, :].astype(F32)
        p_sum = jnp.zeros((1, n_cmp_pad), F32)
        for r in range(B_REP):
            h = g * B_REP + r
            rows = slice(h * HEAD_DIM, (h + 1) * HEAD_DIM)
            s = _col_dot(kct, q[rows]) * SCALE + bias_ref[h:h + 1, :]
            p, _, den = _masked_softmax(s, mask_c)
            p = p / den
            p_sum = p_sum + p
            oc_ref[0, rows, :] = _row_sum(vct * p)
        hi, mid, lo3 = _split3(jnp.broadcast_to(p_sum, (B_STEP_ROWS, n_cmp_pad)))
        imp = (_dot(hi, ov_ref[...]) + _dot(mid, ov_ref[...]) + _dot(lo3, ov_ref[...]))[0:1]
        imp = jnp.where(forced, B_FORCE, jnp.where(blk <= cur, imp, NEG_INF))
        imp = jnp.where(blk < n_sel, imp, B_PAD_SCORE)
        other = jnp.broadcast_to(imp, (pad, pad))
        mine = jnp.transpose(other)
        ahead = (other > mine) | ((other == mine) & (j_idx < b_idx))
        rank = jnp.sum(jnp.where(ahead, 1.0, 0.0), axis=1, keepdims=True)
        ids = jnp.sum(jnp.where(rank == slot, b_val, 0.0), axis=0, keepdims=True)
        ids_ref[0, g:g + 1, :] = ids.astype(I32)


def _b_step_sel_kernel(ids_ref, pt_ref, q_ref, gate_ref, oc_ref, seln_ref, winn_ref, wbuf_ref,
                       s0_ref, s1_ref, s2_ref, s3_ref, bblk_ref, bwin_ref, o_ref, m_ref, l_ref, acc_ref,
                       *, n_past_blocks, blocks_per_page):
    n = pl.program_id(0)
    kb = pl.program_id(1)
    page = s0_ref.shape[4]
    q = q_ref[0].astype(F32)
    seln = seln_ref[0]
    sel_blocks = (s0_ref, s1_ref, s2_ref, s3_ref)
    self_bias = bblk_ref[n_past_blocks]
    lane = lax.broadcasted_iota(I32, (1, page), 1)

    def layout(h):
        g = h // B_REP
        return (slice(h * HEAD_DIM, (h + 1) * HEAD_DIM), slice(g * HEAD_DIM, (g + 1) * HEAD_DIM),
                slice(B_KV_COLS + g * HEAD_DIM, B_KV_COLS + (g + 1) * HEAD_DIM))

    def self_score(h, new):
        rows, k_rows, _ = layout(h)
        return _col_dot(new[k_rows], q[rows]) * SCALE + self_bias[h:h + 1, 0:1]

    @pl.when(kb == 0)
    def _():
        for h in range(B_HEADS):
            _, _, v_rows = layout(h)
            m_ref[h] = jnp.broadcast_to(self_score(h, seln), (1, page))
            l_ref[h] = jnp.where(lane == 0, 1.0, 0.0)
            acc_ref[h] = jnp.where(lane == 0, jnp.broadcast_to(seln[v_rows], (HEAD_DIM, page)), 0.0)

    for g in range(B_KV_HEADS):
        blkid = ids_ref[(n * B_KV_HEADS + g) * V7X_LANES + kb]
        mask = (blkid < n_past_blocks) & (lane // B_SEL_BLOCK == blkid % blocks_per_page)
        bias_blk = bblk_ref[jnp.minimum(blkid, n_past_blocks)]
        kt = sel_blocks[g][0, 0, 0]
        vt = sel_blocks[g][0, 1, 0]
        for r in range(B_REP):
            h = g * B_REP + r
            rows, _, _ = layout(h)
            s = jnp.where(mask, _col_dot(kt, q[rows]) * SCALE + bias_blk[h:h + 1, :], NEG_INF)
            m_old = m_ref[h]
            m_new = jnp.maximum(m_old, _row_max(s))
            p = jnp.where(mask, jnp.exp(s - m_new), 0.0)
            alpha = jnp.exp(m_old - m_new)
            l_ref[h] = alpha * l_ref[h] + p
            acc_ref[h] = alpha * acc_ref[h] + vt * p
            m_ref[h] = m_new

    @pl.when(kb == pl.num_programs(1) - 1)
    def _():
        gate = gate_ref[0]
        winn = winn_ref[0]
        wb = wbuf_ref.shape[4]
        wlane = lax.broadcasted_iota(I32, (1, wb), 1)
        wmask = (wb - wlane >= 0) & (wb - wlane < B_WINDOW)
        for h in range(B_HEADS):
            g = h // B_REP
            rows, _, v_rows = layout(h)
            s = jnp.where(wmask, _col_dot(wbuf_ref[0, 0, g], q[rows]) * SCALE + bwin_ref[h:h + 1, :], NEG_INF)
            s_self = self_score(h, winn)
            m = jnp.maximum(_row_max(s), s_self)
            pb = jnp.where(wmask, jnp.exp(s - m), 0.0)
            ps = jnp.exp(s_self - m)
            o_win = (_row_sum(wbuf_ref[0, 1, g] * pb) + ps * winn[v_rows]) / (_row_sum(pb) + ps)
            o_sel = _row_sum(acc_ref[h]) / _row_sum(l_ref[h])
            o_ref[0, rows, :] = (gate[:, h:h + 1] * oc_ref[0, rows, :] + gate[:, B_HEADS + h:B_HEADS + h + 1] * o_sel
                                 + gate[:, 2 * B_HEADS + h:2 * B_HEADS + h + 1] * o_win)


def _b_step_attn(q, gate, kvt, sel_new, win_new, cache_sel, cache_win, page_table, table):
    n, n_pages = page_table.shape
    page = cache_sel.shape[1]
    past = n_pages * page
    wb = cache_win.shape[1]
    assert wb == B_WINDOW and past % B_SEL_BLOCK == 0 and page % B_SEL_BLOCK == 0
    n_cmp_pad = past // B_CMP_STRIDE
    n_cmp = (past + 1) // B_CMP_STRIDE - (B_CMP_BLOCK // B_CMP_STRIDE) + 1
    n_sel = -(-(past + 1) // B_SEL_BLOCK)
    n_past_blocks = past // B_SEL_BLOCK
    blocks_per_page = page // B_SEL_BLOCK
    assert n_sel <= B_STEP_SEL_PAD and n_cmp <= n_cmp_pad

    bias_c = _bias_of_dist(table, np.maximum(past - (B_CMP_STRIDE * np.arange(n_cmp_pad) + B_CMP_BLOCK - 1), 0))
    ov = jnp.asarray(_overlap_t(B_STEP_SEL_PAD, n_cmp_pad, n_sel, n_cmp).T, BF16)
    col = lambda c: pl.BlockSpec((1, c, 1), lambda b: (b, 0, 0))
    q_col = q.reshape(n, B_Q_COLS, 1)
    oc, ids = pl.pallas_call(
        functools.partial(_b_step_cmp_kernel, n_cmp=n_cmp, n_sel=n_sel, cur=past // B_SEL_BLOCK), grid=(n,),
        in_specs=[col(B_Q_COLS), pl.BlockSpec((1, 2, B_KV_COLS, n_cmp_pad), lambda b: (b, 0, 0, 0)),
                  pl.BlockSpec(bias_c.shape, lambda b: (0, 0)), pl.BlockSpec(ov.shape, lambda b: (0, 0))],
        out_specs=[col(B_Q_COLS), pl.BlockSpec((1, B_KV_HEADS, V7X_LANES), lambda b: (b, 0, 0))],
        out_shape=[jax.ShapeDtypeStruct((n, B_Q_COLS, 1), F32),
                   jax.ShapeDtypeStruct((n, B_KV_HEADS, V7X_LANES), I32)],
        compiler_params=_params("arbitrary"), name="sparse_step_compressed")(q_col, kvt, bias_c, ov)

    blk_pos = B_SEL_BLOCK * np.arange(n_past_blocks + 1)[:, None] + np.arange(B_SEL_BLOCK)[None, :]
    bblk = _bias_of_dist(table, np.tile(np.maximum(past - blk_pos, 0), (1, blocks_per_page)), head_axis=1)
    bwin = _bias_of_dist(table, wb - np.arange(wb))
    scol = lambda c: pl.BlockSpec((1, c, 1), lambda b, kb, ids, pt: (b, 0, 0))

    def sel_spec(g):
        def index(b, kb, ids, pt):
            blk = jnp.minimum(ids[(b * B_KV_HEADS + g) * V7X_LANES + kb], n_past_blocks - 1)
            return (pt[b, blk // blocks_per_page], 0, g, 0, 0)
        return pl.BlockSpec((1, 2, 1, HEAD_DIM, page), index)

    sel_view = cache_sel.transpose(0, 2, 3, 4, 1)
    win_view = cache_win.transpose(0, 2, 3, 4, 1)
    o = pl.pallas_call(
        functools.partial(_b_step_sel_kernel, n_past_blocks=n_past_blocks, blocks_per_page=blocks_per_page),
        grid_spec=pltpu.PrefetchScalarGridSpec(
            num_scalar_prefetch=2, grid=(n, B_TOPN),
            in_specs=[scol(B_Q_COLS), pl.BlockSpec((1, 1, B_TAIL), lambda b, kb, ids, pt: (b, 0, 0)),
                      scol(B_Q_COLS), scol(B_SLAB), scol(B_SLAB),
                      pl.BlockSpec((1,) + win_view.shape[1:], lambda b, kb, ids, pt: (b, 0, 0, 0, 0))]
            + [sel_spec(g) for g in range(B_KV_HEADS)]
            + [pl.BlockSpec(bblk.shape, lambda b, kb, ids, pt: (0, 0, 0)),
               pl.BlockSpec(bwin.shape, lambda b, kb, ids, pt: (0, 0))],
            out_specs=scol(B_Q_COLS),
            scratch_shapes=[pltpu.VMEM((B_HEADS, 1, page), F32), pltpu.VMEM((B_HEADS, 1, page), F32),
                            pltpu.VMEM((B_HEADS, HEAD_DIM, page), F32)]),
        out_shape=jax.ShapeDtypeStruct((n, B_Q_COLS, 1), F32),
        compiler_params=_params("arbitrary", "arbitrary"), name="sparse_step_selected")(
            ids.reshape(-1), page_table, q_col, gate.reshape(n, 1, B_TAIL), oc,
            sel_new.reshape(n, B_SLAB, 1), win_new.reshape(n, B_SLAB, 1), win_view,
            sel_view, sel_view, sel_view, sel_view, bblk, bwin)
    return o.reshape(n, B_Q_COLS)


def _mixer_b(xp, xs, g, w_in, gate_bias, phi_pos, phi_w1, phi_w2, w_out, table, caches, page_table, batch, seq):
    cache_cmp, cache_sel, cache_win = caches
    five = lambda a, rows: a.reshape(-1, rows, 2, B_KV_HEADS, HEAD_DIM)

    w, w_tail, bias = _b_grouped_weights(w_in, gate_bias)
    q, cmp, sel, win, selg, wing, gate = _b_proj_grouped(xp, g, w, w_tail, bias)
    kvt = _b_compress_prompt(cmp, _compress_weights(phi_pos, phi_w1, phi_w2, True), batch, seq)
    o = _b_attn(q, gate, kvt, selg, wing, table, batch, seq)
    w_out_grouped = jnp.pad(w_out.reshape(B_KV_HEADS, B_REP * HEAD_DIM, D_MODEL),
                            ((0, 0), (0, B_GROUP_Q - B_REP * HEAD_DIM), (0, 0))).reshape(B_GQ_COLS, D_MODEL)
    yp = _out_proj(xp, o, w_out_grouped.astype(BF16))
    keep = min(B_WINDOW, seq)
    state = [five(cmp, seq), five(sel, seq), five(win, seq)[:, seq - keep:]]

    n_main = B_Q_COLS + 3 * B_SLAB
    w = w_in[:, :n_main].astype(BF16)
    w_tail = jnp.pad(w_in[:, n_main:], ((0, 0), (0, B_TAIL - B_GATE_COLS))).astype(BF16)
    bias = jnp.pad(gate_bias, (0, B_TAIL - B_GATE_COLS)).reshape(1, B_TAIL)
    sq, scmp, ssel, swin, sgate = _b_proj(xs, g, w, w_tail, bias)
    skvt = _b_compress_step(cache_cmp, page_table, _compress_weights(phi_pos, phi_w1, phi_w2, False))
    so = _b_step_attn(sq, sgate, skvt, ssel, swin, cache_sel, cache_win, page_table, table)
    ys = _out_proj(xs, so, w_out.astype(BF16))
    wb = cache_win.shape[1]
    win_all = jnp.concatenate([cache_win, five(swin, 1)], axis=1)
    state += [five(scmp, 1), five(ssel, 1), win_all[:, wb + 1 - min(B_WINDOW, wb + 1):]]
    return yp, ys, state


def kernel(x_prompt, x_sample, cache_l0_w128, cache_l0_w512, cache_l0_w2048, cache_l1_cmp, cache_l1_sel, cache_l1_win, cache_l2_k, cache_l2_v, cache_l2_logf, cache_l3_w128, cache_l3_w512, cache_l3_w2048, page_table, norm_g, ffn_w_gate, ffn_w_up, ffn_w_down, final_norm_g, rel_bias_table, a_w_in, a_w_out, b_w_in, b_gate_bias, b_phi_pos, b_phi_w1, b_phi_w2, b_w_out, c_w_in, c_forget_bias, c_w_out):
    batch, seq, _ = x_prompt.shape
    depth = norm_g.shape[0]
    layer_caches = ((cache_l0_w128, cache_l0_w512, cache_l0_w2048), (cache_l1_cmp, cache_l1_sel, cache_l1_win),
                    (cache_l2_k, cache_l2_v, cache_l2_logf), (cache_l3_w128, cache_l3_w512, cache_l3_w2048))
    xp = x_prompt.reshape(batch * seq, D_MODEL)
    xs = x_sample.reshape(-1, D_MODEL)
    table = rel_bias_table
    new_state = []
    for i in range(depth):
        kind, j = i % N_MIXERS, i // N_MIXERS
        last = i == depth - 1
        f1 = (norm_g[i, 0], ffn_w_gate[i, 0].astype(BF16), ffn_w_up[i, 0].astype(BF16), ffn_w_down[i, 0].astype(BF16))
        f2 = (norm_g[i, 2], ffn_w_gate[i, 1].astype(BF16), ffn_w_up[i, 1].astype(BF16), ffn_w_down[i, 1].astype(BF16))
        xp, xs = _ffn(xp, *f1), _ffn(xs, *f1)
        if kind == 0:
            xp, xs, state = _mixer_a(xp, xs, norm_g[i, 1], a_w_in[j], a_w_out[j], table, layer_caches[i], batch, seq)
        elif kind == 1:
            xp, xs, state = _mixer_b(xp, xs, norm_g[i, 1], b_w_in[j], b_gate_bias[j], b_phi_pos[j], b_phi_w1[j],
                                     b_phi_w2[j], b_w_out[j], table, layer_caches[i], page_table, batch, seq)
        else:
            xp, xs, state = _mixer_c(xp, xs, norm_g[i, 1], c_w_in[j], c_forget_bias[j], c_w_out[j],
                                     layer_caches[i], page_table, batch, seq)
        new_state.extend(state)
        final_g = final_norm_g if last else None
        xp, xs = _ffn(xp, *f2, final_g), _ffn(xs, *f2, final_g)
    return (xp.reshape(batch, seq, D_MODEL), xs.reshape(-1, 1, D_MODEL), *new_state)
```

```python
import functools
import math

import numpy as np
import jax
import jax.numpy as jnp
from jax import lax
from jax.experimental import pallas as pl
from jax.experimental.pallas import tpu as pltpu

F32 = jnp.float32
BF16 = jnp.bfloat16
I32 = jnp.int32

D_MODEL = 1024
HEAD_DIM = 64
D_FF = 2816
RMS_EPS = 1e-6
NEG_INF = -1e30
SCALE = HEAD_DIM ** -0.5
N_MIXERS = 3

NUM_BUCKETS = 32
REL_MAX_DIST = 2048

A_GROUPS = ((128, 1), (512, 4), (2048, 16))
A_GROUP_HEADS = 4
A_GROUP_COLS = A_GROUP_HEADS * HEAD_DIM
A_HEADS = A_GROUP_HEADS * len(A_GROUPS)
A_BAND = 128

B_HEADS = 12
B_KV_HEADS = 4
B_REP = B_HEADS // B_KV_HEADS
B_CMP_BLOCK = 32
B_CMP_STRIDE = 16
B_SEL_BLOCK = 64
B_TOPN = 16
B_WINDOW = 512
B_PHI_HIDDEN = 128
B_FORCE = 1e4
B_Q_COLS = B_HEADS * HEAD_DIM
B_KV_COLS = B_KV_HEADS * HEAD_DIM
B_GATE_COLS = 3 * B_HEADS

C_HEADS = 16
C_COLS = C_HEADS * HEAD_DIM

V7X_LANES = 128
V7X_VMEM_LIMIT_BYTES = 56 * 1024 * 1024
ROW_TILE = 512
FF_CHUNK = 256
HALF = HEAD_DIM


def _params(*sem):
    return pltpu.CompilerParams(dimension_semantics=sem, vmem_limit_bytes=V7X_VMEM_LIMIT_BYTES)


def _dot(a, b):
    return jnp.dot(a, b, preferred_element_type=F32)


def _dot_nt(a, b):
    return lax.dot_general(a, b, (((1,), (1,)), ((), ())), preferred_element_type=F32)


def _split3(x):
    hi = x.astype(BF16)
    r1 = x - hi.astype(F32)
    mid = r1.astype(BF16)
    lo = (r1 - mid.astype(F32)).astype(BF16)
    return hi, mid, lo


def _dot3(a_bf16_exact, x):
    hi, mid, lo = _split3(x)
    return _dot(a_bf16_exact, hi) + _dot(a_bf16_exact, mid) + _dot(a_bf16_exact, lo)


def _rms(x, g):
    return x * lax.rsqrt(jnp.mean(x * x, axis=-1, keepdims=True) + RMS_EPS) * g


def _lane_lo(shape):
    return (lax.broadcasted_iota(I32, shape, len(shape) - 1) % V7X_LANES) < HALF


def _pick_head(slab, half):
    lo = _lane_lo(slab.shape)
    return jnp.where(lo if half == 0 else jnp.logical_not(lo), slab, jnp.zeros_like(slab))


def _align_head(slab, src_half, dst_half):
    if src_half == dst_half:
        return slab
    return pltpu.roll(slab, HALF, 1)


def _masked_softmax(s, mask):
    s = jnp.where(mask, s, NEG_INF)
    m = jnp.max(s, axis=-1, keepdims=True)
    p = jnp.where(mask, jnp.exp(s - m), 0.0)
    den = jnp.sum(p, axis=-1, keepdims=True)
    return p, m, jnp.where(den > 0, den, 1.0)


def _rel_bucket_np(dist):
    exact = NUM_BUCKETS // 2
    d = np.maximum(dist, 0)
    logd = (np.log(np.maximum(d, 1).astype(np.float32) / np.float32(exact))
            / np.float32(math.log(REL_MAX_DIST / exact))).astype(np.float32)
    far = np.minimum(exact + (logd * np.float32(NUM_BUCKETS - exact)).astype(np.int32), NUM_BUCKETS - 1)
    return np.where(d < exact, d, far).astype(np.int32)


def _bucket_starts():
    b = _rel_bucket_np(np.arange(2 * REL_MAX_DIST + 1))
    assert np.all(np.diff(b) >= 0) and b[-1] == NUM_BUCKETS - 1
    return [int(np.argmax(b >= k)) for k in range(NUM_BUCKETS)]


_BUCKET_STARTS = _bucket_starts()


def _bias_of_dist(table, dist, head_axis=0):
    d = jnp.expand_dims(jnp.asarray(dist, I32), head_axis)
    t = table.astype(F32)
    shape = [1] * d.ndim
    shape[head_axis] = t.shape[1]
    full = tuple(t.shape[1] if a == head_axis else n for a, n in enumerate(d.shape))
    out = jnp.broadcast_to(t[0].reshape(shape), full)
    for k in range(1, NUM_BUCKETS):
        out = jnp.where(d >= _BUCKET_STARTS[k], t[k].reshape(shape), out)
    return out


def _col_dot(kt, q_col):
    return jnp.sum(kt * q_col, axis=0, keepdims=True)


def _row_max(x):
    return jnp.max(x, axis=-1, keepdims=True)


def _row_sum(x):
    return jnp.sum(x, axis=-1, keepdims=True)


def _ffn_kernel(x_ref, g_ref, wg_ref, wu_ref, wd_ref, *rest, final):
    x = x_ref[...]
    h = _rms(x, g_ref[...]).astype(BF16)
    acc = jnp.zeros_like(x)
    for c in range(D_FF // FF_CHUNK):
        sl = slice(c * FF_CHUNK, (c + 1) * FF_CHUNK)
        gate = _dot(h, wg_ref[:, sl])
        up = _dot(h, wu_ref[:, sl])
        act = (gate * jax.nn.sigmoid(gate) * up).astype(BF16)
        acc = acc + _dot(act, wd_ref[sl, :])
    y = x + 0.5 * acc
    if final:
        gf_ref, o_ref = rest
        o_ref[...] = _rms(y, gf_ref[...])
    else:
        (o_ref,) = rest
        o_ref[...] = y


def _row_tile(rows):
    return ROW_TILE if rows % ROW_TILE == 0 else rows


def _ffn(x, g, wg, wu, wd, final_g=None):
    rows = x.shape[0]
    tm = _row_tile(rows)
    row = pl.BlockSpec((tm, D_MODEL), lambda i: (i, 0))
    vec = pl.BlockSpec((1, D_MODEL), lambda i: (0, 0))
    whole = lambda a: pl.BlockSpec(a.shape, lambda i: (0,) * a.ndim)
    args = [x, g.reshape(1, D_MODEL), wg, wu, wd]
    specs = [row, vec, whole(wg), whole(wu), whole(wd)]
    if final_g is not None:
        args.append(final_g.reshape(1, D_MODEL))
        specs.append(vec)
    return pl.pallas_call(
        functools.partial(_ffn_kernel, final=final_g is not None),
        grid=(rows // tm,), in_specs=specs, out_specs=row,
        out_shape=jax.ShapeDtypeStruct(x.shape, F32),
        compiler_params=_params("arbitrary"), name="macaron_swiglu")(*args)


def _out_proj_kernel(x_ref, o_ref, w_ref, y_ref):
    y_ref[...] = x_ref[...] + _dot(o_ref[...].astype(BF16), w_ref[...])


def _out_proj(x, o, w):
    rows = x.shape[0]
    tm = _row_tile(rows)
    cols = o.shape[1]
    return pl.pallas_call(
        _out_proj_kernel, grid=(rows // tm,),
        in_specs=[pl.BlockSpec((tm, D_MODEL), lambda i: (i, 0)),
                  pl.BlockSpec((tm, cols), lambda i: (i, 0)),
                  pl.BlockSpec(w.shape, lambda i: (0, 0))],
        out_specs=pl.BlockSpec((tm, D_MODEL), lambda i: (i, 0)),
        out_shape=jax.ShapeDtypeStruct(x.shape, F32),
        compiler_params=_params("arbitrary"), name="mixer_out_proj")(x, o, w)


def _a_proj_kernel(x_ref, g_ref, w_ref, q0_ref, q1_ref, q2_ref, kv0_ref, kv1_ref, kv2_ref):
    h = _rms(x_ref[...], g_ref[...]).astype(BF16)
    nq = A_HEADS * HEAD_DIM
    for gi, (q_ref, kv_ref) in enumerate(((q0_ref, kv0_ref), (q1_ref, kv1_ref), (q2_ref, kv2_ref))):
        c0 = gi * A_GROUP_COLS
        q_ref[...] = _dot(h, w_ref[:, c0:c0 + A_GROUP_COLS]).astype(BF16)
        kv_ref[:, :A_GROUP_COLS] = _dot(h, w_ref[:, nq + c0:nq + c0 + A_GROUP_COLS])
        kv_ref[:, A_GROUP_COLS:] = _dot(h, w_ref[:, 2 * nq + c0:2 * nq + c0 + A_GROUP_COLS])


def _a_proj(x, g, w):
    rows = x.shape[0]
    tm = _row_tile(rows)
    qspec = pl.BlockSpec((tm, A_GROUP_COLS), lambda i: (i, 0))
    kvspec = pl.BlockSpec((tm, 2 * A_GROUP_COLS), lambda i: (i, 0))
    return pl.pallas_call(
        _a_proj_kernel, grid=(rows // tm,),
        in_specs=[pl.BlockSpec((tm, D_MODEL), lambda i: (i, 0)),
                  pl.BlockSpec((1, D_MODEL), lambda i: (0, 0)),
                  pl.BlockSpec(w.shape, lambda i: (0, 0))],
        out_specs=[qspec] * 3 + [kvspec] * 3,
        out_shape=[jax.ShapeDtypeStruct((rows, A_GROUP_COLS), BF16)] * 3
        + [jax.ShapeDtypeStruct((rows, 2 * A_GROUP_COLS), F32)] * 3,
        compiler_params=_params("arbitrary"), name="dilated_in_proj")(x, g.reshape(1, D_MODEL), w)


def _a_attn_kernel(q_ref, kvc_ref, kvp_ref, bias_ref, o_ref, l_ref):
    blk = pl.program_id(2)
    band = A_BAND
    q = q_ref[0]
    kvc = kvc_ref[0]
    kvp = kvp_ref[0]
    qi = lax.broadcasted_iota(I32, (band, 2 * band), 0)
    kj = lax.broadcasted_iota(I32, (band, 2 * band), 1)
    off = qi + band - kj
    mask = (off >= 0) & (off <= band) & ((kj >= band) | (blk > 0))
    lo = _lane_lo((band, V7X_LANES))
    for hp in range(A_GROUP_HEADS // 2):
        ksl = slice(hp * V7X_LANES, (hp + 1) * V7X_LANES)
        vsl = slice(A_GROUP_COLS + hp * V7X_LANES, A_GROUP_COLS + (hp + 1) * V7X_LANES)
        k = jnp.concatenate([kvp[:, ksl], kvc[:, ksl]], axis=0).astype(BF16)
        v = jnp.concatenate([kvp[:, vsl], kvc[:, vsl]], axis=0).astype(BF16)
        qs = q[:, ksl]
        outs, lses = [], []
        for half in range(2):
            s = _dot_nt(_pick_head(qs, half), k) * SCALE + bias_ref[2 * hp + half]
            p, m, den = _masked_softmax(s, mask)
            outs.append(_dot(p.astype(BF16), v) / den)
            lses.append(jnp.broadcast_to(m + jnp.log(den), (band, V7X_LANES)))
        o_ref[0, :, ksl] = jnp.where(lo, outs[0], outs[1])
        l_ref[0, :, ksl] = jnp.where(lo, lses[0], lses[1])


def _a_attn_group(q, kv, bias, batch, seq, dil):
    sub = seq // dil
    nb = sub // A_BAND
    qv = q.reshape(batch, sub, dil * A_GROUP_COLS)
    kvv = kv.reshape(batch, sub, dil * 2 * A_GROUP_COLS)
    qspec = pl.BlockSpec((1, A_BAND, A_GROUP_COLS), lambda b, r, i: (b, i, r))
    o, lse = pl.pallas_call(
        _a_attn_kernel, grid=(batch, dil, nb),
        in_specs=[qspec,
                  pl.BlockSpec((1, A_BAND, 2 * A_GROUP_COLS), lambda b, r, i: (b, i, r)),
                  pl.BlockSpec((1, A_BAND, 2 * A_GROUP_COLS), lambda b, r, i: (b, jnp.maximum(i - 1, 0), r)),
                  pl.BlockSpec(bias.shape, lambda b, r, i: (0, 0, 0))],
        out_specs=[qspec, qspec],
        out_shape=[jax.ShapeDtypeStruct(qv.shape, F32)] * 2,
        compiler_params=_params("arbitrary", "arbitrary", "arbitrary"), name="dilated_band_attention")(
            qv, kvv, kvv, bias)
    return o.reshape(batch * seq, A_GROUP_COLS), lse.reshape(batch * seq, A_GROUP_COLS)


def _a_band_bias(table, g, dil):
    off = np.arange(A_BAND)[:, None] + A_BAND - np.arange(2 * A_BAND)[None, :]
    dist = np.clip(off, 0, A_BAND) * dil
    return _bias_of_dist(table[:, g * A_GROUP_HEADS:(g + 1) * A_GROUP_HEADS], dist)


def _a_out_kernel(x_ref, o0_ref, o1_ref, o2_ref, l0_ref, l1_ref, l2_ref, w_ref, y_ref):
    ls = [l0_ref[...], l1_ref[...], l2_ref[...]]
    m = jnp.maximum(jnp.maximum(ls[0], ls[1]), ls[2])
    es = [jnp.exp(l - m) for l in ls]
    den = es[0] + es[1] + es[2]
    y = x_ref[...]
    for gi, o_ref in enumerate((o0_ref, o1_ref, o2_ref)):
        og = (o_ref[...] * (es[gi] / den)).astype(BF16)
        y = y + _dot(og, w_ref[gi * A_GROUP_COLS:(gi + 1) * A_GROUP_COLS, :])
    y_ref[...] = y


def _a_out(x, outs, lses, w):
    rows = x.shape[0]
    tm = _row_tile(rows)
    gspec = pl.BlockSpec((tm, A_GROUP_COLS), lambda i: (i, 0))
    xspec = pl.BlockSpec((tm, D_MODEL), lambda i: (i, 0))
    return pl.pallas_call(
        _a_out_kernel, grid=(rows // tm,),
        in_specs=[xspec] + [gspec] * 6 + [pl.BlockSpec(w.shape, lambda i: (0, 0))],
        out_specs=xspec, out_shape=jax.ShapeDtypeStruct(x.shape, F32),
        compiler_params=_params("arbitrary"), name="dilated_combine_out_proj")(x, *outs, *lses, w)


def _a_step_kernel(q0_ref, q1_ref, q2_ref, n0_ref, n1_ref, n2_ref, b0_ref, b1_ref, b2_ref,
                   bias0_ref, bias1_ref, bias2_ref, self_ref, o_ref):
    groups = ((q0_ref, n0_ref, b0_ref, bias0_ref), (q1_ref, n1_ref, b1_ref, bias1_ref),
              (q2_ref, n2_ref, b2_ref, bias2_ref))
    outs, lses = [], []
    for gi, (q_ref, n_ref, b_ref, bias_ref) in enumerate(groups):
        dil = A_GROUPS[gi][1]
        win = b_ref.shape[4]
        mask = lax.broadcasted_iota(I32, (1, win), 1) % dil == 0
        q = q_ref[0].astype(F32)
        new = n_ref[0]
        for j in range(A_GROUP_HEADS):
            head = gi * A_GROUP_HEADS + j
            rows = slice(j * HEAD_DIM, (j + 1) * HEAD_DIM)
            v_rows = slice(A_GROUP_COLS + j * HEAD_DIM, A_GROUP_COLS + (j + 1) * HEAD_DIM)
            s = jnp.where(mask, _col_dot(b_ref[0, 0, j], q[rows]) * SCALE + bias_ref[j:j + 1, :], NEG_INF)
            s_self = _col_dot(new[rows], q[rows]) * SCALE + self_ref[head:head + 1, 0:1]
            m = jnp.maximum(_row_max(s), s_self)
            pb = jnp.where(mask, jnp.exp(s - m), 0.0)
            ps = jnp.exp(s_self - m)
            den = _row_sum(pb) + ps
            outs.append((_row_sum(b_ref[0, 1, j] * pb) + ps * new[v_rows]) / den)
            lses.append(m + jnp.log(den))
    n_groups = len(groups)
    for j in range(A_GROUP_HEADS):
        ls = [lses[gi * A_GROUP_HEADS + j] for gi in range(n_groups)]
        m = functools.reduce(jnp.maximum, ls)
        es = [jnp.exp(l - m) for l in ls]
        den = functools.reduce(lambda a, b: a + b, es)
        for gi in range(n_groups):
            head = gi * A_GROUP_HEADS + j
            o_ref[0, head * HEAD_DIM:(head + 1) * HEAD_DIM, :] = outs[head] * (es[gi] / den)


def _a_step_attn(qs, news, bufs, table):
    n = qs[0].shape[0]
    cols = lambda a: a.reshape(n, a.shape[1], 1)
    views = [b.transpose(0, 2, 3, 4, 1) for b in bufs]
    biases = []
    for g, (win, _) in enumerate(A_GROUPS):
        assert bufs[g].shape[1] == win, "the step kernel reads a full window buffer"
        biases.append(_bias_of_dist(table[:, g * A_GROUP_HEADS:(g + 1) * A_GROUP_HEADS], win - np.arange(win)))
    self_bias = jnp.broadcast_to(_bias_of_dist(table, np.zeros((1,), np.int64)), (A_HEADS, V7X_LANES))
    col_spec = lambda c: pl.BlockSpec((1, c, 1), lambda b: (b, 0, 0))
    whole = lambda a: pl.BlockSpec(a.shape, lambda b: (0,) * a.ndim)
    o = pl.pallas_call(
        _a_step_kernel, grid=(n,),
        in_specs=[col_spec(A_GROUP_COLS)] * 3 + [col_spec(2 * A_GROUP_COLS)] * 3
        + [pl.BlockSpec((1,) + v.shape[1:], lambda b: (b, 0, 0, 0, 0)) for v in views]
        + [whole(b) for b in biases] + [whole(self_bias)],
        out_specs=col_spec(A_HEADS * HEAD_DIM),
        out_shape=jax.ShapeDtypeStruct((n, A_HEADS * HEAD_DIM, 1), F32),
        compiler_params=_params("arbitrary"), name="dilated_step_attention")(
            *[cols(q) for q in qs], *[cols(x) for x in news], *views, *biases, self_bias)
    return o.reshape(n, A_HEADS * HEAD_DIM)


def _mixer_a(xp, xs, g, w_in, w_out, table, caches, batch, seq):
    w_in = w_in.astype(BF16)
    w_out = w_out.astype(BF16)
    pq0, pq1, pq2, pkv0, pkv1, pkv2 = _a_proj(xp, g, w_in)
    outs, lses, state = [], [], []
    for gi, ((win, dil), q, kv) in enumerate(zip(A_GROUPS, (pq0, pq1, pq2), (pkv0, pkv1, pkv2))):
        o, l = _a_attn_group(q, kv, _a_band_bias(table, gi, dil), batch, seq, dil)
        outs.append(o)
        lses.append(l)
        keep = min(win, seq)
        state.append(kv.reshape(batch, seq, 2, A_GROUP_HEADS, HEAD_DIM)[:, seq - keep:])
    yp = _a_out(xp, outs, lses, w_out)

    n = xs.shape[0]
    sq0, sq1, sq2, skv0, skv1, skv2 = _a_proj(xs, g, w_in)
    so = _a_step_attn((sq0, sq1, sq2), (skv0, skv1, skv2), caches, table)
    ys = _out_proj(xs, so, w_out)
    for (win, _), kv, buf in zip(A_GROUPS, (skv0, skv1, skv2), caches):
        new = kv.reshape(n, 1, 2, A_GROUP_HEADS, HEAD_DIM)
        state.append(jnp.concatenate([buf, new], axis=1)[:, buf.shape[1] + 1 - min(win, buf.shape[1] + 1):])
    return yp, ys, state


C_TAIL = V7X_LANES


def _log_sigmoid(z):
    return -(jnp.maximum(-z, 0.0) + jnp.log1p(jnp.exp(-jnp.abs(z))))


def _c_proj_kernel(x_ref, g_ref, w_ref, wt_ref, b_ref, tri_ref, q_ref, k_ref, v_ref, lf_ref, dc_ref, carry_ref,
                   *, tiles_per_seq):
    i = pl.program_id(0)
    h = _rms(x_ref[...], g_ref[...]).astype(BF16)
    q_ref[...] = _dot(h, w_ref[:, 0:C_COLS]).astype(BF16)
    k_ref[...] = _dot(h, w_ref[:, C_COLS:2 * C_COLS])
    v_ref[...] = _dot(h, w_ref[:, 2 * C_COLS:3 * C_COLS])
    logf = _log_sigmoid(_dot(h, wt_ref[...]) + b_ref[...])
    lf_ref[...] = logf

    @pl.when(i % tiles_per_seq == 0)
    def _():
        carry_ref[...] = jnp.zeros_like(carry_ref)

    cum = _dot3(tri_ref[...], logf) + carry_ref[0:1, :]
    dc_ref[...] = cum
    carry_ref[...] = jnp.broadcast_to(cum[cum.shape[0] - 1:, :], carry_ref.shape)


def _c_proj(x, g, w, w_tail, bias, seq):
    rows = x.shape[0]
    tm = _row_tile(rows)
    tiles_per_seq = max(seq // tm, 1)
    tri = jnp.asarray(np.tril(np.ones((tm, tm), np.float32)), BF16)
    big = pl.BlockSpec((tm, C_COLS), lambda i: (i, 0))
    small = pl.BlockSpec((tm, C_TAIL), lambda i: (i, 0))
    return pl.pallas_call(
        functools.partial(_c_proj_kernel, tiles_per_seq=tiles_per_seq), grid=(rows // tm,),
        in_specs=[pl.BlockSpec((tm, D_MODEL), lambda i: (i, 0)),
                  pl.BlockSpec((1, D_MODEL), lambda i: (0, 0)),
                  pl.BlockSpec(w.shape, lambda i: (0, 0)),
                  pl.BlockSpec(w_tail.shape, lambda i: (0, 0)),
                  pl.BlockSpec((1, C_TAIL), lambda i: (0, 0)),
                  pl.BlockSpec(tri.shape, lambda i: (0, 0))],
        out_specs=[big, big, big, small, small],
        out_shape=[jax.ShapeDtypeStruct((rows, C_COLS), BF16)] + [jax.ShapeDtypeStruct((rows, C_COLS), F32)] * 2
        + [jax.ShapeDtypeStruct((rows, C_TAIL), F32)] * 2,
        scratch_shapes=[pltpu.VMEM((8, C_TAIL), F32)],
        compiler_params=_params("arbitrary"), name="forget_in_proj")(
            x, g.reshape(1, D_MODEL), w, w_tail, bias, tri)


C_TILE = 512


def _c_attn_kernel(q_ref, k_ref, v_ref, dq_ref, dk_ref, o_ref, m_ref, l_ref, acc_ref):
    hp = pl.program_id(1)
    i = pl.program_id(2)
    t = C_TILE
    q = q_ref[0]
    dq_tile = dq_ref[0]
    lane = lax.broadcasted_iota(I32, (t, V7X_LANES), 1)
    lo = lane < HALF
    qm = [_pick_head(q, 0), _pick_head(q, 1)]
    dq = [jnp.sum(jnp.where(lane == 2 * hp + half, dq_tile, 0.0), axis=1, keepdims=True) for half in range(2)]
    causal = lax.broadcasted_iota(I32, (t, t), 0) >= lax.broadcasted_iota(I32, (t, t), 1)
    m_ref[...] = jnp.full_like(m_ref, NEG_INF)
    l_ref[...] = jnp.zeros_like(l_ref)
    acc_ref[...] = jnp.zeros_like(acc_ref)

    def tile(c, diagonal):
        start = pl.multiple_of(c * t, t)
        k = k_ref[0, pl.ds(start, t), :].astype(BF16)
        v = v_ref[0, pl.ds(start, t), :].astype(BF16)
        dk = dk_ref[0, 0, c]
        for half in range(2):
            s = _dot_nt(qm[half], k) * SCALE + dq[half] - dk[half:half + 1, :]
            if diagonal:
                s = jnp.where(causal, s, NEG_INF)
            m_old = m_ref[half]
            m_new = jnp.maximum(m_old, jnp.max(s, axis=1, keepdims=True))
            p = jnp.exp(s - m_new[:, 0:1])
            if diagonal:
                p = jnp.where(causal, p, 0.0)
            alpha = jnp.exp(m_old - m_new)
            l_ref[half] = alpha * l_ref[half] + jnp.sum(p, axis=1, keepdims=True)
            acc_ref[half] = alpha * acc_ref[half] + _dot(p.astype(BF16), v)
            m_ref[half] = m_new

    def body(c, carry):
        tile(c, False)
        return carry

    lax.fori_loop(0, i, body, 0)
    tile(i, True)
    o_ref[0] = jnp.where(lo, acc_ref[0] / l_ref[0], acc_ref[1] / l_ref[1])


def _c_attn(q, k, v, dcum, batch, seq):
    t = C_TILE
    nt = seq // t
    pairs = C_HEADS // 2
    q3, k3, v3 = (a.reshape(batch, seq, C_COLS) for a in (q, k, v))
    dq = dcum.reshape(batch, seq, C_TAIL)
    dk = dq[:, :, :C_HEADS].transpose(0, 2, 1).reshape(batch, pairs, 2, nt, t).transpose(0, 1, 3, 2, 4)
    qspec = pl.BlockSpec((1, t, V7X_LANES), lambda b, h, i: (b, i, h))
    kspec = pl.BlockSpec((1, seq, V7X_LANES), lambda b, h, i: (b, 0, h))
    o = pl.pallas_call(
        _c_attn_kernel, grid=(batch, pairs, nt),
        in_specs=[qspec, kspec, kspec,
                  pl.BlockSpec((1, t, C_TAIL), lambda b, h, i: (b, i, 0)),
                  pl.BlockSpec((1, 1, nt, 2, t), lambda b, h, i: (b, h, 0, 0, 0))],
        out_specs=qspec, out_shape=jax.ShapeDtypeStruct((batch, seq, C_COLS), F32),
        scratch_shapes=[pltpu.VMEM((2, t, V7X_LANES), F32)] * 3,
        compiler_params=_params("arbitrary", "arbitrary", "arbitrary"), name="forget_attention")(
            q3, k3, v3, dq, dk)
    return o.reshape(batch * seq, C_COLS)


C_STEP_PAGES = 4


def _c_step_kernel(pt_ref, q_ref, kn_ref, vn_ref, lfn_ref, *rest):
    page_refs = rest[:3 * C_STEP_PAGES]
    lower_ref, o_ref, qb_ref, m_ref, l_ref, acc_ref, carry_ref = rest[3 * C_STEP_PAGES:]
    j = pl.program_id(1)
    page = qb_ref.shape[1]
    heads = (C_HEADS, HEAD_DIM)

    def per_head(x):
        return jnp.broadcast_to(x[:, None, :], heads + (x.shape[1],)).reshape(C_COLS, x.shape[1])

    @pl.when(j == 0)
    def _():
        q = q_ref[0].astype(F32)
        lane0 = lax.broadcasted_iota(I32, (C_COLS, page), 1) == 0
        qb_ref[...] = jnp.broadcast_to(q, (C_COLS, page))
        s_self = jnp.sum((kn_ref[0] * q).reshape(heads + (1,)), axis=1) * SCALE
        m_ref[...] = jnp.broadcast_to(s_self, (C_HEADS, page))
        l_ref[...] = jnp.where(lax.broadcasted_iota(I32, (C_HEADS, page), 1) == 0, 1.0, 0.0)
        acc_ref[...] = jnp.where(lane0, jnp.broadcast_to(vn_ref[0], (C_COLS, page)), 0.0)
        carry_ref[...] = jnp.broadcast_to(lfn_ref[0], (C_HEADS, page))

    for pg in range(C_STEP_PAGES):
        k_ref, v_ref, lf_ref = page_refs[3 * pg:3 * pg + 3]
        lf = lf_ref[0]
        hi, mid, lo3 = _split3(lf)
        decay = _dot(hi, lower_ref[...]) + _dot(mid, lower_ref[...]) + _dot(lo3, lower_ref[...]) + carry_ref[...]
        s = jnp.sum((k_ref[0] * qb_ref[...]).reshape(heads + (page,)), axis=1) * SCALE + decay
        m_old = m_ref[...]
        m_new = jnp.maximum(m_old, _row_max(s))
        p = jnp.exp(s - m_new)
        alpha = jnp.exp(m_old - m_new)
        l_ref[...] = alpha * l_ref[...] + p
        acc_ref[...] = per_head(alpha) * acc_ref[...] + per_head(p) * v_ref[0]
        m_ref[...] = m_new
        carry_ref[...] = carry_ref[...] + _row_sum(lf)

    @pl.when(j == pl.num_programs(1) - 1)
    def _():
        o_ref[0] = _row_sum(acc_ref[...]) / per_head(_row_sum(l_ref[...]))


def _c_step_attn(q, k_new, v_new, lf_new, cache_k, cache_v, cache_logf, page_table):
    n, n_pages = page_table.shape
    page = cache_k.shape[1]
    assert n_pages % C_STEP_PAGES == 0
    lower = jnp.asarray(np.tril(np.ones((page, page), np.float32), -1), BF16)
    col = lambda c: pl.BlockSpec((1, c, 1), lambda b, j, pt: (b, 0, 0))

    def paged(rows, pg):
        return pl.BlockSpec((1, rows, page), lambda b, j, pt: (pt[b, n_pages - 1 - (j * C_STEP_PAGES + pg)], 0, 0))

    k_view = cache_k.transpose(0, 2, 3, 1).reshape(-1, C_COLS, page)
    v_view = cache_v.transpose(0, 2, 3, 1).reshape(-1, C_COLS, page)
    lf_view = cache_logf.transpose(0, 2, 1)
    page_specs, page_args = [], []
    for pg in range(C_STEP_PAGES):
        page_specs += [paged(C_COLS, pg), paged(C_COLS, pg), paged(C_HEADS, pg)]
        page_args += [k_view, v_view, lf_view]
    o = pl.pallas_call(
        _c_step_kernel,
        grid_spec=pltpu.PrefetchScalarGridSpec(
            num_scalar_prefetch=1, grid=(n, n_pages // C_STEP_PAGES),
            in_specs=[col(C_COLS), col(C_COLS), col(C_COLS), col(C_HEADS)] + page_specs
            + [pl.BlockSpec(lower.shape, lambda b, j, pt: (0, 0))],
            out_specs=col(C_COLS),
            scratch_shapes=[pltpu.VMEM((C_COLS, page), F32), pltpu.VMEM((C_HEADS, page), F32),
                            pltpu.VMEM((C_HEADS, page), F32), pltpu.VMEM((C_COLS, page), F32),
                            pltpu.VMEM((C_HEADS, page), F32)]),
        out_shape=jax.ShapeDtypeStruct((n, C_COLS, 1), F32),
        compiler_params=_params("arbitrary", "arbitrary"), name="forget_paged_step")(
            page_table, q.reshape(n, C_COLS, 1), k_new.reshape(n, C_COLS, 1), v_new.reshape(n, C_COLS, 1),
            lf_new[:, :C_HEADS].reshape(n, C_HEADS, 1), *page_args, lower)
    return o.reshape(n, C_COLS)


def _mixer_c(xp, xs, g, w_in, forget_bias, w_out, caches, page_table, batch, seq):
    w = w_in[:, :3 * C_COLS].astype(BF16)
    w_tail = jnp.pad(w_in[:, 3 * C_COLS:], ((0, 0), (0, C_TAIL - C_HEADS))).astype(BF16)
    bias = jnp.pad(forget_bias, (0, C_TAIL - C_HEADS)).reshape(1, C_TAIL)
    w_out = w_out.astype(BF16)
    cache_k, cache_v, cache_logf = caches

    q, k, v, logf, dcum = _c_proj(xp, g, w, w_tail, bias, seq)
    o = _c_attn(q, k, v, dcum, batch, seq)
    yp = _out_proj(xp, o, w_out)
    state = [k.reshape(batch, seq, C_HEADS, HEAD_DIM), v.reshape(batch, seq, C_HEADS, HEAD_DIM),
             logf[:, :C_HEADS].reshape(batch, seq, C_HEADS)]

    n = xs.shape[0]
    sq, sk, sv, slogf, _ = _c_proj(xs, g, w, w_tail, bias, 1)
    so = _c_step_attn(sq, sk, sv, slogf, cache_k, cache_v, cache_logf, page_table)
    ys = _out_proj(xs, so, w_out)
    state += [sk.reshape(n, 1, C_HEADS, HEAD_DIM), sv.reshape(n, 1, C_HEADS, HEAD_DIM),
              slogf[:, :C_HEADS].reshape(n, 1, C_HEADS)]
    return yp, ys, state


B_TAIL = V7X_LANES
B_SLAB = 2 * B_KV_COLS


def _b_proj_kernel(x_ref, g_ref, w_ref, wt_ref, b_ref, q_ref, cmp_ref, sel_ref, win_ref, gate_ref):
    h = _rms(x_ref[...], g_ref[...]).astype(BF16)
    q_ref[...] = _dot(h, w_ref[:, 0:B_Q_COLS]).astype(BF16)
    c0 = B_Q_COLS
    cmp_ref[...] = _dot(h, w_ref[:, c0:c0 + B_SLAB])
    sel_ref[...] = _dot(h, w_ref[:, c0 + B_SLAB:c0 + 2 * B_SLAB])
    win_ref[...] = _dot(h, w_ref[:, c0 + 2 * B_SLAB:c0 + 3 * B_SLAB])
    gate_ref[...] = jax.nn.sigmoid(_dot(h, wt_ref[...]) + b_ref[...])


def _b_proj(x, g, w, w_tail, bias):
    rows = x.shape[0]
    tm = _row_tile(rows)
    spec = lambda cols: pl.BlockSpec((tm, cols), lambda i: (i, 0))
    shape = lambda cols, dt: jax.ShapeDtypeStruct((rows, cols), dt)
    return pl.pallas_call(
        _b_proj_kernel, grid=(rows // tm,),
        in_specs=[spec(D_MODEL), pl.BlockSpec((1, D_MODEL), lambda i: (0, 0)),
                  pl.BlockSpec(w.shape, lambda i: (0, 0)), pl.BlockSpec(w_tail.shape, lambda i: (0, 0)),
                  pl.BlockSpec((1, B_TAIL), lambda i: (0, 0))],
        out_specs=[spec(B_Q_COLS)] + [spec(B_SLAB)] * 3 + [spec(B_TAIL)],
        out_shape=[shape(B_Q_COLS, BF16)] + [shape(B_SLAB, F32)] * 3 + [shape(B_TAIL, F32)],
        compiler_params=_params("arbitrary"), name="sparse_in_proj_step")(x, g.reshape(1, D_MODEL), w, w_tail, bias)


B_GROUP_Q = 2 * V7X_LANES
B_GQ_COLS = B_KV_HEADS * B_GROUP_Q
B_GKV_COLS = B_KV_HEADS * V7X_LANES
B_GGATE_COLS = B_KV_HEADS * V7X_LANES


def _b_proj_grouped_kernel(x_ref, g_ref, w_ref, wt_ref, b_ref, q_ref, cmp_ref, sel_ref, win_ref, selg_ref, wing_ref,
                           gate_ref):
    h = _rms(x_ref[...], g_ref[...]).astype(BF16)
    c = 0
    q_ref[...] = _dot(h, w_ref[:, c:c + B_GQ_COLS]).astype(BF16)
    c += B_GQ_COLS
    for ref in (cmp_ref, sel_ref, win_ref):
        ref[...] = _dot(h, w_ref[:, c:c + B_SLAB])
        c += B_SLAB
    for ref in (selg_ref, wing_ref):
        ref[...] = _dot(h, w_ref[:, c:c + B_GKV_COLS]).astype(BF16)
        c += B_GKV_COLS
    gate_ref[...] = jax.nn.sigmoid(_dot(h, wt_ref[...]) + b_ref[...])


def _b_grouped_weights(w_in, gate_bias):
    q = w_in[:, :B_Q_COLS].reshape(D_MODEL, B_KV_HEADS, B_REP * HEAD_DIM)
    q = jnp.pad(q, ((0, 0), (0, 0), (0, B_GROUP_Q - B_REP * HEAD_DIM))).reshape(D_MODEL, B_GQ_COLS)
    main = w_in[:, B_Q_COLS:B_Q_COLS + 3 * B_SLAB]

    def grouped(slab):
        kv = slab.reshape(D_MODEL, 2, B_KV_HEADS, HEAD_DIM)
        return kv.transpose(0, 2, 1, 3).reshape(D_MODEL, B_GKV_COLS)

    w = jnp.concatenate([q, main, grouped(main[:, B_SLAB:2 * B_SLAB]), grouped(main[:, 2 * B_SLAB:])], axis=1)

    def gates(a):
        a = a.reshape(a.shape[:-1] + (3, B_KV_HEADS, B_REP))
        a = jnp.moveaxis(a, -2, -3).reshape(a.shape[:-3] + (B_KV_HEADS, 3 * B_REP))
        pad = [(0, 0)] * (a.ndim - 1) + [(0, V7X_LANES - 3 * B_REP)]
        return jnp.pad(a, pad).reshape(a.shape[:-2] + (B_GGATE_COLS,))

    tail = w_in[:, B_Q_COLS + 3 * B_SLAB:]
    return w.astype(BF16), gates(tail).astype(BF16), gates(gate_bias).reshape(1, B_GGATE_COLS)


def _b_proj_grouped(x, g, w, w_tail, bias):
    rows = x.shape[0]
    tm = _row_tile(rows)
    spec = lambda cols: pl.BlockSpec((tm, cols), lambda i: (i, 0))
    shape = lambda cols, dt: jax.ShapeDtypeStruct((rows, cols), dt)
    return pl.pallas_call(
        _b_proj_grouped_kernel, grid=(rows // tm,),
        in_specs=[spec(D_MODEL), pl.BlockSpec((1, D_MODEL), lambda i: (0, 0)),
                  pl.BlockSpec(w.shape, lambda i: (0, 0)), pl.BlockSpec(w_tail.shape, lambda i: (0, 0)),
                  pl.BlockSpec((1, B_GGATE_COLS), lambda i: (0, 0))],
        out_specs=[spec(B_GQ_COLS)] + [spec(B_SLAB)] * 3 + [spec(B_GKV_COLS)] * 2 + [spec(B_GGATE_COLS)],
        out_shape=[shape(B_GQ_COLS, BF16)] + [shape(B_SLAB, F32)] * 3 + [shape(B_GKV_COLS, BF16)] * 2
        + [shape(B_GGATE_COLS, F32)],
        compiler_params=_params("arbitrary"), name="sparse_in_proj_prompt")(x, g.reshape(1, D_MODEL), w, w_tail, bias)


B_PHI_ROWS = 2 * B_KV_HEADS * B_PHI_HIDDEN
B_CHUNK_COLS = B_CMP_STRIDE * B_SLAB


def _compress_weights(phi_pos, phi_w1, phi_w2, pad_heads):
    half = B_CMP_BLOCK // B_CMP_STRIDE
    eye = jnp.eye(B_KV_HEADS, dtype=F32)
    w1r = phi_w1.reshape(2, half, B_CMP_STRIDE, HEAD_DIM, B_PHI_HIDDEN)
    first = jnp.einsum('cpsde,gh->cspgehd', w1r, eye).reshape(2, B_CMP_STRIDE, B_PHI_ROWS, B_KV_COLS)
    pos_term = jnp.einsum('cpsd,cpsde->cpe', phi_pos.reshape(2, half, B_CMP_STRIDE, HEAD_DIM), w1r,
                          precision=lax.Precision.HIGHEST)
    pos = jnp.broadcast_to(pos_term[:, :, None, :, None], (2, half, B_KV_HEADS, B_PHI_HIDDEN, V7X_LANES))
    second = jnp.einsum('ced,gh->cgdhe', phi_w2, eye)
    if pad_heads:
        second = jnp.pad(second, ((0, 0), (0, 0), (0, V7X_LANES - HEAD_DIM), (0, 0), (0, 0)))
    second = second.reshape(2, -1, B_KV_HEADS * B_PHI_HIDDEN)
    return first.astype(BF16), pos.reshape(2, B_PHI_ROWS, V7X_LANES), second.astype(BF16)


def _compress_first(load, first_ref, c):
    acc = None
    for s in range(B_CMP_STRIDE):
        part = _dot_nt(first_ref[c, s], load(s, c).astype(BF16))
        acc = part if acc is None else acc + part
    return acc


def _compress_second(acc, pos_ref, second_ref, c):
    n_chunks = acc.shape[1]
    acc = acc + pos_ref[c][:, 0:1]
    rows = B_PHI_ROWS // 2
    pre = acc[:rows] + pltpu.roll(acc[rows:], n_chunks - 1, 1)
    return _dot(second_ref[c], jax.nn.gelu(pre).astype(BF16))


def _b_compress_prompt_kernel(x_ref, first_ref, pos_ref, second_ref, o_ref):
    def load(s, c):
        c0 = s * B_SLAB + c * B_KV_COLS
        return x_ref[0, :, c0:c0 + B_KV_COLS]

    for c in range(2):
        o_ref[0, c] = _compress_second(_compress_first(load, first_ref, c), pos_ref, second_ref, c).astype(BF16)


def _b_compress_prompt(cmp, weights, batch, seq):
    n_chunks = seq // B_CMP_STRIDE
    first, pos, second = weights
    out_rows = second.shape[1]
    whole = lambda a: pl.BlockSpec(a.shape, lambda b: (0,) * a.ndim)
    return pl.pallas_call(
        _b_compress_prompt_kernel, grid=(batch,),
        in_specs=[pl.BlockSpec((1, n_chunks, B_CHUNK_COLS), lambda b: (b, 0, 0)),
                  whole(first), whole(pos), whole(second)],
        out_specs=pl.BlockSpec((1, 2, out_rows, n_chunks), lambda b: (b, 0, 0, 0)),
        out_shape=jax.ShapeDtypeStruct((batch, 2, out_rows, n_chunks), BF16),
        compiler_params=_params("arbitrary"), name="sparse_compress_prompt")(
            cmp.reshape(batch, n_chunks, B_CHUNK_COLS), first, pos, second).reshape(
                batch, 2, B_KV_HEADS, V7X_LANES, n_chunks)


def _b_compress_step_kernel(pt_ref, cache_ref, first_ref, pos_ref, second_ref, o_ref, buf_ref, acc_ref, sem_ref,
                            *, pages_per_half, chunks_per_page):
    n = pl.program_id(0)
    hf = pl.program_id(1)
    step = n * 2 + hf
    total = pl.num_programs(0) * 2
    slot = step % 2
    half_chunks = pages_per_half * chunks_per_page

    def copies(st, sl):
        nn = st // 2
        hh = st % 2
        return [pltpu.make_async_copy(cache_ref.at[pt_ref[nn, hh * pages_per_half + j]],
                                      buf_ref.at[sl, pl.ds(j * chunks_per_page, chunks_per_page), :],
                                      sem_ref.at[sl]) for j in range(pages_per_half)]

    @pl.when(step == 0)
    def _():
        for cp in copies(0, 0):
            cp.start()

    @pl.when(step + 1 < total)
    def _():
        for cp in copies(step + 1, 1 - slot):
            cp.start()

    for cp in copies(step, slot):
        cp.wait()

    def load(s, c):
        c0 = s * B_SLAB + c * B_KV_COLS
        return buf_ref[slot, :, c0:c0 + B_KV_COLS]

    for c in range(2):
        part = _compress_first(load, first_ref, c)

        @pl.when(hf == 0)
        def _():
            acc_ref[c, :, 0:half_chunks] = part

        @pl.when(hf == 1)
        def _():
            acc_ref[c, :, half_chunks:2 * half_chunks] = part

    @pl.when(hf == 1)
    def _():
        for c in range(2):
            o_ref[0, c] = _compress_second(acc_ref[c], pos_ref, second_ref, c).astype(BF16)


def _b_compress_step(cache_cmp, page_table, weights):
    n, n_pages = page_table.shape
    page = cache_cmp.shape[1]
    chunks_per_page = page // B_CMP_STRIDE
    n_chunks = n_pages * chunks_per_page
    pages_per_half = n_pages // 2
    first, pos, second = weights
    whole = lambda a: pl.BlockSpec(a.shape, lambda b, h, pt: (0,) * a.ndim)
    return pl.pallas_call(
        functools.partial(_b_compress_step_kernel, pages_per_half=pages_per_half, chunks_per_page=chunks_per_page),
        grid_spec=pltpu.PrefetchScalarGridSpec(
            num_scalar_prefetch=1, grid=(n, 2),
            in_specs=[pl.BlockSpec(memory_space=pl.ANY), whole(first), whole(pos), whole(second)],
            out_specs=pl.BlockSpec((1, 2, B_KV_COLS, n_chunks), lambda b, h, pt: (b, 0, 0, 0)),
            scratch_shapes=[pltpu.VMEM((2, n_chunks // 2, B_CHUNK_COLS), F32),
                            pltpu.VMEM((2, B_PHI_ROWS, n_chunks), F32),
                            pltpu.SemaphoreType.DMA((2,))]),
        out_shape=jax.ShapeDtypeStruct((n, 2, B_KV_COLS, n_chunks), BF16),
        compiler_params=_params("arbitrary", "arbitrary"), name="sparse_compress_step")(
            page_table, cache_cmp.reshape(-1, chunks_per_page, B_CHUNK_COLS), first, pos, second)


B_TILE = 128
B_FAR_TILES = REL_MAX_DIST // B_TILE + 2


def _toeplitz_tiles(table):
    shape = (B_FAR_TILES + 1, B_TILE, B_TILE)
    dist = (B_TILE * (lax.broadcasted_iota(I32, shape, 0) - 1) + lax.broadcasted_iota(I32, shape, 1)
            - lax.broadcasted_iota(I32, shape, 2))
    return jnp.where(dist[None] >= 0, _bias_of_dist(table, dist), NEG_INF)


def _cmp_bias_tiles(table, seq, n_chunks):
    shape = (seq // B_TILE, B_TILE, n_chunks)
    dist = (B_TILE * lax.broadcasted_iota(I32, shape, 0) + lax.broadcasted_iota(I32, shape, 1)
            - B_CMP_STRIDE * lax.broadcasted_iota(I32, shape, 2) - (B_CMP_BLOCK - 1))
    return _bias_of_dist(table, dist, head_axis=1)


def _overlap_t(n_sel_padded, n_cmp_padded, n_sel, n_cmp):
    c_start = np.arange(n_cmp_padded)[None, :] * B_CMP_STRIDE
    s_start = np.arange(n_sel_padded)[:, None] * B_SEL_BLOCK
    ov = (c_start < s_start + B_SEL_BLOCK) & (c_start + B_CMP_BLOCK > s_start)
    ov &= (np.arange(n_cmp_padded)[None, :] < n_cmp) & (np.arange(n_sel_padded)[:, None] < n_sel)
    return ov.astype(np.float32)


B_SEL_CHUNK = 512
B_SEL_PAD = 128
B_PAD_SCORE = -3e30


def _b_attn_kernel(q_ref, gate_ref, kvt_ref, tcmp_ref, sel_ref, win_ref, ttab_ref, ovt_ref, o_ref,
                   l_ref, acc_ref, imp_ref, *, n_sel):
    g = pl.program_id(1)
    i = pl.program_id(2)
    t = B_TILE
    rows = B_REP * t
    ch = B_SEL_CHUNK
    tiles_per_chunk = ch // t
    q = q_ref[0]
    gate = gate_ref[0]
    n_cmp_pad = kvt_ref.shape[4]
    qpos_c = i * t + lax.broadcasted_iota(I32, (rows, n_cmp_pad), 0) % t
    cidx = lax.broadcasted_iota(I32, (rows, n_cmp_pad), 1)
    mask_c = qpos_c >= cidx * B_CMP_STRIDE + (B_CMP_BLOCK - 1)
    blk = lax.broadcasted_iota(I32, (n_sel, t), 0)
    cur = (i * t + lax.broadcasted_iota(I32, (n_sel, t), 1)) // B_SEL_BLOCK
    forced = (blk == 0) | (blk == cur) | (blk == cur - 1)
    kb = lax.broadcasted_iota(I32, (ch, B_SEL_PAD), 1)
    kblk = lax.broadcasted_iota(I32, (ch, B_SEL_PAD), 0) // B_SEL_BLOCK
    n_win_tiles = B_WINDOW // t + 1
    wq = lax.broadcasted_iota(I32, (rows, n_win_tiles * t), 0) % t
    wk = lax.broadcasted_iota(I32, (rows, n_win_tiles * t), 1)
    mask_w = (B_WINDOW + wq - wk < B_WINDOW) & (wk // t >= n_win_tiles - 1 - i)
    n_chunks = (i + tiles_per_chunk) // tiles_per_chunk
    heads = lambda x: [x[r * t:(r + 1) * t] for r in range(B_REP)]
    q_all = jnp.concatenate(
        [_align_head(_pick_head(q[:, (r // 2) * V7X_LANES:(r // 2 + 1) * V7X_LANES], r % 2), r % 2, 0)
         for r in range(B_REP)], axis=0)

    def bias_rows(tiles):
        return jnp.concatenate([jnp.concatenate([ttab_ref[g * B_REP + r, d] for d in tiles], axis=1)
                                for r in range(B_REP)], axis=0)

    s = _dot(q_all, kvt_ref[0, 0, 0]) * SCALE + tcmp_ref[0].reshape(rows, n_cmp_pad)
    p, _, den = _masked_softmax(s, mask_c)
    p = p / den
    o_cmp = heads(_dot_nt(p.astype(BF16), kvt_ref[0, 1, 0]))
    p_sum = functools.reduce(lambda a, b: a + b, heads(p))

    hi, mid, lo3 = _split3(p_sum)
    imp = _dot_nt(ovt_ref[...], hi) + _dot_nt(ovt_ref[...], mid) + _dot_nt(ovt_ref[...], lo3)
    imp = jnp.where(forced, B_FORCE, jnp.where(blk <= cur, imp, NEG_INF))
    imp_ref[...] = imp

    def rank_body(j, rank):
        row = imp_ref[pl.ds(j, 1), :]
        ahead = (row > imp) | ((row == imp) & (j < blk))
        return rank + jnp.where(ahead, 1.0, 0.0)

    n_visible = jnp.minimum(n_sel, (i * t + t - 1) // B_SEL_BLOCK + 1)
    rank = lax.fori_loop(0, n_visible, rank_body, jnp.zeros((n_sel, t), F32))

    penalty = jnp.where(rank < B_TOPN, 0.0, NEG_INF)
    if n_sel < B_SEL_PAD:
        penalty = jnp.concatenate([penalty, jnp.zeros((B_SEL_PAD - n_sel, t), F32)], axis=0)
    penalty = jnp.transpose(penalty).astype(BF16)
    lhs = jnp.concatenate([q_all * SCALE, jnp.concatenate([penalty] * B_REP, axis=0)], axis=1)

    def scores(c):
        start = pl.multiple_of(c * ch, ch)
        kv = sel_ref[0, pl.ds(start, ch), :]
        onehot = jnp.where(kb == kblk + c * (ch // B_SEL_BLOCK), 1.0, 0.0).astype(BF16)
        tiles = [jnp.clip(i - c * tiles_per_chunk - j, -1, B_FAR_TILES - 1) + 1 for j in range(tiles_per_chunk)]
        return _dot_nt(lhs, jnp.concatenate([kv, onehot], axis=1)) + bias_rows(tiles), kv

    def lane_fold(x, op):
        return functools.reduce(op, [x[:, k * V7X_LANES:(k + 1) * V7X_LANES] for k in range(ch // V7X_LANES)])

    m_lanes = lax.fori_loop(0, n_chunks, lambda c, m: jnp.maximum(m, lane_fold(scores(c)[0], jnp.maximum)),
                            jnp.full((rows, V7X_LANES), NEG_INF, F32))
    m_rows = _row_max(m_lanes)

    l_ref[...] = jnp.zeros_like(l_ref)
    acc_ref[...] = jnp.zeros_like(acc_ref)

    def sum_body(c, carry):
        s, kv = scores(c)
        p = jnp.exp(s - m_rows)
        l_ref[...] = l_ref[...] + lane_fold(p, lambda a, b: a + b)
        acc_ref[...] = acc_ref[...] + _dot(p.astype(BF16), kv)
        return carry

    lax.fori_loop(0, n_chunks, sum_body, 0)
    o_sel = heads(acc_ref[...] / _row_sum(l_ref[...]))

    starts = [pl.multiple_of(jnp.maximum(i - (n_win_tiles - 1) + j, 0) * t, t) for j in range(n_win_tiles)]
    kvw = jnp.concatenate([win_ref[0, pl.ds(st, t), :] for st in starts], axis=0)
    s = _dot_nt(q_all, kvw) * SCALE + bias_rows([n_win_tiles - j for j in range(n_win_tiles)])
    p, _, den = _masked_softmax(s, mask_w)
    o_win = heads(_dot(p.astype(BF16), kvw) / den)
    slabs = [jnp.zeros((t, V7X_LANES), F32) for _ in range(B_GROUP_Q // V7X_LANES)]
    for r in range(B_REP):
        o_v = gate[:, B_REP + r:B_REP + r + 1] * o_sel[r] + gate[:, 2 * B_REP + r:2 * B_REP + r + 1] * o_win[r]
        o = gate[:, r:r + 1] * _pick_head(o_cmp[r], 0) + _align_head(_pick_head(o_v, 1), 1, 0)
        slabs[r // 2] = slabs[r // 2] + _align_head(o, 0, r % 2)
    for sidx, slab in enumerate(slabs):
        o_ref[0, :, sidx * V7X_LANES:(sidx + 1) * V7X_LANES] = slab


def _b_attn(q, gate, kvt, selg, wing, table, batch, seq):
    t = B_TILE
    n_chunks = seq // B_CMP_STRIDE
    n_sel = seq // B_SEL_BLOCK
    assert n_sel <= B_SEL_PAD and n_sel % 8 == 0 and seq % B_SEL_CHUNK == 0
    tcmp = _cmp_bias_tiles(table, seq, n_chunks)
    ttab = _toeplitz_tiles(table)
    ovt = jnp.asarray(_overlap_t(n_sel, n_chunks, n_sel, n_chunks - 1), BF16)
    whole = lambda a: pl.BlockSpec(a.shape, lambda b, g, i: (0,) * a.ndim)
    seq_block = pl.BlockSpec((1, seq, V7X_LANES), lambda b, g, i: (b, 0, g))
    o = pl.pallas_call(
        functools.partial(_b_attn_kernel, n_sel=n_sel), grid=(batch, B_KV_HEADS, seq // t),
        in_specs=[pl.BlockSpec((1, t, B_GROUP_Q), lambda b, g, i: (b, i, g)),
                  pl.BlockSpec((1, t, V7X_LANES), lambda b, g, i: (b, i, g)),
                  pl.BlockSpec((1, 2, 1, V7X_LANES, n_chunks), lambda b, g, i: (b, 0, g, 0, 0)),
                  pl.BlockSpec((1, B_REP, t, n_chunks), lambda b, g, i: (i, g, 0, 0)),
                  seq_block, seq_block, whole(ttab), whole(ovt)],
        out_specs=pl.BlockSpec((1, t, B_GROUP_Q), lambda b, g, i: (b, i, g)),
        out_shape=jax.ShapeDtypeStruct((batch, seq, B_GQ_COLS), F32),
        scratch_shapes=[pltpu.VMEM((B_REP * t, V7X_LANES), F32)] * 2 + [pltpu.VMEM((n_sel, t), F32)],
        compiler_params=_params("arbitrary", "arbitrary", "arbitrary"), name="sparse_prompt_attention")(
            q.reshape(batch, seq, B_GQ_COLS), gate.reshape(batch, seq, B_GGATE_COLS), kvt, tcmp,
            selg.reshape(batch, seq, B_GKV_COLS), wing.reshape(batch, seq, B_GKV_COLS), ttab, ovt)
    return o.reshape(batch * seq, B_GQ_COLS)


B_STEP_SEL_PAD = 256
B_STEP_ROWS = 8


def _b_step_cmp_kernel(q_ref, kvt_ref, bias_ref, ov_ref, oc_ref, ids_ref, *, n_cmp, n_sel, cur):
    pad = B_STEP_SEL_PAD
    n_cmp_pad = kvt_ref.shape[3]
    q = q_ref[0].astype(F32)
    mask_c = lax.broadcasted_iota(I32, (1, n_cmp_pad), 1) < n_cmp
    blk = lax.broadcasted_iota(I32, (1, pad), 1)
    forced = (blk == 0) | (blk == cur) | (blk == cur - 1)
    b_idx = lax.broadcasted_iota(I32, (pad, pad), 0)
    j_idx = lax.broadcasted_iota(I32, (pad, pad), 1)
    slot = lax.broadcasted_iota(I32, (pad, V7X_LANES), 1).astype(F32)
    b_val = lax.broadcasted_iota(I32, (pad, V7X_LANES), 0).astype(F32)
    for g in range(B_KV_HEADS):
        kv_rows = slice(g * HEAD_DIM, (g + 1) * HEAD_DIM)
        kct = kvt_ref[0, 0, kv_rows, :].astype(F32)
        vct = kvt_ref[0, 1, kv_rows, :].astype(F32)
        p_sum = jnp.zeros((1, n_cmp_pad), F32)
        for r in range(B_REP):
            h = g * B_REP + r
            rows = slice(h * HEAD_DIM, (h + 1) * HEAD_DIM)
            s = _col_dot(kct, q[rows]) * SCALE + bias_ref[h:h + 1, :]
            p, _, den = _masked_softmax(s, mask_c)
            p = p / den
            p_sum = p_sum + p
            oc_ref[0, rows, :] = _row_sum(vct * p)
        hi, mid, lo3 = _split3(jnp.broadcast_to(p_sum, (B_STEP_ROWS, n_cmp_pad)))
        imp = (_dot(hi, ov_ref[...]) + _dot(mid, ov_ref[...]) + _dot(lo3, ov_ref[...]))[0:1]
        imp = jnp.where(forced, B_FORCE, jnp.where(blk <= cur, imp, NEG_INF))
        imp = jnp.where(blk < n_sel, imp, B_PAD_SCORE)
        other = jnp.broadcast_to(imp, (pad, pad))
        mine = jnp.transpose(other)
        ahead = (other > mine) | ((other == mine) & (j_idx < b_idx))
        rank = jnp.sum(jnp.where(ahead, 1.0, 0.0), axis=1, keepdims=True)
        ids = jnp.sum(jnp.where(rank == slot, b_val, 0.0), axis=0, keepdims=True)
        ids_ref[0, g:g + 1, :] = ids.astype(I32)


def _b_step_sel_kernel(ids_ref, pt_ref, q_ref, gate_ref, oc_ref, seln_ref, winn_ref, wbuf_ref,
                       s0_ref, s1_ref, s2_ref, s3_ref, bblk_ref, bwin_ref, o_ref, m_ref, l_ref, acc_ref,
                       *, n_past_blocks, blocks_per_page):
    n = pl.program_id(0)
    kb = pl.program_id(1)
    page = s0_ref.shape[4]
    q = q_ref[0].astype(F32)
    seln = seln_ref[0]
    sel_blocks = (s0_ref, s1_ref, s2_ref, s3_ref)
    self_bias = bblk_ref[n_past_blocks]
    lane = lax.broadcasted_iota(I32, (1, page), 1)

    def layout(h):
        g = h // B_REP
        return (slice(h * HEAD_DIM, (h + 1) * HEAD_DIM), slice(g * HEAD_DIM, (g + 1) * HEAD_DIM),
                slice(B_KV_COLS + g * HEAD_DIM, B_KV_COLS + (g + 1) * HEAD_DIM))

    def self_score(h, new):
        rows, k_rows, _ = layout(h)
        return _col_dot(new[k_rows], q[rows]) * SCALE + self_bias[h:h + 1, 0:1]

    @pl.when(kb == 0)
    def _():
        for h in range(B_HEADS):
            _, _, v_rows = layout(h)
            m_ref[h] = jnp.broadcast_to(self_score(h, seln), (1, page))
            l_ref[h] = jnp.where(lane == 0, 1.0, 0.0)
            acc_ref[h] = jnp.where(lane == 0, jnp.broadcast_to(seln[v_rows], (HEAD_DIM, page)), 0.0)

    for g in range(B_KV_HEADS):
        blkid = ids_ref[(n * B_KV_HEADS + g) * V7X_LANES + kb]
        mask = (blkid < n_past_blocks) & (lane // B_SEL_BLOCK == blkid % blocks_per_page)
        bias_blk = bblk_ref[jnp.minimum(blkid, n_past_blocks)]
        kt = sel_blocks[g][0, 0, 0]
        vt = sel_blocks[g][0, 1, 0]
        for r in range(B_REP):
            h = g * B_REP + r
            rows, _, _ = layout(h)
            s = jnp.where(mask, _col_dot(kt, q[rows]) * SCALE + bias_blk[h:h + 1, :], NEG_INF)
            m_old = m_ref[h]
            m_new = jnp.maximum(m_old, _row_max(s))
            p = jnp.where(mask, jnp.exp(s - m_new), 0.0)
            alpha = jnp.exp(m_old - m_new)
            l_ref[h] = alpha * l_ref[h] + p
            acc_ref[h] = alpha * acc_ref[h] + vt * p
            m_ref[h] = m_new

    @pl.when(kb == pl.num_programs(1) - 1)
    def _():
        gate = gate_ref[0]
        winn = winn_ref[0]
        wb = wbuf_ref.shape[4]
        wlane = lax.broadcasted_iota(I32, (1, wb), 1)
        wmask = (wb - wlane >= 0) & (wb - wlane < B_WINDOW)
        for h in range(B_HEADS):
            g = h // B_REP
            rows, _, v_rows = layout(h)
            s = jnp.where(wmask, _col_dot(wbuf_ref[0, 0, g], q[rows]) * SCALE + bwin_ref[h:h + 1, :], NEG_INF)
            s_self = self_score(h, winn)
            m = jnp.maximum(_row_max(s), s_self)
            pb = jnp.where(wmask, jnp.exp(s - m), 0.0)
            ps = jnp.exp(s_self - m)
            o_win = (_row_sum(wbuf_ref[0, 1, g] * pb) + ps * winn[v_rows]) / (_row_sum(pb) + ps)
            o_sel = _row_sum(acc_ref[h]) / _row_sum(l_ref[h])
            o_ref[0, rows, :] = (gate[:, h:h + 1] * oc_ref[0, rows, :] + gate[:, B_HEADS + h:B_HEADS + h + 1] * o_sel
                                 + gate[:, 2 * B_HEADS + h:2 * B_HEADS + h + 1] * o_win)


def _b_step_attn(q, gate, kvt, sel_new, win_new, cache_sel, cache_win, page_table, table):
    n, n_pages = page_table.shape
    page = cache_sel.shape[1]
    past = n_pages * page
    wb = cache_win.shape[1]
    assert wb == B_WINDOW and past % B_SEL_BLOCK == 0 and page % B_SEL_BLOCK == 0
    n_cmp_pad = past // B_CMP_STRIDE
    n_cmp = (past + 1) // B_CMP_STRIDE - (B_CMP_BLOCK // B_CMP_STRIDE) + 1
    n_sel = -(-(past + 1) // B_SEL_BLOCK)
    n_past_blocks = past // B_SEL_BLOCK
    blocks_per_page = page // B_SEL_BLOCK
    assert n_sel <= B_STEP_SEL_PAD and n_cmp <= n_cmp_pad

    bias_c = _bias_of_dist(table, np.maximum(past - (B_CMP_STRIDE * np.arange(n_cmp_pad) + B_CMP_BLOCK - 1), 0))
    ov = jnp.asarray(_overlap_t(B_STEP_SEL_PAD, n_cmp_pad, n_sel, n_cmp).T, BF16)
    col = lambda c: pl.BlockSpec((1, c, 1), lambda b: (b, 0, 0))
    q_col = q.reshape(n, B_Q_COLS, 1)
    oc, ids = pl.pallas_call(
        functools.partial(_b_step_cmp_kernel, n_cmp=n_cmp, n_sel=n_sel, cur=past // B_SEL_BLOCK), grid=(n,),
        in_specs=[col(B_Q_COLS), pl.BlockSpec((1, 2, B_KV_COLS, n_cmp_pad), lambda b: (b, 0, 0, 0)),
                  pl.BlockSpec(bias_c.shape, lambda b: (0, 0)), pl.BlockSpec(ov.shape, lambda b: (0, 0))],
        out_specs=[col(B_Q_COLS), pl.BlockSpec((1, B_KV_HEADS, V7X_LANES), lambda b: (b, 0, 0))],
        out_shape=[jax.ShapeDtypeStruct((n, B_Q_COLS, 1), F32),
                   jax.ShapeDtypeStruct((n, B_KV_HEADS, V7X_LANES), I32)],
        compiler_params=_params("arbitrary"), name="sparse_step_compressed")(q_col, kvt, bias_c, ov)

    blk_pos = B_SEL_BLOCK * np.arange(n_past_blocks + 1)[:, None] + np.arange(B_SEL_BLOCK)[None, :]
    bblk = _bias_of_dist(table, np.tile(np.maximum(past - blk_pos, 0), (1, blocks_per_page)), head_axis=1)
    bwin = _bias_of_dist(table, wb - np.arange(wb))
    scol = lambda c: pl.BlockSpec((1, c, 1), lambda b, kb, ids, pt: (b, 0, 0))

    def sel_spec(g):
        def index(b, kb, ids, pt):
            blk = jnp.minimum(ids[(b * B_KV_HEADS + g) * V7X_LANES + kb], n_past_blocks - 1)
            return (pt[b, blk // blocks_per_page], 0, g, 0, 0)
        return pl.BlockSpec((1, 2, 1, HEAD_DIM, page), index)

    sel_view = cache_sel.transpose(0, 2, 3, 4, 1)
    win_view = cache_win.transpose(0, 2, 3, 4, 1)
    o = pl.pallas_call(
        functools.partial(_b_step_sel_kernel, n_past_blocks=n_past_blocks, blocks_per_page=blocks_per_page),
        grid_spec=pltpu.PrefetchScalarGridSpec(
            num_scalar_prefetch=2, grid=(n, B_TOPN),
            in_specs=[scol(B_Q_COLS), pl.BlockSpec((1, 1, B_TAIL), lambda b, kb, ids, pt: (b, 0, 0)),
                      scol(B_Q_COLS), scol(B_SLAB), scol(B_SLAB),
                      pl.BlockSpec((1,) + win_view.shape[1:], lambda b, kb, ids, pt: (b, 0, 0, 0, 0))]
            + [sel_spec(g) for g in range(B_KV_HEADS)]
            + [pl.BlockSpec(bblk.shape, lambda b, kb, ids, pt: (0, 0, 0)),
               pl.BlockSpec(bwin.shape, lambda b, kb, ids, pt: (0, 0))],
            out_specs=scol(B_Q_COLS),
            scratch_shapes=[pltpu.VMEM((B_HEADS, 1, page), F32), pltpu.VMEM((B_HEADS, 1, page), F32),
                            pltpu.VMEM((B_HEADS, HEAD_DIM, page), F32)]),
        out_shape=jax.ShapeDtypeStruct((n, B_Q_COLS, 1), F32),
        compiler_params=_params("arbitrary", "arbitrary"), name="sparse_step_selected")(
            ids.reshape(-1), page_table, q_col, gate.reshape(n, 1, B_TAIL), oc,
            sel_new.reshape(n, B_SLAB, 1), win_new.reshape(n, B_SLAB, 1), win_view,
            sel_view, sel_view, sel_view, sel_view, bblk, bwin)
    return o.reshape(n, B_Q_COLS)


def _mixer_b(xp, xs, g, w_in, gate_bias, phi_pos, phi_w1, phi_w2, w_out, table, caches, page_table, batch, seq):
    cache_cmp, cache_sel, cache_win = caches
    five = lambda a, rows: a.reshape(-1, rows, 2, B_KV_HEADS, HEAD_DIM)

    w, w_tail, bias = _b_grouped_weights(w_in, gate_bias)
    q, cmp, sel, win, selg, wing, gate = _b_proj_grouped(xp, g, w, w_tail, bias)
    kvt = _b_compress_prompt(cmp, _compress_weights(phi_pos, phi_w1, phi_w2, True), batch, seq)
    o = _b_attn(q, gate, kvt, selg, wing, table, batch, seq)
    w_out_grouped = jnp.pad(w_out.reshape(B_KV_HEADS, B_REP * HEAD_DIM, D_MODEL),
                            ((0, 0), (0, B_GROUP_Q - B_REP * HEAD_DIM), (0, 0))).reshape(B_GQ_COLS, D_MODEL)
    yp = _out_proj(xp, o, w_out_grouped.astype(BF16))
    keep = min(B_WINDOW, seq)
    state = [five(cmp, seq), five(sel, seq), five(win, seq)[:, seq - keep:]]

    n_main = B_Q_COLS + 3 * B_SLAB
    w = w_in[:, :n_main].astype(BF16)
    w_tail = jnp.pad(w_in[:, n_main:], ((0, 0), (0, B_TAIL - B_GATE_COLS))).astype(BF16)
    bias = jnp.pad(gate_bias, (0, B_TAIL - B_GATE_COLS)).reshape(1, B_TAIL)
    sq, scmp, ssel, swin, sgate = _b_proj(xs, g, w, w_tail, bias)
    skvt = _b_compress_step(cache_cmp, page_table, _compress_weights(phi_pos, phi_w1, phi_w2, False))
    so = _b_step_attn(sq, sgate, skvt, ssel, swin, cache_sel, cache_win, page_table, table)
    ys = _out_proj(xs, so, w_out.astype(BF16))
    wb = cache_win.shape[1]
    win_all = jnp.concatenate([cache_win, five(swin, 1)], axis=1)
    state += [five(scmp, 1), five(ssel, 1), win_all[:, wb + 1 - min(B_WINDOW, wb + 1):]]
    return yp, ys, state


def kernel(x_prompt, x_sample, cache_l0_w128, cache_l0_w512, cache_l0_w2048, cache_l1_cmp, cache_l1_sel, cache_l1_win, cache_l2_k, cache_l2_v, cache_l2_logf, cache_l3_w128, cache_l3_w512, cache_l3_w2048, page_table, norm_g, ffn_w_gate, ffn_w_up, ffn_w_down, final_norm_g, rel_bias_table, a_w_in, a_w_out, b_w_in, b_gate_bias, b_phi_pos, b_phi_w1, b_phi_w2, b_w_out, c_w_in, c_forget_bias, c_w_out):
    batch, seq, _ = x_prompt.shape
    depth = norm_g.shape[0]
    layer_caches = ((cache_l0_w128, cache_l0_w512, cache_l0_w2048), (cache_l1_cmp, cache_l1_sel, cache_l1_win),
                    (cache_l2_k, cache_l2_v, cache_l2_logf), (cache_l3_w128, cache_l3_w512, cache_l3_w2048))
    xp = x_prompt.reshape(batch * seq, D_MODEL)
    xs = x_sample.reshape(-1, D_MODEL)
    table = rel_bias_table
    new_state = []
    for i in range(depth):
        kind, j = i % N_MIXERS, i // N_MIXERS
        last = i == depth - 1
        f1 = (norm_g[i, 0], ffn_w_gate[i, 0].astype(BF16), ffn_w_up[i, 0].astype(BF16), ffn_w_down[i, 0].astype(BF16))
        f2 = (norm_g[i, 2], ffn_w_gate[i, 1].astype(BF16), ffn_w_up[i, 1].astype(BF16), ffn_w_down[i, 1].astype(BF16))
        xp, xs = _ffn(xp, *f1), _ffn(xs, *f1)
        if kind == 0:
            xp, xs, state = _mixer_a(xp, xs, norm_g[i, 1], a_w_in[j], a_w_out[j], table, layer_caches[i], batch, seq)
        elif kind == 1:
            xp, xs, state = _mixer_b(xp, xs, norm_g[i, 1], b_w_in[j], b_gate_bias[j], b_phi_pos[j], b_phi_w1[j],
                                     b_phi_w2[j], b_w_out[j], table, layer_caches[i], page_table, batch, seq)
        else:
            xp, xs, state = _mixer_c(xp, xs, norm_g[i, 1], c_w_in[j], c_forget_bias[j], c_w_out[j],
                                     layer_caches[i], page_table, batch, seq)
        new_state.extend(state)
        final_g = final_norm_g if last else None
        xp, xs = _ffn(xp, *f2, final_g), _ffn(xs, *f2, final_g)
    return (xp.reshape(batch, seq, D_MODEL), xs.reshape(-1, 1, D_MODEL), *new_state)
```

```python
import functools
import math

import numpy as np
import jax
import jax.numpy as jnp
from jax import lax
from jax.experimental import pallas as pl
from jax.experimental.pallas import tpu as pltpu

F32 = jnp.float32
BF16 = jnp.bfloat16
I32 = jnp.int32

D_MODEL = 1024
HEAD_DIM = 64
D_FF = 2816
RMS_EPS = 1e-6
NEG_INF = -1e30
SCALE = HEAD_DIM ** -0.5
N_MIXERS = 3

NUM_BUCKETS = 32
REL_MAX_DIST = 2048

A_GROUPS = ((128, 1), (512, 4), (2048, 16))
A_GROUP_HEADS = 4
A_GROUP_COLS = A_GROUP_HEADS * HEAD_DIM
A_HEADS = A_GROUP_HEADS * len(A_GROUPS)
A_BAND = 128

B_HEADS = 12
B_KV_HEADS = 4
B_REP = B_HEADS // B_KV_HEADS
B_CMP_BLOCK = 32
B_CMP_STRIDE = 16
B_SEL_BLOCK = 64
B_TOPN = 16
B_WINDOW = 512
B_PHI_HIDDEN = 128
B_FORCE = 1e4
B_Q_COLS = B_HEADS * HEAD_DIM
B_KV_COLS = B_KV_HEADS * HEAD_DIM
B_GATE_COLS = 3 * B_HEADS

C_HEADS = 16
C_COLS = C_HEADS * HEAD_DIM

V7X_LANES = 128
V7X_VMEM_LIMIT_BYTES = 56 * 1024 * 1024
ROW_TILE = 512
FF_CHUNK = 256
HALF = HEAD_DIM


def _params(*sem):
    return pltpu.CompilerParams(dimension_semantics=sem, vmem_limit_bytes=V7X_VMEM_LIMIT_BYTES)


def _dot(a, b):
    return jnp.dot(a, b, preferred_element_type=F32)


def _dot_nt(a, b):
    return lax.dot_general(a, b, (((1,), (1,)), ((), ())), preferred_element_type=F32)


def _split3(x):
    hi = x.astype(BF16)
    r1 = x - hi.astype(F32)
    mid = r1.astype(BF16)
    lo = (r1 - mid.astype(F32)).astype(BF16)
    return hi, mid, lo


def _dot3(a_bf16_exact, x):
    hi, mid, lo = _split3(x)
    return _dot(a_bf16_exact, hi) + _dot(a_bf16_exact, mid) + _dot(a_bf16_exact, lo)


def _rms(x, g):
    return x * lax.rsqrt(jnp.mean(x * x, axis=-1, keepdims=True) + RMS_EPS) * g


def _lane_lo(shape):
    return (lax.broadcasted_iota(I32, shape, len(shape) - 1) % V7X_LANES) < HALF


def _pick_head(slab, half):
    lo = _lane_lo(slab.shape)
    return jnp.where(lo if half == 0 else jnp.logical_not(lo), slab, jnp.zeros_like(slab))


def _align_head(slab, src_half, dst_half):
    if src_half == dst_half:
        return slab
    return pltpu.roll(slab, HALF, 1)


def _masked_softmax(s, mask):
    s = jnp.where(mask, s, NEG_INF)
    m = jnp.max(s, axis=-1, keepdims=True)
    p = jnp.where(mask, jnp.exp(s - m), 0.0)
    den = jnp.sum(p, axis=-1, keepdims=True)
    return p, m, jnp.where(den > 0, den, 1.0)


def _rel_bucket_np(dist):
    exact = NUM_BUCKETS // 2
    d = np.maximum(dist, 0)
    logd = (np.log(np.maximum(d, 1).astype(np.float32) / np.float32(exact))
            / np.float32(math.log(REL_MAX_DIST / exact))).astype(np.float32)
    far = np.minimum(exact + (logd * np.float32(NUM_BUCKETS - exact)).astype(np.int32), NUM_BUCKETS - 1)
    return np.where(d < exact, d, far).astype(np.int32)


def _bucket_starts():
    b = _rel_bucket_np(np.arange(2 * REL_MAX_DIST + 1))
    assert np.all(np.diff(b) >= 0) and b[-1] == NUM_BUCKETS - 1
    return [int(np.argmax(b >= k)) for k in range(NUM_BUCKETS)]


_BUCKET_STARTS = _bucket_starts()


def _bias_of_dist(table, dist, head_axis=0):
    d = jnp.expand_dims(jnp.asarray(dist, I32), head_axis)
    t = table.astype(F32)
    shape = [1] * d.ndim
    shape[head_axis] = t.shape[1]
    full = tuple(t.shape[1] if a == head_axis else n for a, n in enumerate(d.shape))
    out = jnp.broadcast_to(t[0].reshape(shape), full)
    for k in range(1, NUM_BUCKETS):
        out = jnp.where(d >= _BUCKET_STARTS[k], t[k].reshape(shape), out)
    return out


def _col_dot(kt, q_col):
    return jnp.sum(kt * q_col, axis=0, keepdims=True)


def _row_max(x):
    return jnp.max(x, axis=-1, keepdims=True)


def _row_sum(x):
    return jnp.sum(x, axis=-1, keepdims=True)


def _ffn_kernel(x_ref, g_ref, wg_ref, wu_ref, wd_ref, *rest, final):
    x = x_ref[...]
    h = _rms(x, g_ref[...]).astype(BF16)
    acc = jnp.zeros_like(x)
    for c in range(D_FF // FF_CHUNK):
        sl = slice(c * FF_CHUNK, (c + 1) * FF_CHUNK)
        gate = _dot(h, wg_ref[:, sl])
        up = _dot(h, wu_ref[:, sl])
        act = (gate * jax.nn.sigmoid(gate) * up).astype(BF16)
        acc = acc + _dot(act, wd_ref[sl, :])
    y = x + 0.5 * acc
    if final:
        gf_ref, o_ref = rest
        o_ref[...] = _rms(y, gf_ref[...])
    else:
        (o_ref,) = rest
        o_ref[...] = y


def _row_tile(rows):
    return ROW_TILE if rows % ROW_TILE == 0 else rows


def _ffn(x, g, wg, wu, wd, final_g=None):
    rows = x.shape[0]
    tm = _row_tile(rows)
    row = pl.BlockSpec((tm, D_MODEL), lambda i: (i, 0))
    vec = pl.BlockSpec((1, D_MODEL), lambda i: (0, 0))
    whole = lambda a: pl.BlockSpec(a.shape, lambda i: (0,) * a.ndim)
    args = [x, g.reshape(1, D_MODEL), wg, wu, wd]
    specs = [row, vec, whole(wg), whole(wu), whole(wd)]
    if final_g is not None:
        args.append(final_g.reshape(1, D_MODEL))
        specs.append(vec)
    return pl.pallas_call(
        functools.partial(_ffn_kernel, final=final_g is not None),
        grid=(rows // tm,), in_specs=specs, out_specs=row,
        out_shape=jax.ShapeDtypeStruct(x.shape, F32),
        compiler_params=_params("arbitrary"), name="macaron_swiglu")(*args)


def _out_proj_kernel(x_ref, o_ref, w_ref, y_ref):
    y_ref[...] = x_ref[...] + _dot(o_ref[...].astype(BF16), w_ref[...])


def _out_proj(x, o, w):
    rows = x.shape[0]
    tm = _row_tile(rows)
    cols = o.shape[1]
    return pl.pallas_call(
        _out_proj_kernel, grid=(rows // tm,),
        in_specs=[pl.BlockSpec((tm, D_MODEL), lambda i: (i, 0)),
                  pl.BlockSpec((tm, cols), lambda i: (i, 0)),
                  pl.BlockSpec(w.shape, lambda i: (0, 0))],
        out_specs=pl.BlockSpec((tm, D_MODEL), lambda i: (i, 0)),
        out_shape=jax.ShapeDtypeStruct(x.shape, F32),
        compiler_params=_params("arbitrary"), name="mixer_out_proj")(x, o, w)


def _a_proj_kernel(x_ref, g_ref, w_ref, q0_ref, q1_ref, q2_ref, kv0_ref, kv1_ref, kv2_ref):
    h = _rms(x_ref[...], g_ref[...]).astype(BF16)
    nq = A_HEADS * HEAD_DIM
    for gi, (q_ref, kv_ref) in enumerate(((q0_ref, kv0_ref), (q1_ref, kv1_ref), (q2_ref, kv2_ref))):
        c0 = gi * A_GROUP_COLS
        q_ref[...] = _dot(h, w_ref[:, c0:c0 + A_GROUP_COLS]).astype(BF16)
        kv_ref[:, :A_GROUP_COLS] = _dot(h, w_ref[:, nq + c0:nq + c0 + A_GROUP_COLS])
        kv_ref[:, A_GROUP_COLS:] = _dot(h, w_ref[:, 2 * nq + c0:2 * nq + c0 + A_GROUP_COLS])


def _a_proj(x, g, w):
    rows = x.shape[0]
    tm = _row_tile(rows)
    qspec = pl.BlockSpec((tm, A_GROUP_COLS), lambda i: (i, 0))
    kvspec = pl.BlockSpec((tm, 2 * A_GROUP_COLS), lambda i: (i, 0))
    return pl.pallas_call(
        _a_proj_kernel, grid=(rows // tm,),
        in_specs=[pl.BlockSpec((tm, D_MODEL), lambda i: (i, 0)),
                  pl.BlockSpec((1, D_MODEL), lambda i: (0, 0)),
                  pl.BlockSpec(w.shape, lambda i: (0, 0))],
        out_specs=[qspec] * 3 + [kvspec] * 3,
        out_shape=[jax.ShapeDtypeStruct((rows, A_GROUP_COLS), BF16)] * 3
        + [jax.ShapeDtypeStruct((rows, 2 * A_GROUP_COLS), F32)] * 3,
        compiler_params=_params("arbitrary"), name="dilated_in_proj")(x, g.reshape(1, D_MODEL), w)


def _a_attn_kernel(q_ref, kvc_ref, kvp_ref, bias_ref, o_ref, l_ref):
    blk = pl.program_id(2)
    band = A_BAND
    rows = A_GROUP_HEADS * band
    q = q_ref[0]
    k = jnp.concatenate([kvp_ref[0, :, :A_GROUP_COLS], kvc_ref[0, :, :A_GROUP_COLS]], axis=0).astype(BF16)
    v = jnp.concatenate([kvp_ref[0, :, A_GROUP_COLS:], kvc_ref[0, :, A_GROUP_COLS:]], axis=0).astype(BF16)
    head_of_lane = lax.broadcasted_iota(I32, (band, A_GROUP_COLS), 1) // HEAD_DIM
    q_all = jnp.concatenate([jnp.where(head_of_lane == h, q, jnp.zeros_like(q)) for h in range(A_GROUP_HEADS)],
                            axis=0)
    qi = lax.broadcasted_iota(I32, (rows, 2 * band), 0) % band
    kj = lax.broadcasted_iota(I32, (rows, 2 * band), 1)
    off = qi + band - kj
    mask = (off >= 0) & (off <= band) & ((kj >= band) | (blk > 0))
    s = _dot_nt(q_all, k) * SCALE + bias_ref[...].reshape(rows, 2 * band)
    p, m, den = _masked_softmax(s, mask)
    out = _dot(p.astype(BF16), v) / den
    lse = jnp.broadcast_to(m + jnp.log(den), (rows, A_GROUP_COLS))
    pick = lambda x: functools.reduce(
        lambda a, b: a + b, [jnp.where(head_of_lane == h, x[h * band:(h + 1) * band], 0.0) for h in range(A_GROUP_HEADS)])
    o_ref[0] = pick(out)
    l_ref[0] = pick(lse)


def _a_attn_group(q, kv, bias, batch, seq, dil):
    sub = seq // dil
    nb = sub // A_BAND
    qv = q.reshape(batch, sub, dil * A_GROUP_COLS)
    kvv = kv.reshape(batch, sub, dil * 2 * A_GROUP_COLS)
    qspec = pl.BlockSpec((1, A_BAND, A_GROUP_COLS), lambda b, r, i: (b, i, r))
    o, lse = pl.pallas_call(
        _a_attn_kernel, grid=(batch, dil, nb),
        in_specs=[qspec,
                  pl.BlockSpec((1, A_BAND, 2 * A_GROUP_COLS), lambda b, r, i: (b, i, r)),
                  pl.BlockSpec((1, A_BAND, 2 * A_GROUP_COLS), lambda b, r, i: (b, jnp.maximum(i - 1, 0), r)),
                  pl.BlockSpec(bias.shape, lambda b, r, i: (0, 0, 0))],
        out_specs=[qspec, qspec],
        out_shape=[jax.ShapeDtypeStruct(qv.shape, F32)] * 2,
        compiler_params=_params("arbitrary", "arbitrary", "arbitrary"), name="dilated_band_attention")(
            qv, kvv, kvv, bias)
    return o.reshape(batch * seq, A_GROUP_COLS), lse.reshape(batch * seq, A_GROUP_COLS)


def _a_band_bias(table, g, dil):
    off = np.arange(A_BAND)[:, None] + A_BAND - np.arange(2 * A_BAND)[None, :]
    dist = np.clip(off, 0, A_BAND) * dil
    return _bias_of_dist(table[:, g * A_GROUP_HEADS:(g + 1) * A_GROUP_HEADS], dist)


def _a_out_kernel(x_ref, o0_ref, o1_ref, o2_ref, l0_ref, l1_ref, l2_ref, w_ref, y_ref):
    ls = [l0_ref[...], l1_ref[...], l2_ref[...]]
    m = jnp.maximum(jnp.maximum(ls[0], ls[1]), ls[2])
    es = [jnp.exp(l - m) for l in ls]
    den = es[0] + es[1] + es[2]
    y = x_ref[...]
    for gi, o_ref in enumerate((o0_ref, o1_ref, o2_ref)):
        og = (o_ref[...] * (es[gi] / den)).astype(BF16)
        y = y + _dot(og, w_ref[gi * A_GROUP_COLS:(gi + 1) * A_GROUP_COLS, :])
    y_ref[...] = y


def _a_out(x, outs, lses, w):
    rows = x.shape[0]
    tm = _row_tile(rows)
    gspec = pl.BlockSpec((tm, A_GROUP_COLS), lambda i: (i, 0))
    xspec = pl.BlockSpec((tm, D_MODEL), lambda i: (i, 0))
    return pl.pallas_call(
        _a_out_kernel, grid=(rows // tm,),
        in_specs=[xspec] + [gspec] * 6 + [pl.BlockSpec(w.shape, lambda i: (0, 0))],
        out_specs=xspec, out_shape=jax.ShapeDtypeStruct(x.shape, F32),
        compiler_params=_params("arbitrary"), name="dilated_combine_out_proj")(x, *outs, *lses, w)


def _a_step_kernel(q0_ref, q1_ref, q2_ref, n0_ref, n1_ref, n2_ref, b0_ref, b1_ref, b2_ref,
                   bias0_ref, bias1_ref, bias2_ref, self_ref, o_ref):
    groups = ((q0_ref, n0_ref, b0_ref, bias0_ref), (q1_ref, n1_ref, b1_ref, bias1_ref),
              (q2_ref, n2_ref, b2_ref, bias2_ref))
    outs, lses = [], []
    for gi, (q_ref, n_ref, b_ref, bias_ref) in enumerate(groups):
        dil = A_GROUPS[gi][1]
        win = b_ref.shape[4]
        mask = lax.broadcasted_iota(I32, (1, win), 1) % dil == 0
        q = q_ref[0].astype(F32)
        new = n_ref[0]
        for j in range(A_GROUP_HEADS):
            head = gi * A_GROUP_HEADS + j
            rows = slice(j * HEAD_DIM, (j + 1) * HEAD_DIM)
            v_rows = slice(A_GROUP_COLS + j * HEAD_DIM, A_GROUP_COLS + (j + 1) * HEAD_DIM)
            s = jnp.where(mask, _col_dot(b_ref[0, 0, j], q[rows]) * SCALE + bias_ref[j:j + 1, :], NEG_INF)
            s_self = _col_dot(new[rows], q[rows]) * SCALE + self_ref[head:head + 1, 0:1]
            m = jnp.maximum(_row_max(s), s_self)
            pb = jnp.where(mask, jnp.exp(s - m), 0.0)
            ps = jnp.exp(s_self - m)
            den = _row_sum(pb) + ps
            outs.append((_row_sum(b_ref[0, 1, j] * pb) + ps * new[v_rows]) / den)
            lses.append(m + jnp.log(den))
    n_groups = len(groups)
    for j in range(A_GROUP_HEADS):
        ls = [lses[gi * A_GROUP_HEADS + j] for gi in range(n_groups)]
        m = functools.reduce(jnp.maximum, ls)
        es = [jnp.exp(l - m) for l in ls]
        den = functools.reduce(lambda a, b: a + b, es)
        for gi in range(n_groups):
            head = gi * A_GROUP_HEADS + j
            o_ref[0, head * HEAD_DIM:(head + 1) * HEAD_DIM, :] = outs[head] * (es[gi] / den)


def _a_step_attn(qs, news, bufs, table):
    n = qs[0].shape[0]
    cols = lambda a: a.reshape(n, a.shape[1], 1)
    views = [b.transpose(0, 2, 3, 4, 1) for b in bufs]
    biases = []
    for g, (win, _) in enumerate(A_GROUPS):
        assert bufs[g].shape[1] == win, "the step kernel reads a full window buffer"
        biases.append(_bias_of_dist(table[:, g * A_GROUP_HEADS:(g + 1) * A_GROUP_HEADS], win - np.arange(win)))
    self_bias = jnp.broadcast_to(_bias_of_dist(table, np.zeros((1,), np.int64)), (A_HEADS, V7X_LANES))
    col_spec = lambda c: pl.BlockSpec((1, c, 1), lambda b: (b, 0, 0))
    whole = lambda a: pl.BlockSpec(a.shape, lambda b: (0,) * a.ndim)
    o = pl.pallas_call(
        _a_step_kernel, grid=(n,),
        in_specs=[col_spec(A_GROUP_COLS)] * 3 + [col_spec(2 * A_GROUP_COLS)] * 3
        + [pl.BlockSpec((1,) + v.shape[1:], lambda b: (b, 0, 0, 0, 0)) for v in views]
        + [whole(b) for b in biases] + [whole(self_bias)],
        out_specs=col_spec(A_HEADS * HEAD_DIM),
        out_shape=jax.ShapeDtypeStruct((n, A_HEADS * HEAD_DIM, 1), F32),
        compiler_params=_params("arbitrary"), name="dilated_step_attention")(
            *[cols(q) for q in qs], *[cols(x) for x in news], *views, *biases, self_bias)
    return o.reshape(n, A_HEADS * HEAD_DIM)


def _mixer_a(xp, xs, g, w_in, w_out, table, caches, batch, seq):
    w_in = w_in.astype(BF16)
    w_out = w_out.astype(BF16)
    pq0, pq1, pq2, pkv0, pkv1, pkv2 = _a_proj(xp, g, w_in)
    outs, lses, state = [], [], []
    for gi, ((win, dil), q, kv) in enumerate(zip(A_GROUPS, (pq0, pq1, pq2), (pkv0, pkv1, pkv2))):
        o, l = _a_attn_group(q, kv, _a_band_bias(table, gi, dil), batch, seq, dil)
        outs.append(o)
        lses.append(l)
        keep = min(win, seq)
        state.append(kv.reshape(batch, seq, 2 * A_GROUP_COLS)[:, seq - keep:].reshape(batch, keep, 2, A_GROUP_HEADS, HEAD_DIM))
    yp = _a_out(xp, outs, lses, w_out)

    n = xs.shape[0]
    sq0, sq1, sq2, skv0, skv1, skv2 = _a_proj(xs, g, w_in)
    so = _a_step_attn((sq0, sq1, sq2), (skv0, skv1, skv2), caches, table)
    ys = _out_proj(xs, so, w_out)
    for (win, _), kv, buf in zip(A_GROUPS, (skv0, skv1, skv2), caches):
        new = kv.reshape(n, 1, 2, A_GROUP_HEADS, HEAD_DIM)
        state.append(jnp.concatenate([buf, new], axis=1)[:, buf.shape[1] + 1 - min(win, buf.shape[1] + 1):])
    return yp, ys, state


C_TAIL = V7X_LANES


def _log_sigmoid(z):
    return -(jnp.maximum(-z, 0.0) + jnp.log1p(jnp.exp(-jnp.abs(z))))


def _c_proj_kernel(x_ref, g_ref, w_ref, wkv_t_ref, wt_ref, b_ref, tri_ref, q_ref, k_ref, v_ref, kt_ref, vt_ref,
                   lf_ref, dc_ref, carry_ref, *, tiles_per_seq):
    i = pl.program_id(0)
    h = _rms(x_ref[...], g_ref[...]).astype(BF16)
    q_ref[...] = _dot(h, w_ref[:, 0:C_COLS]).astype(BF16)
    k_ref[...] = _dot(h, w_ref[:, C_COLS:2 * C_COLS]).astype(BF16)
    v_ref[...] = _dot(h, w_ref[:, 2 * C_COLS:3 * C_COLS]).astype(BF16)
    kt_ref[0] = _dot_nt(wkv_t_ref[0:C_COLS, :], h)
    vt_ref[0] = _dot_nt(wkv_t_ref[C_COLS:2 * C_COLS, :], h)
    logf = _log_sigmoid(_dot(h, wt_ref[...]) + b_ref[...])
    lf_ref[...] = logf

    @pl.when(i % tiles_per_seq == 0)
    def _():
        carry_ref[...] = jnp.zeros_like(carry_ref)

    cum = _dot3(tri_ref[...], logf) + carry_ref[0:1, :]
    dc_ref[...] = cum
    carry_ref[...] = jnp.broadcast_to(cum[cum.shape[0] - 1:, :], carry_ref.shape)


def _c_proj(x, g, w, w_tail, bias, batch, seq):
    rows = x.shape[0]
    tm = _row_tile(rows) if seq > 1 else rows
    tiles_per_seq = max(seq // tm, 1)
    seqs_per_tile = max(tm // seq, 1)
    tri = jnp.asarray(np.tril(np.ones((tm, tm), np.float32)), BF16)
    wkv_t = w[:, C_COLS:].T
    big = pl.BlockSpec((tm, C_COLS), lambda i: (i, 0))
    small = pl.BlockSpec((tm, C_TAIL), lambda i: (i, 0))
    t_shape = (batch // seqs_per_tile, C_COLS, seq * seqs_per_tile)
    t_spec = pl.BlockSpec((1, C_COLS, tm), lambda i: (i // tiles_per_seq, 0, i % tiles_per_seq))
    return pl.pallas_call(
        functools.partial(_c_proj_kernel, tiles_per_seq=tiles_per_seq), grid=(rows // tm,),
        in_specs=[pl.BlockSpec((tm, D_MODEL), lambda i: (i, 0)),
                  pl.BlockSpec((1, D_MODEL), lambda i: (0, 0)),
                  pl.BlockSpec(w.shape, lambda i: (0, 0)),
                  pl.BlockSpec(wkv_t.shape, lambda i: (0, 0)),
                  pl.BlockSpec(w_tail.shape, lambda i: (0, 0)),
                  pl.BlockSpec((1, C_TAIL), lambda i: (0, 0)),
                  pl.BlockSpec(tri.shape, lambda i: (0, 0))],
        out_specs=[big, big, big, t_spec, t_spec, small, small],
        out_shape=[jax.ShapeDtypeStruct((rows, C_COLS), BF16)] * 3 + [jax.ShapeDtypeStruct(t_shape, F32)] * 2
        + [jax.ShapeDtypeStruct((rows, C_TAIL), F32)] * 2,
        scratch_shapes=[pltpu.VMEM((8, C_TAIL), F32)],
        compiler_params=_params("arbitrary"), name="forget_in_proj")(
            x, g.reshape(1, D_MODEL), w, wkv_t, w_tail, bias, tri)


C_TILE = 512


def _c_attn_kernel(q_ref, k_ref, v_ref, dq_ref, dk_ref, o_ref, m_ref, l_ref, acc_ref):
    hp = pl.program_id(1)
    i = pl.program_id(2)
    t = C_TILE
    q = q_ref[0]
    dq_tile = dq_ref[0]
    lane = lax.broadcasted_iota(I32, (t, V7X_LANES), 1)
    lo = lane < HALF
    qm = [_pick_head(q, 0), _pick_head(q, 1)]
    dq = [jnp.sum(jnp.where(lane == 2 * hp + half, dq_tile, 0.0), axis=1, keepdims=True) for half in range(2)]
    causal = lax.broadcasted_iota(I32, (t, t), 0) >= lax.broadcasted_iota(I32, (t, t), 1)
    m_ref[...] = jnp.full_like(m_ref, NEG_INF)
    l_ref[...] = jnp.zeros_like(l_ref)
    acc_ref[...] = jnp.zeros_like(acc_ref)

    def tile(c, diagonal):
        start = pl.multiple_of(c * t, t)
        k = k_ref[0, pl.ds(start, t), :]
        v = v_ref[0, pl.ds(start, t), :]
        dk = dk_ref[0, 0, c]
        for half in range(2):
            s = _dot_nt(qm[half], k) * SCALE + dq[half] - dk[half:half + 1, :]
            if diagonal:
                s = jnp.where(causal, s, NEG_INF)
            m_old = m_ref[half]
            m_new = jnp.maximum(m_old, jnp.max(s, axis=1, keepdims=True))
            p = jnp.exp(s - m_new[:, 0:1])
            if diagonal:
                p = jnp.where(causal, p, 0.0)
            alpha = jnp.exp(m_old - m_new)
            l_ref[half] = alpha * l_ref[half] + jnp.sum(p, axis=1, keepdims=True)
            acc_ref[half] = alpha * acc_ref[half] + _dot(p.astype(BF16), v)
            m_ref[half] = m_new

    def body(c, carry):
        tile(c, False)
        return carry

    lax.fori_loop(0, i, body, 0)
    tile(i, True)
    o_ref[0] = jnp.where(lo, acc_ref[0] / l_ref[0], acc_ref[1] / l_ref[1])


def _c_attn(q, k, v, dcum, batch, seq):
    t = C_TILE
    nt = seq // t
    pairs = C_HEADS // 2
    q3, k3, v3 = (a.reshape(batch, seq, C_COLS) for a in (q, k, v))
    dq = dcum.reshape(batch, seq, C_TAIL)
    dk = dq[:, :, :C_HEADS].transpose(0, 2, 1).reshape(batch, pairs, 2, nt, t).transpose(0, 1, 3, 2, 4)
    qspec = pl.BlockSpec((1, t, V7X_LANES), lambda b, h, i: (b, i, h))
    kspec = pl.BlockSpec((1, seq, V7X_LANES), lambda b, h, i: (b, 0, h))
    o = pl.pallas_call(
        _c_attn_kernel, grid=(batch, pairs, nt),
        in_specs=[qspec, kspec, kspec,
                  pl.BlockSpec((1, t, C_TAIL), lambda b, h, i: (b, i, 0)),
                  pl.BlockSpec((1, 1, nt, 2, t), lambda b, h, i: (b, h, 0, 0, 0))],
        out_specs=qspec, out_shape=jax.ShapeDtypeStruct((batch, seq, C_COLS), F32),
        scratch_shapes=[pltpu.VMEM((2, t, V7X_LANES), F32)] * 3,
        compiler_params=_params("arbitrary", "arbitrary", "arbitrary"), name="forget_attention")(
            q3, k3, v3, dq, dk)
    return o.reshape(batch * seq, C_COLS)


C_STEP_PAGES = 4


def _c_step_kernel(pt_ref, q_ref, kn_ref, vn_ref, lfn_ref, *rest):
    page_refs = rest[:3 * C_STEP_PAGES]
    lower_ref, o_ref, qb_ref, m_ref, l_ref, acc_ref, carry_ref = rest[3 * C_STEP_PAGES:]
    j = pl.program_id(1)
    page = qb_ref.shape[1]
    heads = (C_HEADS, HEAD_DIM)

    def per_head(x):
        return jnp.broadcast_to(x[:, None, :], heads + (x.shape[1],)).reshape(C_COLS, x.shape[1])

    @pl.when(j == 0)
    def _():
        q = q_ref[0].astype(F32)
        lane0 = lax.broadcasted_iota(I32, (C_COLS, page), 1) == 0
        qb_ref[...] = jnp.broadcast_to(q, (C_COLS, page))
        s_self = jnp.sum((kn_ref[0] * q).reshape(heads + (1,)), axis=1) * SCALE
        m_ref[...] = jnp.broadcast_to(s_self, (C_HEADS, page))
        l_ref[...] = jnp.where(lax.broadcasted_iota(I32, (C_HEADS, page), 1) == 0, 1.0, 0.0)
        acc_ref[...] = jnp.where(lane0, jnp.broadcast_to(vn_ref[0], (C_COLS, page)), 0.0)
        carry_ref[...] = jnp.broadcast_to(lfn_ref[0], (C_HEADS, page))

    for pg in range(C_STEP_PAGES):
        k_ref, v_ref, lf_ref = page_refs[3 * pg:3 * pg + 3]
        lf = lf_ref[0]
        hi, mid, lo3 = _split3(lf)
        decay = _dot(hi, lower_ref[...]) + _dot(mid, lower_ref[...]) + _dot(lo3, lower_ref[...]) + carry_ref[...]
        s = jnp.sum((k_ref[0] * qb_ref[...]).reshape(heads + (page,)), axis=1) * SCALE + decay
        m_old = m_ref[...]
        m_new = jnp.maximum(m_old, _row_max(s))
        p = jnp.exp(s - m_new)
        alpha = jnp.exp(m_old - m_new)
        l_ref[...] = alpha * l_ref[...] + p
        acc_ref[...] = per_head(alpha) * acc_ref[...] + per_head(p) * v_ref[0]
        m_ref[...] = m_new
        carry_ref[...] = carry_ref[...] + _row_sum(lf)

    @pl.when(j == pl.num_programs(1) - 1)
    def _():
        o_ref[0] = _row_sum(acc_ref[...]) / per_head(_row_sum(l_ref[...]))


def _c_step_attn(q, k_new, v_new, lf_new, cache_k, cache_v, cache_logf, page_table):
    n, n_pages = page_table.shape
    page = cache_k.shape[1]
    assert n_pages % C_STEP_PAGES == 0
    lower = jnp.asarray(np.tril(np.ones((page, page), np.float32), -1), BF16)
    col = lambda c: pl.BlockSpec((1, c, 1), lambda b, j, pt: (b, 0, 0))

    def paged(rows, pg):
        return pl.BlockSpec((1, rows, page), lambda b, j, pt: (pt[b, n_pages - 1 - (j * C_STEP_PAGES + pg)], 0, 0))

    k_view = cache_k.transpose(0, 2, 3, 1).reshape(-1, C_COLS, page)
    v_view = cache_v.transpose(0, 2, 3, 1).reshape(-1, C_COLS, page)
    lf_view = cache_logf.transpose(0, 2, 1)
    page_specs, page_args = [], []
    for pg in range(C_STEP_PAGES):
        page_specs += [paged(C_COLS, pg), paged(C_COLS, pg), paged(C_HEADS, pg)]
        page_args += [k_view, v_view, lf_view]
    o = pl.pallas_call(
        _c_step_kernel,
        grid_spec=pltpu.PrefetchScalarGridSpec(
            num_scalar_prefetch=1, grid=(n, n_pages // C_STEP_PAGES),
            in_specs=[col(C_COLS), col(C_COLS), col(C_COLS), col(C_HEADS)] + page_specs
            + [pl.BlockSpec(lower.shape, lambda b, j, pt: (0, 0))],
            out_specs=col(C_COLS),
            scratch_shapes=[pltpu.VMEM((C_COLS, page), F32), pltpu.VMEM((C_HEADS, page), F32),
                            pltpu.VMEM((C_HEADS, page), F32), pltpu.VMEM((C_COLS, page), F32),
                            pltpu.VMEM((C_HEADS, page), F32)]),
        out_shape=jax.ShapeDtypeStruct((n, C_COLS, 1), F32),
        compiler_params=_params("arbitrary", "arbitrary"), name="forget_paged_step")(
            page_table, q.reshape(n, C_COLS, 1), k_new.reshape(n, C_COLS, 1), v_new.reshape(n, C_COLS, 1),
            lf_new[:, :C_HEADS].reshape(n, C_HEADS, 1), *page_args, lower)
    return o.reshape(n, C_COLS)


def _mixer_c(xp, xs, g, w_in, forget_bias, w_out, caches, page_table, batch, seq):
    w = w_in[:, :3 * C_COLS].astype(BF16)
    w_tail = jnp.pad(w_in[:, 3 * C_COLS:], ((0, 0), (0, C_TAIL - C_HEADS))).astype(BF16)
    bias = jnp.pad(forget_bias, (0, C_TAIL - C_HEADS)).reshape(1, C_TAIL)
    w_out = w_out.astype(BF16)
    cache_k, cache_v, cache_logf = caches

    q, k, v, kt, vt, logf, dcum = _c_proj(xp, g, w, w_tail, bias, batch, seq)
    o = _c_attn(q, k, v, dcum, batch, seq)
    yp = _out_proj(xp, o, w_out)
    heads_last = lambda a: a.reshape(a.shape[0], C_HEADS, HEAD_DIM, a.shape[2]).transpose(0, 3, 1, 2)
    state = [heads_last(kt), heads_last(vt), logf[:, :C_HEADS].reshape(batch, seq, C_HEADS)]

    n = xs.shape[0]
    sq, sk, sv, skt, svt, slogf, _ = _c_proj(xs, g, w, w_tail, bias, n, 1)
    k_new, v_new = skt[0].T, svt[0].T
    so = _c_step_attn(sq, k_new, v_new, slogf, cache_k, cache_v, cache_logf, page_table)
    ys = _out_proj(xs, so, w_out)
    state += [k_new.reshape(n, 1, C_HEADS, HEAD_DIM), v_new.reshape(n, 1, C_HEADS, HEAD_DIM),
              slogf[:, :C_HEADS].reshape(n, 1, C_HEADS)]
    return yp, ys, state


B_TAIL = V7X_LANES
B_SLAB = 2 * B_KV_COLS


def _b_proj_kernel(x_ref, g_ref, w_ref, wt_ref, b_ref, q_ref, cmp_ref, sel_ref, win_ref, gate_ref):
    h = _rms(x_ref[...], g_ref[...]).astype(BF16)
    q_ref[...] = _dot(h, w_ref[:, 0:B_Q_COLS]).astype(BF16)
    c0 = B_Q_COLS
    cmp_ref[...] = _dot(h, w_ref[:, c0:c0 + B_SLAB])
    sel_ref[...] = _dot(h, w_ref[:, c0 + B_SLAB:c0 + 2 * B_SLAB])
    win_ref[...] = _dot(h, w_ref[:, c0 + 2 * B_SLAB:c0 + 3 * B_SLAB])
    gate_ref[...] = jax.nn.sigmoid(_dot(h, wt_ref[...]) + b_ref[...])


def _b_proj(x, g, w, w_tail, bias):
    rows = x.shape[0]
    tm = _row_tile(rows)
    spec = lambda cols: pl.BlockSpec((tm, cols), lambda i: (i, 0))
    shape = lambda cols, dt: jax.ShapeDtypeStruct((rows, cols), dt)
    return pl.pallas_call(
        _b_proj_kernel, grid=(rows // tm,),
        in_specs=[spec(D_MODEL), pl.BlockSpec((1, D_MODEL), lambda i: (0, 0)),
                  pl.BlockSpec(w.shape, lambda i: (0, 0)), pl.BlockSpec(w_tail.shape, lambda i: (0, 0)),
                  pl.BlockSpec((1, B_TAIL), lambda i: (0, 0))],
        out_specs=[spec(B_Q_COLS)] + [spec(B_SLAB)] * 3 + [spec(B_TAIL)],
        out_shape=[shape(B_Q_COLS, BF16)] + [shape(B_SLAB, F32)] * 3 + [shape(B_TAIL, F32)],
        compiler_params=_params("arbitrary"), name="sparse_in_proj_step")(x, g.reshape(1, D_MODEL), w, w_tail, bias)


B_GROUP_Q = 2 * V7X_LANES
B_GQ_COLS = B_KV_HEADS * B_GROUP_Q
B_GKV_COLS = B_KV_HEADS * V7X_LANES
B_GGATE_COLS = B_KV_HEADS * V7X_LANES


def _b_proj_grouped_kernel(x_ref, g_ref, w_ref, wst_ref, wt_ref, b_ref, q_ref, cmp_ref, cmpt_ref, selt_ref, wint_ref,
                           selg_ref, wing_ref, gate_ref):
    h = _rms(x_ref[...], g_ref[...]).astype(BF16)
    c = 0
    q_ref[...] = _dot(h, w_ref[:, c:c + B_GQ_COLS]).astype(BF16)
    c += B_GQ_COLS
    cmp_ref[...] = _dot(h, w_ref[:, c:c + B_SLAB]).astype(BF16)
    c += B_SLAB
    for ref in (selg_ref, wing_ref):
        ref[...] = _dot(h, w_ref[:, c:c + B_GKV_COLS]).astype(BF16)
        c += B_GKV_COLS
    for idx, ref in enumerate((cmpt_ref, selt_ref, wint_ref)):
        ref[0] = _dot_nt(wst_ref[idx * B_SLAB:(idx + 1) * B_SLAB, :], h)
    gate_ref[...] = jax.nn.sigmoid(_dot(h, wt_ref[...]) + b_ref[...])


def _b_grouped_weights(w_in, gate_bias):
    q = w_in[:, :B_Q_COLS].reshape(D_MODEL, B_KV_HEADS, B_REP * HEAD_DIM)
    q = jnp.pad(q, ((0, 0), (0, 0), (0, B_GROUP_Q - B_REP * HEAD_DIM))).reshape(D_MODEL, B_GQ_COLS)
    main = w_in[:, B_Q_COLS:B_Q_COLS + 3 * B_SLAB]

    def grouped(slab):
        kv = slab.reshape(D_MODEL, 2, B_KV_HEADS, HEAD_DIM)
        return kv.transpose(0, 2, 1, 3).reshape(D_MODEL, B_GKV_COLS)

    w = jnp.concatenate([q, main[:, :B_SLAB], grouped(main[:, B_SLAB:2 * B_SLAB]), grouped(main[:, 2 * B_SLAB:])],
                        axis=1)

    def gates(a):
        a = a.reshape(a.shape[:-1] + (3, B_KV_HEADS, B_REP))
        a = jnp.moveaxis(a, -2, -3).reshape(a.shape[:-3] + (B_KV_HEADS, 3 * B_REP))
        pad = [(0, 0)] * (a.ndim - 1) + [(0, V7X_LANES - 3 * B_REP)]
        return jnp.pad(a, pad).reshape(a.shape[:-2] + (B_GGATE_COLS,))

    tail = w_in[:, B_Q_COLS + 3 * B_SLAB:]
    return w.astype(BF16), main.T.astype(BF16), gates(tail).astype(BF16), gates(gate_bias).reshape(1, B_GGATE_COLS)


def _b_proj_grouped(x, g, w, w_state_t, w_tail, bias, batch, seq):
    rows = x.shape[0]
    tm = _row_tile(rows)
    tiles_per_seq = seq // tm
    spec = lambda cols: pl.BlockSpec((tm, cols), lambda i: (i, 0))
    shape = lambda cols, dt: jax.ShapeDtypeStruct((rows, cols), dt)
    t_spec = pl.BlockSpec((1, B_SLAB, tm), lambda i: (i // tiles_per_seq, 0, i % tiles_per_seq))
    t_shape = jax.ShapeDtypeStruct((batch, B_SLAB, seq), F32)
    return pl.pallas_call(
        _b_proj_grouped_kernel, grid=(rows // tm,),
        in_specs=[spec(D_MODEL), pl.BlockSpec((1, D_MODEL), lambda i: (0, 0)),
                  pl.BlockSpec(w.shape, lambda i: (0, 0)), pl.BlockSpec(w_state_t.shape, lambda i: (0, 0)),
                  pl.BlockSpec(w_tail.shape, lambda i: (0, 0)), pl.BlockSpec((1, B_GGATE_COLS), lambda i: (0, 0))],
        out_specs=[spec(B_GQ_COLS), spec(B_SLAB)] + [t_spec] * 3 + [spec(B_GKV_COLS)] * 2 + [spec(B_GGATE_COLS)],
        out_shape=[shape(B_GQ_COLS, BF16), shape(B_SLAB, BF16)] + [t_shape] * 3 + [shape(B_GKV_COLS, BF16)] * 2
        + [shape(B_GGATE_COLS, F32)],
        compiler_params=_params("arbitrary"), name="sparse_in_proj_prompt")(
            x, g.reshape(1, D_MODEL), w, w_state_t, w_tail, bias)


B_PHI_ROWS = 2 * B_KV_HEADS * B_PHI_HIDDEN
B_CHUNK_COLS = B_CMP_STRIDE * B_SLAB


def _compress_weights(phi_pos, phi_w1, phi_w2, pad_heads):
    half = B_CMP_BLOCK // B_CMP_STRIDE
    eye = jnp.eye(B_KV_HEADS, dtype=F32)
    w1r = phi_w1.reshape(2, half, B_CMP_STRIDE, HEAD_DIM, B_PHI_HIDDEN)
    first = jnp.einsum('cpsde,gh->cspgehd', w1r, eye).reshape(2, B_CMP_STRIDE, B_PHI_ROWS, B_KV_COLS)
    pos_term = jnp.einsum('cpsd,cpsde->cpe', phi_pos.reshape(2, half, B_CMP_STRIDE, HEAD_DIM), w1r,
                          precision=lax.Precision.HIGHEST)
    pos = jnp.broadcast_to(pos_term[:, :, None, :, None], (2, half, B_KV_HEADS, B_PHI_HIDDEN, V7X_LANES))
    second = jnp.einsum('ced,gh->cgdhe', phi_w2, eye)
    if pad_heads:
        second = jnp.pad(second, ((0, 0), (0, 0), (0, V7X_LANES - HEAD_DIM), (0, 0), (0, 0)))
    second = second.reshape(2, -1, B_KV_HEADS * B_PHI_HIDDEN)
    return first.astype(BF16), pos.reshape(2, B_PHI_ROWS, V7X_LANES), second.astype(BF16)


def _compress_first(load, first_ref, c):
    acc = None
    for s in range(B_CMP_STRIDE):
        part = _dot_nt(first_ref[c, s], load(s, c).astype(BF16))
        acc = part if acc is None else acc + part
    return acc


def _compress_second(acc, pos_ref, second_ref, c):
    n_chunks = acc.shape[1]
    acc = acc + pos_ref[c][:, 0:1]
    rows = B_PHI_ROWS // 2
    pre = acc[:rows] + pltpu.roll(acc[rows:], n_chunks - 1, 1)
    return _dot(second_ref[c], jax.nn.gelu(pre).astype(BF16))


def _b_compress_prompt_kernel(x_ref, first_ref, pos_ref, second_ref, o_ref):
    def load(s, c):
        c0 = s * B_SLAB + c * B_KV_COLS
        return x_ref[0, :, c0:c0 + B_KV_COLS]

    for c in range(2):
        o_ref[0, c] = _compress_second(_compress_first(load, first_ref, c), pos_ref, second_ref, c).astype(BF16)


def _b_compress_prompt(cmp, weights, batch, seq):
    n_chunks = seq // B_CMP_STRIDE
    first, pos, second = weights
    out_rows = second.shape[1]
    whole = lambda a: pl.BlockSpec(a.shape, lambda b: (0,) * a.ndim)
    return pl.pallas_call(
        _b_compress_prompt_kernel, grid=(batch,),
        in_specs=[pl.BlockSpec((1, n_chunks, B_CHUNK_COLS), lambda b: (b, 0, 0)),
                  whole(first), whole(pos), whole(second)],
        out_specs=pl.BlockSpec((1, 2, out_rows, n_chunks), lambda b: (b, 0, 0, 0)),
        out_shape=jax.ShapeDtypeStruct((batch, 2, out_rows, n_chunks), BF16),
        compiler_params=_params("arbitrary"), name="sparse_compress_prompt")(
            cmp.reshape(batch, n_chunks, B_CHUNK_COLS), first, pos, second).reshape(
                batch, 2, B_KV_HEADS, V7X_LANES, n_chunks)


def _b_compress_step_kernel(pt_ref, cache_ref, first_ref, pos_ref, second_ref, o_ref, buf_ref, acc_ref, sem_ref,
                            *, pages_per_half, chunks_per_page):
    n = pl.program_id(0)
    hf = pl.program_id(1)
    step = n * 2 + hf
    total = pl.num_programs(0) * 2
    slot = step % 2
    half_chunks = pages_per_half * chunks_per_page

    def copies(st, sl):
        nn = st // 2
        hh = st % 2
        return [pltpu.make_async_copy(cache_ref.at[pt_ref[nn, hh * pages_per_half + j]],
                                      buf_ref.at[sl, pl.ds(j * chunks_per_page, chunks_per_page), :],
                                      sem_ref.at[sl]) for j in range(pages_per_half)]

    @pl.when(step == 0)
    def _():
        for cp in copies(0, 0):
            cp.start()

    @pl.when(step + 1 < total)
    def _():
        for cp in copies(step + 1, 1 - slot):
            cp.start()

    for cp in copies(step, slot):
        cp.wait()

    def load(s, c):
        c0 = s * B_SLAB + c * B_KV_COLS
        return buf_ref[slot, :, c0:c0 + B_KV_COLS]

    for c in range(2):
        part = _compress_first(load, first_ref, c)

        @pl.when(hf == 0)
        def _():
            acc_ref[c, :, 0:half_chunks] = part

        @pl.when(hf == 1)
        def _():
            acc_ref[c, :, half_chunks:2 * half_chunks] = part

    @pl.when(hf == 1)
    def _():
        for c in range(2):
            o_ref[0, c] = _compress_second(acc_ref[c], pos_ref, second_ref, c).astype(BF16)


def _b_compress_step(cache_cmp, page_table, weights):
    n, n_pages = page_table.shape
    page = cache_cmp.shape[1]
    chunks_per_page = page // B_CMP_STRIDE
    n_chunks = n_pages * chunks_per_page
    pages_per_half = n_pages // 2
    first, pos, second = weights
    whole = lambda a: pl.BlockSpec(a.shape, lambda b, h, pt: (0,) * a.ndim)
    return pl.pallas_call(
        functools.partial(_b_compress_step_kernel, pages_per_half=pages_per_half, chunks_per_page=chunks_per_page),
        grid_spec=pltpu.PrefetchScalarGridSpec(
            num_scalar_prefetch=1, grid=(n, 2),
            in_specs=[pl.BlockSpec(memory_space=pl.ANY), whole(first), whole(pos), whole(second)],
            out_specs=pl.BlockSpec((1, 2, B_KV_COLS, n_chunks), lambda b, h, pt: (b, 0, 0, 0)),
            scratch_shapes=[pltpu.VMEM((2, n_chunks // 2, B_CHUNK_COLS), F32),
                            pltpu.VMEM((2, B_PHI_ROWS, n_chunks), F32),
                            pltpu.SemaphoreType.DMA((2,))]),
        out_shape=jax.ShapeDtypeStruct((n, 2, B_KV_COLS, n_chunks), BF16),
        compiler_params=_params("arbitrary", "arbitrary"), name="sparse_compress_step")(
            page_table, cache_cmp.reshape(-1, chunks_per_page, B_CHUNK_COLS), first, pos, second)


B_TILE = 128
B_FAR_TILES = REL_MAX_DIST // B_TILE + 2


def _toeplitz_tiles(table):
    shape = (B_FAR_TILES + 1, B_TILE, B_TILE)
    dist = (B_TILE * (lax.broadcasted_iota(I32, shape, 0) - 1) + lax.broadcasted_iota(I32, shape, 1)
            - lax.broadcasted_iota(I32, shape, 2))
    return jnp.where(dist[None] >= 0, _bias_of_dist(table, dist), NEG_INF)


def _cmp_bias_tiles(table, seq, n_chunks):
    shape = (seq // B_TILE, B_TILE, n_chunks)
    dist = (B_TILE * lax.broadcasted_iota(I32, shape, 0) + lax.broadcasted_iota(I32, shape, 1)
            - B_CMP_STRIDE * lax.broadcasted_iota(I32, shape, 2) - (B_CMP_BLOCK - 1))
    return _bias_of_dist(table, dist, head_axis=1)


def _overlap_t(n_sel_padded, n_cmp_padded, n_sel, n_cmp):
    c_start = np.arange(n_cmp_padded)[None, :] * B_CMP_STRIDE
    s_start = np.arange(n_sel_padded)[:, None] * B_SEL_BLOCK
    ov = (c_start < s_start + B_SEL_BLOCK) & (c_start + B_CMP_BLOCK > s_start)
    ov &= (np.arange(n_cmp_padded)[None, :] < n_cmp) & (np.arange(n_sel_padded)[:, None] < n_sel)
    return ov.astype(np.float32)


B_SEL_CHUNK = 512
B_SEL_PAD = 128
B_PAD_SCORE = -3e30


def _b_attn_kernel(q_ref, gate_ref, kvt_ref, tcmp_ref, sel_ref, win_ref, ttab_ref, ovt_ref, o_ref,
                   l_ref, acc_ref, imp_ref, *, n_sel):
    g = pl.program_id(1)
    i = pl.program_id(2)
    t = B_TILE
    rows = B_REP * t
    ch = B_SEL_CHUNK
    tiles_per_chunk = ch // t
    q = q_ref[0]
    gate = gate_ref[0]
    n_cmp_pad = kvt_ref.shape[4]
    qpos_c = i * t + lax.broadcasted_iota(I32, (rows, n_cmp_pad), 0) % t
    cidx = lax.broadcasted_iota(I32, (rows, n_cmp_pad), 1)
    mask_c = qpos_c >= cidx * B_CMP_STRIDE + (B_CMP_BLOCK - 1)
    blk = lax.broadcasted_iota(I32, (n_sel, t), 0)
    cur = (i * t + lax.broadcasted_iota(I32, (n_sel, t), 1)) // B_SEL_BLOCK
    forced = (blk == 0) | (blk == cur) | (blk == cur - 1)
    kb = lax.broadcasted_iota(I32, (ch, B_SEL_PAD), 1)
    kblk = lax.broadcasted_iota(I32, (ch, B_SEL_PAD), 0) // B_SEL_BLOCK
    n_win_tiles = B_WINDOW // t + 1
    wq = lax.broadcasted_iota(I32, (rows, n_win_tiles * t), 0) % t
    wk = lax.broadcasted_iota(I32, (rows, n_win_tiles * t), 1)
    mask_w = (B_WINDOW + wq - wk < B_WINDOW) & (wk // t >= n_win_tiles - 1 - i)
    n_chunks = (i + tiles_per_chunk) // tiles_per_chunk
    heads = lambda x: [x[r * t:(r + 1) * t] for r in range(B_REP)]
    q_all = jnp.concatenate(
        [_align_head(_pick_head(q[:, (r // 2) * V7X_LANES:(r // 2 + 1) * V7X_LANES], r % 2), r % 2, 0)
         for r in range(B_REP)], axis=0)

    def bias_rows(tiles):
        return jnp.concatenate([jnp.concatenate([ttab_ref[g * B_REP + r, d] for d in tiles], axis=1)
                                for r in range(B_REP)], axis=0)

    s = _dot(q_all, kvt_ref[0, 0, 0]) * SCALE + tcmp_ref[0].reshape(rows, n_cmp_pad)
    p, _, den = _masked_softmax(s, mask_c)
    p = p / den
    o_cmp = heads(_dot_nt(p.astype(BF16), kvt_ref[0, 1, 0]))
    p_sum = functools.reduce(lambda a, b: a + b, heads(p))

    hi, mid, lo3 = _split3(p_sum)
    imp = _dot_nt(ovt_ref[...], hi) + _dot_nt(ovt_ref[...], mid) + _dot_nt(ovt_ref[...], lo3)
    imp = jnp.where(forced, B_FORCE, jnp.where(blk <= cur, imp, NEG_INF))
    imp_ref[...] = imp

    def rank_body(j, rank):
        row = imp_ref[pl.ds(j, 1), :]
        ahead = (row > imp) | ((row == imp) & (j < blk))
        return rank + jnp.where(ahead, 1.0, 0.0)

    n_visible = jnp.minimum(n_sel, (i * t + t - 1) // B_SEL_BLOCK + 1)
    rank = lax.fori_loop(0, n_visible, rank_body, jnp.zeros((n_sel, t), F32))

    penalty = jnp.where(rank < B_TOPN, 0.0, NEG_INF)
    if n_sel < B_SEL_PAD:
        penalty = jnp.concatenate([penalty, jnp.zeros((B_SEL_PAD - n_sel, t), F32)], axis=0)
    penalty = jnp.transpose(penalty).astype(BF16)
    lhs = jnp.concatenate([q_all * SCALE, jnp.concatenate([penalty] * B_REP, axis=0)], axis=1)

    def scores(c):
        start = pl.multiple_of(c * ch, ch)
        kv = sel_ref[0, pl.ds(start, ch), :]
        onehot = jnp.where(kb == kblk + c * (ch // B_SEL_BLOCK), 1.0, 0.0).astype(BF16)
        tiles = [jnp.clip(i - c * tiles_per_chunk - j, -1, B_FAR_TILES - 1) + 1 for j in range(tiles_per_chunk)]
        return _dot_nt(lhs, jnp.concatenate([kv, onehot], axis=1)) + bias_rows(tiles), kv

    def lane_fold(x, op):
        return functools.reduce(op, [x[:, k * V7X_LANES:(k + 1) * V7X_LANES] for k in range(ch // V7X_LANES)])

    m_lanes = lax.fori_loop(0, n_chunks, lambda c, m: jnp.maximum(m, lane_fold(scores(c)[0], jnp.maximum)),
                            jnp.full((rows, V7X_LANES), NEG_INF, F32))
    m_rows = _row_max(m_lanes)

    l_ref[...] = jnp.zeros_like(l_ref)
    acc_ref[...] = jnp.zeros_like(acc_ref)

    def sum_body(c, carry):
        s, kv = scores(c)
        p = jnp.exp(s - m_rows)
        l_ref[...] = l_ref[...] + lane_fold(p, lambda a, b: a + b)
        acc_ref[...] = acc_ref[...] + _dot(p.astype(BF16), kv)
        return carry

    lax.fori_loop(0, n_chunks, sum_body, 0)
    o_sel = heads(acc_ref[...] / _row_sum(l_ref[...]))

    starts = [pl.multiple_of(jnp.maximum(i - (n_win_tiles - 1) + j, 0) * t, t) for j in range(n_win_tiles)]
    kvw = jnp.concatenate([win_ref[0, pl.ds(st, t), :] for st in starts], axis=0)
    s = _dot_nt(q_all, kvw) * SCALE + bias_rows([n_win_tiles - j for j in range(n_win_tiles)])
    p, _, den = _masked_softmax(s, mask_w)
    o_win = heads(_dot(p.astype(BF16), kvw) / den)
    slabs = [jnp.zeros((t, V7X_LANES), F32) for _ in range(B_GROUP_Q // V7X_LANES)]
    for r in range(B_REP):
        o_v = gate[:, B_REP + r:B_REP + r + 1] * o_sel[r] + gate[:, 2 * B_REP + r:2 * B_REP + r + 1] * o_win[r]
        o = gate[:, r:r + 1] * _pick_head(o_cmp[r], 0) + _align_head(_pick_head(o_v, 1), 1, 0)
        slabs[r // 2] = slabs[r // 2] + _align_head(o, 0, r % 2)
    for sidx, slab in enumerate(slabs):
        o_ref[0, :, sidx * V7X_LANES:(sidx + 1) * V7X_LANES] = slab


def _b_attn(q, gate, kvt, selg, wing, table, batch, seq):
    t = B_TILE
    n_chunks = seq // B_CMP_STRIDE
    n_sel = seq // B_SEL_BLOCK
    assert n_sel <= B_SEL_PAD and n_sel % 8 == 0 and seq % B_SEL_CHUNK == 0
    tcmp = _cmp_bias_tiles(table, seq, n_chunks)
    ttab = _toeplitz_tiles(table)
    ovt = jnp.asarray(_overlap_t(n_sel, n_chunks, n_sel, n_chunks - 1), BF16)
    whole = lambda a: pl.BlockSpec(a.shape, lambda b, g, i: (0,) * a.ndim)
    seq_block = pl.BlockSpec((1, seq, V7X_LANES), lambda b, g, i: (b, 0, g))
    o = pl.pallas_call(
        functools.partial(_b_attn_kernel, n_sel=n_sel), grid=(batch, B_KV_HEADS, seq // t),
        in_specs=[pl.BlockSpec((1, t, B_GROUP_Q), lambda b, g, i: (b, i, g)),
                  pl.BlockSpec((1, t, V7X_LANES), lambda b, g, i: (b, i, g)),
                  pl.BlockSpec((1, 2, 1, V7X_LANES, n_chunks), lambda b, g, i: (b, 0, g, 0, 0)),
                  pl.BlockSpec((1, B_REP, t, n_chunks), lambda b, g, i: (i, g, 0, 0)),
                  seq_block, seq_block, whole(ttab), whole(ovt)],
        out_specs=pl.BlockSpec((1, t, B_GROUP_Q), lambda b, g, i: (b, i, g)),
        out_shape=jax.ShapeDtypeStruct((batch, seq, B_GQ_COLS), F32),
        scratch_shapes=[pltpu.VMEM((B_REP * t, V7X_LANES), F32)] * 2 + [pltpu.VMEM((n_sel, t), F32)],
        compiler_params=_params("arbitrary", "arbitrary", "arbitrary"), name="sparse_prompt_attention")(
            q.reshape(batch, seq, B_GQ_COLS), gate.reshape(batch, seq, B_GGATE_COLS), kvt, tcmp,
            selg.reshape(batch, seq, B_GKV_COLS), wing.reshape(batch, seq, B_GKV_COLS), ttab, ovt)
    return o.reshape(batch * seq, B_GQ_COLS)


B_STEP_SEL_PAD = 256
B_STEP_ROWS = 8
B_STEP_SLOTS = 2


def _b_step_cmp_kernel(q_ref, kvt_ref, bias_ref, ov_ref, oc_ref, ids_ref, *, n_cmp, n_sel, cur):
    pad = B_STEP_SEL_PAD
    n_cmp_pad = kvt_ref.shape[3]
    q = q_ref[0].astype(F32)
    mask_c = lax.broadcasted_iota(I32, (1, n_cmp_pad), 1) < n_cmp
    blk = lax.broadcasted_iota(I32, (1, pad), 1)
    forced = (blk == 0) | (blk == cur) | (blk == cur - 1)
    b_idx = lax.broadcasted_iota(I32, (pad, pad), 0)
    j_idx = lax.broadcasted_iota(I32, (pad, pad), 1)
    slot = lax.broadcasted_iota(I32, (pad, V7X_LANES), 1).astype(F32)
    b_val = lax.broadcasted_iota(I32, (pad, V7X_LANES), 0).astype(F32)
    for g in range(B_KV_HEADS):
        kv_rows = slice(g * HEAD_DIM, (g + 1) * HEAD_DIM)
        kct = kvt_ref[0, 0, kv_rows, :].astype(F32)
        vct = kvt_ref[0, 1, kv_rows, :].astype(F32)
        p_sum = jnp.zeros((1, n_cmp_pad), F32)
        for r in range(B_REP):
            h = g * B_REP + r
            rows = slice(h * HEAD_DIM, (h + 1) * HEAD_DIM)
            s = _col_dot(kct, q[rows]) * SCALE + bias_ref[h:h + 1, :]
            p, _, den = _masked_softmax(s, mask_c)
            p = p / den
            p_sum = p_sum + p
            oc_ref[0, rows, :] = _row_sum(vct * p)
        hi, mid, lo3 = _split3(jnp.broadcast_to(p_sum, (B_STEP_ROWS, n_cmp_pad)))
        imp = (_dot(hi, ov_ref[...]) + _dot(mid, ov_ref[...]) + _dot(lo3, ov_ref[...]))[0:1]
        imp = jnp.where(forced, B_FORCE, jnp.where(blk <= cur, imp, NEG_INF))
        imp = jnp.where(blk < n_sel, imp, B_PAD_SCORE)
        other = jnp.broadcast_to(imp, (pad, pad))
        mine = jnp.transpose(other)
        ahead = (other > mine) | ((other == mine) & (j_idx < b_idx))
        rank = jnp.sum(jnp.where(ahead, 1.0, 0.0), axis=1, keepdims=True)
        ids = jnp.sum(jnp.where(rank == slot, b_val, 0.0), axis=0, keepdims=True)
        ids_ref[0, g:g + 1, :] = ids.astype(I32)


def _b_step_sel_kernel(ids_ref, pt_ref, q_ref, gate_ref, oc_ref, seln_ref, winn_ref, wbuf_ref, *rest,
                       n_past_blocks, blocks_per_page):
    sel_blocks = rest[:B_STEP_SLOTS * B_KV_HEADS]
    bblk_ref, bwin_ref, o_ref, m_ref, l_ref, acc_ref = rest[B_STEP_SLOTS * B_KV_HEADS:]
    n = pl.program_id(0)
    kb = pl.program_id(1)
    page = sel_blocks[0].shape[4]
    q = q_ref[0].astype(F32)
    seln = seln_ref[0]
    self_bias = bblk_ref[n_past_blocks]
    lane = lax.broadcasted_iota(I32, (1, page), 1)

    def layout(h):
        g = h // B_REP
        return (slice(h * HEAD_DIM, (h + 1) * HEAD_DIM), slice(g * HEAD_DIM, (g + 1) * HEAD_DIM),
                slice(B_KV_COLS + g * HEAD_DIM, B_KV_COLS + (g + 1) * HEAD_DIM))

    def self_score(h, new):
        rows, k_rows, _ = layout(h)
        return _col_dot(new[k_rows], q[rows]) * SCALE + self_bias[h:h + 1, 0:1]

    @pl.when(kb == 0)
    def _():
        for h in range(B_HEADS):
            _, _, v_rows = layout(h)
            m_ref[h] = jnp.broadcast_to(self_score(h, seln), (1, page))
            l_ref[h] = jnp.where(lane == 0, 1.0, 0.0)
            acc_ref[h] = jnp.where(lane == 0, jnp.broadcast_to(seln[v_rows], (HEAD_DIM, page)), 0.0)

    for slot in range(B_STEP_SLOTS):
        for g in range(B_KV_HEADS):
            blkid = ids_ref[(n * B_KV_HEADS + g) * V7X_LANES + kb * B_STEP_SLOTS + slot]
            mask = (blkid < n_past_blocks) & (lane // B_SEL_BLOCK == blkid % blocks_per_page)
            bias_blk = bblk_ref[jnp.minimum(blkid, n_past_blocks)]
            kt = sel_blocks[slot * B_KV_HEADS + g][0, 0, 0]
            vt = sel_blocks[slot * B_KV_HEADS + g][0, 1, 0]
            for r in range(B_REP):
                h = g * B_REP + r
                rows, _, _ = layout(h)
                s = jnp.where(mask, _col_dot(kt, q[rows]) * SCALE + bias_blk[h:h + 1, :], NEG_INF)
                m_old = m_ref[h]
                m_new = jnp.maximum(m_old, _row_max(s))
                p = jnp.where(mask, jnp.exp(s - m_new), 0.0)
                alpha = jnp.exp(m_old - m_new)
                l_ref[h] = alpha * l_ref[h] + p
                acc_ref[h] = alpha * acc_ref[h] + vt * p
                m_ref[h] = m_new

    @pl.when(kb == pl.num_programs(1) - 1)
    def _():
        gate = gate_ref[0]
        winn = winn_ref[0]
        wb = wbuf_ref.shape[4]
        wlane = lax.broadcasted_iota(I32, (1, wb), 1)
        wmask = (wb - wlane >= 0) & (wb - wlane < B_WINDOW)
        for h in range(B_HEADS):
            g = h // B_REP
            rows, _, v_rows = layout(h)
            s = jnp.where(wmask, _col_dot(wbuf_ref[0, 0, g], q[rows]) * SCALE + bwin_ref[h:h + 1, :], NEG_INF)
            s_self = self_score(h, winn)
            m = jnp.maximum(_row_max(s), s_self)
            pb = jnp.where(wmask, jnp.exp(s - m), 0.0)
            ps = jnp.exp(s_self - m)
            o_win = (_row_sum(wbuf_ref[0, 1, g] * pb) + ps * winn[v_rows]) / (_row_sum(pb) + ps)
            o_sel = _row_sum(acc_ref[h]) / _row_sum(l_ref[h])
            o_ref[0, rows, :] = (gate[:, h:h + 1] * oc_ref[0, rows, :] + gate[:, B_HEADS + h:B_HEADS + h + 1] * o_sel
                                 + gate[:, 2 * B_HEADS + h:2 * B_HEADS + h + 1] * o_win)


def _b_step_attn(q, gate, kvt, sel_new, win_new, cache_sel, cache_win, page_table, table):
    n, n_pages = page_table.shape
    page = cache_sel.shape[1]
    past = n_pages * page
    wb = cache_win.shape[1]
    assert wb == B_WINDOW and past % B_SEL_BLOCK == 0 and page % B_SEL_BLOCK == 0
    n_cmp_pad = past // B_CMP_STRIDE
    n_cmp = (past + 1) // B_CMP_STRIDE - (B_CMP_BLOCK // B_CMP_STRIDE) + 1
    n_sel = -(-(past + 1) // B_SEL_BLOCK)
    n_past_blocks = past // B_SEL_BLOCK
    blocks_per_page = page // B_SEL_BLOCK
    assert n_sel <= B_STEP_SEL_PAD and n_cmp <= n_cmp_pad and B_TOPN % B_STEP_SLOTS == 0

    bias_c = _bias_of_dist(table, np.maximum(past - (B_CMP_STRIDE * np.arange(n_cmp_pad) + B_CMP_BLOCK - 1), 0))
    ov = jnp.asarray(_overlap_t(B_STEP_SEL_PAD, n_cmp_pad, n_sel, n_cmp).T, BF16)
    col = lambda c: pl.BlockSpec((1, c, 1), lambda b: (b, 0, 0))
    q_col = q.reshape(n, B_Q_COLS, 1)
    oc, ids = pl.pallas_call(
        functools.partial(_b_step_cmp_kernel, n_cmp=n_cmp, n_sel=n_sel, cur=past // B_SEL_BLOCK), grid=(n,),
        in_specs=[col(B_Q_COLS), pl.BlockSpec((1, 2, B_KV_COLS, n_cmp_pad), lambda b: (b, 0, 0, 0)),
                  pl.BlockSpec(bias_c.shape, lambda b: (0, 0)), pl.BlockSpec(ov.shape, lambda b: (0, 0))],
        out_specs=[col(B_Q_COLS), pl.BlockSpec((1, B_KV_HEADS, V7X_LANES), lambda b: (b, 0, 0))],
        out_shape=[jax.ShapeDtypeStruct((n, B_Q_COLS, 1), F32),
                   jax.ShapeDtypeStruct((n, B_KV_HEADS, V7X_LANES), I32)],
        compiler_params=_params("arbitrary"), name="sparse_step_compressed")(q_col, kvt, bias_c, ov)

    blk_pos = B_SEL_BLOCK * np.arange(n_past_blocks + 1)[:, None] + np.arange(B_SEL_BLOCK)[None, :]
    bblk = _bias_of_dist(table, np.tile(np.maximum(past - blk_pos, 0), (1, blocks_per_page)), head_axis=1)
    bwin = _bias_of_dist(table, wb - np.arange(wb))
    scol = lambda c: pl.BlockSpec((1, c, 1), lambda b, kb, ids, pt: (b, 0, 0))

    def sel_spec(g, slot):
        def index(b, kb, ids, pt):
            blk = jnp.minimum(ids[(b * B_KV_HEADS + g) * V7X_LANES + kb * B_STEP_SLOTS + slot], n_past_blocks - 1)
            return (pt[b, blk // blocks_per_page], 0, g, 0, 0)
        return pl.BlockSpec((1, 2, 1, HEAD_DIM, page), index)

    sel_view = cache_sel.transpose(0, 2, 3, 4, 1)
    win_view = cache_win.transpose(0, 2, 3, 4, 1)
    o = pl.pallas_call(
        functools.partial(_b_step_sel_kernel, n_past_blocks=n_past_blocks, blocks_per_page=blocks_per_page),
        grid_spec=pltpu.PrefetchScalarGridSpec(
            num_scalar_prefetch=2, grid=(n, B_TOPN // B_STEP_SLOTS),
            in_specs=[scol(B_Q_COLS), pl.BlockSpec((1, 1, B_TAIL), lambda b, kb, ids, pt: (b, 0, 0)),
                      scol(B_Q_COLS), scol(B_SLAB), scol(B_SLAB),
                      pl.BlockSpec((1,) + win_view.shape[1:], lambda b, kb, ids, pt: (b, 0, 0, 0, 0))]
            + [sel_spec(g, slot) for slot in range(B_STEP_SLOTS) for g in range(B_KV_HEADS)]
            + [pl.BlockSpec(bblk.shape, lambda b, kb, ids, pt: (0, 0, 0)),
               pl.BlockSpec(bwin.shape, lambda b, kb, ids, pt: (0, 0))],
            out_specs=scol(B_Q_COLS),
            scratch_shapes=[pltpu.VMEM((B_HEADS, 1, page), F32), pltpu.VMEM((B_HEADS, 1, page), F32),
                            pltpu.VMEM((B_HEADS, HEAD_DIM, page), F32)]),
        out_shape=jax.ShapeDtypeStruct((n, B_Q_COLS, 1), F32),
        compiler_params=_params("arbitrary", "arbitrary"), name="sparse_step_selected")(
            ids.reshape(-1), page_table, q_col, gate.reshape(n, 1, B_TAIL), oc,
            sel_new.reshape(n, B_SLAB, 1), win_new.reshape(n, B_SLAB, 1), win_view,
            *([sel_view] * (B_STEP_SLOTS * B_KV_HEADS)), bblk, bwin)
    return o.reshape(n, B_Q_COLS)


def _mixer_b(xp, xs, g, w_in, gate_bias, phi_pos, phi_w1, phi_w2, w_out, table, caches, page_table, batch, seq):
    cache_cmp, cache_sel, cache_win = caches
    five = lambda a, rows: a.reshape(-1, rows, 2, B_KV_HEADS, HEAD_DIM)

    w, w_state_t, w_tail, bias = _b_grouped_weights(w_in, gate_bias)
    q, cmp, cmpt, selt, wint, selg, wing, gate = _b_proj_grouped(xp, g, w, w_state_t, w_tail, bias, batch, seq)
    kvt = _b_compress_prompt(cmp, _compress_weights(phi_pos, phi_w1, phi_w2, True), batch, seq)
    o = _b_attn(q, gate, kvt, selg, wing, table, batch, seq)
    w_out_grouped = jnp.pad(w_out.reshape(B_KV_HEADS, B_REP * HEAD_DIM, D_MODEL),
                            ((0, 0), (0, B_GROUP_Q - B_REP * HEAD_DIM), (0, 0))).reshape(B_GQ_COLS, D_MODEL)
    yp = _out_proj(xp, o, w_out_grouped.astype(BF16))
    keep = min(B_WINDOW, seq)
    tokens_first = lambda a: a.reshape(batch, 2, B_KV_HEADS, HEAD_DIM, a.shape[2]).transpose(0, 4, 1, 2, 3)
    state = [tokens_first(cmpt), tokens_first(selt), tokens_first(wint[:, :, seq - keep:])]

    n_main = B_Q_COLS + 3 * B_SLAB
    w = w_in[:, :n_main].astype(BF16)
    w_tail = jnp.pad(w_in[:, n_main:], ((0, 0), (0, B_TAIL - B_GATE_COLS))).astype(BF16)
    bias = jnp.pad(gate_bias, (0, B_TAIL - B_GATE_COLS)).reshape(1, B_TAIL)
    sq, scmp, ssel, swin, sgate = _b_proj(xs, g, w, w_tail, bias)
    skvt = _b_compress_step(cache_cmp, page_table, _compress_weights(phi_pos, phi_w1, phi_w2, False))
    so = _b_step_attn(sq, sgate, skvt, ssel, swin, cache_sel, cache_win, page_table, table)
    ys = _out_proj(xs, so, w_out.astype(BF16))
    wb = cache_win.shape[1]
    win_all = jnp.concatenate([cache_win, five(swin, 1)], axis=1)
    state += [five(scmp, 1), five(ssel, 1), win_all[:, wb + 1 - min(B_WINDOW, wb + 1):]]
    return yp, ys, state


def kernel(x_prompt, x_sample, cache_l0_w128, cache_l0_w512, cache_l0_w2048, cache_l1_cmp, cache_l1_sel, cache_l1_win, cache_l2_k, cache_l2_v, cache_l2_logf, cache_l3_w128, cache_l3_w512, cache_l3_w2048, page_table, norm_g, ffn_w_gate, ffn_w_up, ffn_w_down, final_norm_g, rel_bias_table, a_w_in, a_w_out, b_w_in, b_gate_bias, b_phi_pos, b_phi_w1, b_phi_w2, b_w_out, c_w_in, c_forget_bias, c_w_out):
    batch, seq, _ = x_prompt.shape
    depth = norm_g.shape[0]
    layer_caches = ((cache_l0_w128, cache_l0_w512, cache_l0_w2048), (cache_l1_cmp, cache_l1_sel, cache_l1_win),
                    (cache_l2_k, cache_l2_v, cache_l2_logf), (cache_l3_w128, cache_l3_w512, cache_l3_w2048))
    xp = x_prompt.reshape(batch * seq, D_MODEL)
    xs = x_sample.reshape(-1, D_MODEL)
    table = rel_bias_table
    new_state = []
    for i in range(depth):
        kind, j = i % N_MIXERS, i // N_MIXERS
        last = i == depth - 1
        f1 = (norm_g[i, 0], ffn_w_gate[i, 0].astype(BF16), ffn_w_up[i, 0].astype(BF16), ffn_w_down[i, 0].astype(BF16))
        f2 = (norm_g[i, 2], ffn_w_gate[i, 1].astype(BF16), ffn_w_up[i, 1].astype(BF16), ffn_w_down[i, 1].astype(BF16))
        xp, xs = _ffn(xp, *f1), _ffn(xs, *f1)
        if kind == 0:
            xp, xs, state = _mixer_a(xp, xs, norm_g[i, 1], a_w_in[j], a_w_out[j], table, layer_caches[i], batch, seq)
        elif kind == 1:
            xp, xs, state = _mixer_b(xp, xs, norm_g[i, 1], b_w_in[j], b_gate_bias[j], b_phi_pos[j], b_phi_w1[j],
                                     b_phi_w2[j], b_w_out[j], table, layer_caches[i], page_table, batch, seq)
        else:
            xp, xs, state = _mixer_c(xp, xs, norm_g[i, 1], c_w_in[j], c_forget_bias[j], c_w_out[j],
                                     layer_caches[i], page_table, batch, seq)
        new_state.extend(state)
        final_g = final_norm_g if last else None
        xp, xs = _ffn(xp, *f2, final_g), _ffn(xs, *f2, final_g)
    return (xp.reshape(batch, seq, D_MODEL), xs.reshape(-1, 1, D_MODEL), *new_state)
```

```python
import functools
import math

import numpy as np
import jax
import jax.numpy as jnp
from jax import lax
from jax.experimental import pallas as pl
from jax.experimental.pallas import tpu as pltpu

F32 = jnp.float32
BF16 = jnp.bfloat16
I32 = jnp.int32

D_MODEL = 1024
HEAD_DIM = 64
D_FF = 2816
RMS_EPS = 1e-6
NEG_INF = -1e30
SCALE = HEAD_DIM ** -0.5
N_MIXERS = 3

NUM_BUCKETS = 32
REL_MAX_DIST = 2048

A_GROUPS = ((128, 1), (512, 4), (2048, 16))
A_GROUP_HEADS = 4
A_GROUP_COLS = A_GROUP_HEADS * HEAD_DIM
A_HEADS = A_GROUP_HEADS * len(A_GROUPS)
A_BAND = 128

B_HEADS = 12
B_KV_HEADS = 4
B_REP = B_HEADS // B_KV_HEADS
B_CMP_BLOCK = 32
B_CMP_STRIDE = 16
B_SEL_BLOCK = 64
B_TOPN = 16
B_WINDOW = 512
B_PHI_HIDDEN = 128
B_FORCE = 1e4
B_Q_COLS = B_HEADS * HEAD_DIM
B_KV_COLS = B_KV_HEADS * HEAD_DIM
B_GATE_COLS = 3 * B_HEADS

C_HEADS = 16
C_COLS = C_HEADS * HEAD_DIM

V7X_LANES = 128
V7X_VMEM_LIMIT_BYTES = 56 * 1024 * 1024
ROW_TILE = 512
FF_CHUNK = 256
HALF = HEAD_DIM


def _params(*sem):
    return pltpu.CompilerParams(dimension_semantics=sem, vmem_limit_bytes=V7X_VMEM_LIMIT_BYTES)


def _dot(a, b):
    return jnp.dot(a, b, preferred_element_type=F32)


def _dot_nt(a, b):
    return lax.dot_general(a, b, (((1,), (1,)), ((), ())), preferred_element_type=F32)


def _split3(x):
    hi = x.astype(BF16)
    r1 = x - hi.astype(F32)
    mid = r1.astype(BF16)
    lo = (r1 - mid.astype(F32)).astype(BF16)
    return hi, mid, lo


def _dot3(a_bf16_exact, x):
    hi, mid, lo = _split3(x)
    return _dot(a_bf16_exact, hi) + _dot(a_bf16_exact, mid) + _dot(a_bf16_exact, lo)


def _rms(x, g):
    return x * lax.rsqrt(jnp.mean(x * x, axis=-1, keepdims=True) + RMS_EPS) * g


def _lane_lo(shape):
    return (lax.broadcasted_iota(I32, shape, len(shape) - 1) % V7X_LANES) < HALF


def _pick_head(slab, half):
    lo = _lane_lo(slab.shape)
    return jnp.where(lo if half == 0 else jnp.logical_not(lo), slab, jnp.zeros_like(slab))


def _align_head(slab, src_half, dst_half):
    if src_half == dst_half:
        return slab
    return pltpu.roll(slab, HALF, 1)


def _masked_softmax(s, mask):
    s = jnp.where(mask, s, NEG_INF)
    m = jnp.max(s, axis=-1, keepdims=True)
    p = jnp.where(mask, jnp.exp(s - m), 0.0)
    den = jnp.sum(p, axis=-1, keepdims=True)
    return p, m, jnp.where(den > 0, den, 1.0)


def _rel_bucket_np(dist):
    exact = NUM_BUCKETS // 2
    d = np.maximum(dist, 0)
    logd = (np.log(np.maximum(d, 1).astype(np.float32) / np.float32(exact))
            / np.float32(math.log(REL_MAX_DIST / exact))).astype(np.float32)
    far = np.minimum(exact + (logd * np.float32(NUM_BUCKETS - exact)).astype(np.int32), NUM_BUCKETS - 1)
    return np.where(d < exact, d, far).astype(np.int32)


def _bucket_starts():
    b = _rel_bucket_np(np.arange(2 * REL_MAX_DIST + 1))
    assert np.all(np.diff(b) >= 0) and b[-1] == NUM_BUCKETS - 1
    return [int(np.argmax(b >= k)) for k in range(NUM_BUCKETS)]


_BUCKET_STARTS = _bucket_starts()


def _bias_of_dist(table, dist, head_axis=0):
    d = jnp.expand_dims(jnp.asarray(dist, I32), head_axis)
    t = table.astype(F32)
    shape = [1] * d.ndim
    shape[head_axis] = t.shape[1]
    full = tuple(t.shape[1] if a == head_axis else n for a, n in enumerate(d.shape))
    out = jnp.broadcast_to(t[0].reshape(shape), full)
    for k in range(1, NUM_BUCKETS):
        out = jnp.where(d >= _BUCKET_STARTS[k], t[k].reshape(shape), out)
    return out


def _col_dot(kt, q_col):
    return jnp.sum(kt * q_col, axis=0, keepdims=True)


def _row_max(x):
    return jnp.max(x, axis=-1, keepdims=True)


def _row_sum(x):
    return jnp.sum(x, axis=-1, keepdims=True)


def _ffn_kernel(x_ref, g_ref, wg_ref, wu_ref, wd_ref, *rest, final):
    x = x_ref[...]
    h = _rms(x, g_ref[...]).astype(BF16)
    acc = jnp.zeros_like(x)
    for c in range(D_FF // FF_CHUNK):
        sl = slice(c * FF_CHUNK, (c + 1) * FF_CHUNK)
        gate = _dot(h, wg_ref[:, sl])
        up = _dot(h, wu_ref[:, sl])
        act = (gate * jax.nn.sigmoid(gate) * up).astype(BF16)
        acc = acc + _dot(act, wd_ref[sl, :])
    y = x + 0.5 * acc
    if final:
        gf_ref, o_ref = rest
        o_ref[...] = _rms(y, gf_ref[...])
    else:
        (o_ref,) = rest
        o_ref[...] = y


def _row_tile(rows):
    return ROW_TILE if rows % ROW_TILE == 0 else rows


def _ffn(x, g, wg, wu, wd, final_g=None):
    rows = x.shape[0]
    tm = _row_tile(rows)
    row = pl.BlockSpec((tm, D_MODEL), lambda i: (i, 0))
    vec = pl.BlockSpec((1, D_MODEL), lambda i: (0, 0))
    whole = lambda a: pl.BlockSpec(a.shape, lambda i: (0,) * a.ndim)
    args = [x, g.reshape(1, D_MODEL), wg, wu, wd]
    specs = [row, vec, whole(wg), whole(wu), whole(wd)]
    if final_g is not None:
        args.append(final_g.reshape(1, D_MODEL))
        specs.append(vec)
    return pl.pallas_call(
        functools.partial(_ffn_kernel, final=final_g is not None),
        grid=(rows // tm,), in_specs=specs, out_specs=row,
        out_shape=jax.ShapeDtypeStruct(x.shape, F32),
        compiler_params=_params("arbitrary"), name="macaron_swiglu")(*args)


def _out_proj_kernel(x_ref, o_ref, w_ref, y_ref):
    y_ref[...] = x_ref[...] + _dot(o_ref[...].astype(BF16), w_ref[...])


def _out_proj(x, o, w):
    rows = x.shape[0]
    tm = _row_tile(rows)
    cols = o.shape[1]
    return pl.pallas_call(
        _out_proj_kernel, grid=(rows // tm,),
        in_specs=[pl.BlockSpec((tm, D_MODEL), lambda i: (i, 0)),
                  pl.BlockSpec((tm, cols), lambda i: (i, 0)),
                  pl.BlockSpec(w.shape, lambda i: (0, 0))],
        out_specs=pl.BlockSpec((tm, D_MODEL), lambda i: (i, 0)),
        out_shape=jax.ShapeDtypeStruct(x.shape, F32),
        compiler_params=_params("arbitrary"), name="mixer_out_proj")(x, o, w)


def _a_proj_kernel(x_ref, g_ref, w_ref, q0_ref, q1_ref, q2_ref, kv0_ref, kv1_ref, kv2_ref):
    h = _rms(x_ref[...], g_ref[...]).astype(BF16)
    nq = A_HEADS * HEAD_DIM
    for gi, (q_ref, kv_ref) in enumerate(((q0_ref, kv0_ref), (q1_ref, kv1_ref), (q2_ref, kv2_ref))):
        c0 = gi * A_GROUP_COLS
        q_ref[...] = _dot(h, w_ref[:, c0:c0 + A_GROUP_COLS]).astype(BF16)
        kv_ref[:, :A_GROUP_COLS] = _dot(h, w_ref[:, nq + c0:nq + c0 + A_GROUP_COLS])
        kv_ref[:, A_GROUP_COLS:] = _dot(h, w_ref[:, 2 * nq + c0:2 * nq + c0 + A_GROUP_COLS])


def _a_proj(x, g, w):
    rows = x.shape[0]
    tm = _row_tile(rows)
    qspec = pl.BlockSpec((tm, A_GROUP_COLS), lambda i: (i, 0))
    kvspec = pl.BlockSpec((tm, 2 * A_GROUP_COLS), lambda i: (i, 0))
    return pl.pallas_call(
        _a_proj_kernel, grid=(rows // tm,),
        in_specs=[pl.BlockSpec((tm, D_MODEL), lambda i: (i, 0)),
                  pl.BlockSpec((1, D_MODEL), lambda i: (0, 0)),
                  pl.BlockSpec(w.shape, lambda i: (0, 0))],
        out_specs=[qspec] * 3 + [kvspec] * 3,
        out_shape=[jax.ShapeDtypeStruct((rows, A_GROUP_COLS), BF16)] * 3
        + [jax.ShapeDtypeStruct((rows, 2 * A_GROUP_COLS), F32)] * 3,
        compiler_params=_params("arbitrary"), name="dilated_in_proj")(x, g.reshape(1, D_MODEL), w)


def _a_attn_kernel(q_ref, kvc_ref, kvp_ref, bias_ref, o_ref, l_ref):
    blk = pl.program_id(2)
    band = A_BAND
    rows = A_GROUP_HEADS * band
    q = q_ref[0]
    k = jnp.concatenate([kvp_ref[0, :, :A_GROUP_COLS], kvc_ref[0, :, :A_GROUP_COLS]], axis=0).astype(BF16)
    v = jnp.concatenate([kvp_ref[0, :, A_GROUP_COLS:], kvc_ref[0, :, A_GROUP_COLS:]], axis=0).astype(BF16)
    head_of_lane = lax.broadcasted_iota(I32, (band, A_GROUP_COLS), 1) // HEAD_DIM
    q_all = jnp.concatenate([jnp.where(head_of_lane == h, q, jnp.zeros_like(q)) for h in range(A_GROUP_HEADS)],
                            axis=0)
    qi = lax.broadcasted_iota(I32, (rows, 2 * band), 0) % band
    kj = lax.broadcasted_iota(I32, (rows, 2 * band), 1)
    off = qi + band - kj
    mask = (off >= 0) & (off <= band) & ((kj >= band) | (blk > 0))
    s = _dot_nt(q_all, k) * SCALE + bias_ref[...].reshape(rows, 2 * band)
    p, m, den = _masked_softmax(s, mask)
    out = _dot(p.astype(BF16), v) / den
    lse = jnp.broadcast_to(m + jnp.log(den), (rows, A_GROUP_COLS))
    pick = lambda x: functools.reduce(
        lambda a, b: a + b, [jnp.where(head_of_lane == h, x[h * band:(h + 1) * band], 0.0) for h in range(A_GROUP_HEADS)])
    o_ref[0] = pick(out)
    l_ref[0] = pick(lse)


def _a_attn_group(q, kv, bias, batch, seq, dil):
    sub = seq // dil
    nb = sub // A_BAND
    qv = q.reshape(batch, sub, dil * A_GROUP_COLS)
    kvv = kv.reshape(batch, sub, dil * 2 * A_GROUP_COLS)
    qspec = pl.BlockSpec((1, A_BAND, A_GROUP_COLS), lambda b, r, i: (b, i, r))
    o, lse = pl.pallas_call(
        _a_attn_kernel, grid=(batch, dil, nb),
        in_specs=[qspec,
                  pl.BlockSpec((1, A_BAND, 2 * A_GROUP_COLS), lambda b, r, i: (b, i, r)),
                  pl.BlockSpec((1, A_BAND, 2 * A_GROUP_COLS), lambda b, r, i: (b, jnp.maximum(i - 1, 0), r)),
                  pl.BlockSpec(bias.shape, lambda b, r, i: (0, 0, 0))],
        out_specs=[qspec, qspec],
        out_shape=[jax.ShapeDtypeStruct(qv.shape, F32)] * 2,
        compiler_params=_params("arbitrary", "arbitrary", "arbitrary"), name="dilated_band_attention")(
            qv, kvv, kvv, bias)
    return o.reshape(batch * seq, A_GROUP_COLS), lse.reshape(batch * seq, A_GROUP_COLS)


def _a_band_bias(table, g, dil):
    off = np.arange(A_BAND)[:, None] + A_BAND - np.arange(2 * A_BAND)[None, :]
    dist = np.clip(off, 0, A_BAND) * dil
    return _bias_of_dist(table[:, g * A_GROUP_HEADS:(g + 1) * A_GROUP_HEADS], dist)


def _a_out_kernel(x_ref, o0_ref, o1_ref, o2_ref, l0_ref, l1_ref, l2_ref, w_ref, y_ref):
    ls = [l0_ref[...], l1_ref[...], l2_ref[...]]
    m = jnp.maximum(jnp.maximum(ls[0], ls[1]), ls[2])
    es = [jnp.exp(l - m) for l in ls]
    den = es[0] + es[1] + es[2]
    y = x_ref[...]
    for gi, o_ref in enumerate((o0_ref, o1_ref, o2_ref)):
        og = (o_ref[...] * (es[gi] / den)).astype(BF16)
        y = y + _dot(og, w_ref[gi * A_GROUP_COLS:(gi + 1) * A_GROUP_COLS, :])
    y_ref[...] = y


def _a_out(x, outs, lses, w):
    rows = x.shape[0]
    tm = _row_tile(rows)
    gspec = pl.BlockSpec((tm, A_GROUP_COLS), lambda i: (i, 0))
    xspec = pl.BlockSpec((tm, D_MODEL), lambda i: (i, 0))
    return pl.pallas_call(
        _a_out_kernel, grid=(rows // tm,),
        in_specs=[xspec] + [gspec] * 6 + [pl.BlockSpec(w.shape, lambda i: (0, 0))],
        out_specs=xspec, out_shape=jax.ShapeDtypeStruct(x.shape, F32),
        compiler_params=_params("arbitrary"), name="dilated_combine_out_proj")(x, *outs, *lses, w)


def _a_step_kernel(q0_ref, q1_ref, q2_ref, n0_ref, n1_ref, n2_ref, b0_ref, b1_ref, b2_ref,
                   bias0_ref, bias1_ref, bias2_ref, self_ref, o_ref):
    groups = ((q0_ref, n0_ref, b0_ref, bias0_ref), (q1_ref, n1_ref, b1_ref, bias1_ref),
              (q2_ref, n2_ref, b2_ref, bias2_ref))
    outs, lses = [], []
    for gi, (q_ref, n_ref, b_ref, bias_ref) in enumerate(groups):
        dil = A_GROUPS[gi][1]
        win = b_ref.shape[4]
        mask = lax.broadcasted_iota(I32, (1, win), 1) % dil == 0
        q = q_ref[0].astype(F32)
        new = n_ref[0]
        for j in range(A_GROUP_HEADS):
            head = gi * A_GROUP_HEADS + j
            rows = slice(j * HEAD_DIM, (j + 1) * HEAD_DIM)
            v_rows = slice(A_GROUP_COLS + j * HEAD_DIM, A_GROUP_COLS + (j + 1) * HEAD_DIM)
            s = jnp.where(mask, _col_dot(b_ref[0, 0, j], q[rows]) * SCALE + bias_ref[j:j + 1, :], NEG_INF)
            s_self = _col_dot(new[rows], q[rows]) * SCALE + self_ref[head:head + 1, 0:1]
            m = jnp.maximum(_row_max(s), s_self)
            pb = jnp.where(mask, jnp.exp(s - m), 0.0)
            ps = jnp.exp(s_self - m)
            den = _row_sum(pb) + ps
            outs.append((_row_sum(b_ref[0, 1, j] * pb) + ps * new[v_rows]) / den)
            lses.append(m + jnp.log(den))
    n_groups = len(groups)
    for j in range(A_GROUP_HEADS):
        ls = [lses[gi * A_GROUP_HEADS + j] for gi in range(n_groups)]
        m = functools.reduce(jnp.maximum, ls)
        es = [jnp.exp(l - m) for l in ls]
        den = functools.reduce(lambda a, b: a + b, es)
        for gi in range(n_groups):
            head = gi * A_GROUP_HEADS + j
            o_ref[0, head * HEAD_DIM:(head + 1) * HEAD_DIM, :] = outs[head] * (es[gi] / den)


def _a_step_attn(qs, news, bufs, table):
    n = qs[0].shape[0]
    cols = lambda a: a.reshape(n, a.shape[1], 1)
    views = [b.transpose(0, 2, 3, 4, 1) for b in bufs]
    biases = []
    for g, (win, _) in enumerate(A_GROUPS):
        assert bufs[g].shape[1] == win, "the step kernel reads a full window buffer"
        biases.append(_bias_of_dist(table[:, g * A_GROUP_HEADS:(g + 1) * A_GROUP_HEADS], win - np.arange(win)))
    self_bias = jnp.broadcast_to(_bias_of_dist(table, np.zeros((1,), np.int64)), (A_HEADS, V7X_LANES))
    col_spec = lambda c: pl.BlockSpec((1, c, 1), lambda b: (b, 0, 0))
    whole = lambda a: pl.BlockSpec(a.shape, lambda b: (0,) * a.ndim)
    o = pl.pallas_call(
        _a_step_kernel, grid=(n,),
        in_specs=[col_spec(A_GROUP_COLS)] * 3 + [col_spec(2 * A_GROUP_COLS)] * 3
        + [pl.BlockSpec((1,) + v.shape[1:], lambda b: (b, 0, 0, 0, 0)) for v in views]
        + [whole(b) for b in biases] + [whole(self_bias)],
        out_specs=col_spec(A_HEADS * HEAD_DIM),
        out_shape=jax.ShapeDtypeStruct((n, A_HEADS * HEAD_DIM, 1), F32),
        compiler_params=_params("arbitrary"), name="dilated_step_attention")(
            *[cols(q) for q in qs], *[cols(x) for x in news], *views, *biases, self_bias)
    return o.reshape(n, A_HEADS * HEAD_DIM)


def _mixer_a(xp, xs, g, w_in, w_out, table, caches, batch, seq):
    w_in = w_in.astype(BF16)
    w_out = w_out.astype(BF16)
    pq0, pq1, pq2, pkv0, pkv1, pkv2 = _a_proj(xp, g, w_in)
    outs, lses, state = [], [], []
    for gi, ((win, dil), q, kv) in enumerate(zip(A_GROUPS, (pq0, pq1, pq2), (pkv0, pkv1, pkv2))):
        o, l = _a_attn_group(q, kv, _a_band_bias(table, gi, dil), batch, seq, dil)
        outs.append(o)
        lses.append(l)
        keep = min(win, seq)
        state.append(kv.reshape(batch, seq, 2 * A_GROUP_COLS)[:, seq - keep:].reshape(batch, keep, 2, A_GROUP_HEADS, HEAD_DIM))
    yp = _a_out(xp, outs, lses, w_out)

    n = xs.shape[0]
    sq0, sq1, sq2, skv0, skv1, skv2 = _a_proj(xs, g, w_in)
    so = _a_step_attn((sq0, sq1, sq2), (skv0, skv1, skv2), caches, table)
    ys = _out_proj(xs, so, w_out)
    for (win, _), kv, buf in zip(A_GROUPS, (skv0, skv1, skv2), caches):
        new = kv.reshape(n, 1, 2, A_GROUP_HEADS, HEAD_DIM)
        state.append(jnp.concatenate([buf, new], axis=1)[:, buf.shape[1] + 1 - min(win, buf.shape[1] + 1):])
    return yp, ys, state


C_TAIL = V7X_LANES


def _log_sigmoid(z):
    return -(jnp.maximum(-z, 0.0) + jnp.log1p(jnp.exp(-jnp.abs(z))))


def _c_proj_kernel(x_ref, g_ref, w_ref, wkv_t_ref, wt_ref, b_ref, tri_ref, q_ref, k_ref, v_ref, kt_ref, vt_ref,
                   lf_ref, dc_ref, carry_ref, *, tiles_per_seq):
    i = pl.program_id(0)
    h = _rms(x_ref[...], g_ref[...]).astype(BF16)
    q_ref[...] = _dot(h, w_ref[:, 0:C_COLS]).astype(BF16)
    k_ref[...] = _dot(h, w_ref[:, C_COLS:2 * C_COLS]).astype(BF16)
    v_ref[...] = _dot(h, w_ref[:, 2 * C_COLS:3 * C_COLS]).astype(BF16)
    kt_ref[0] = _dot_nt(wkv_t_ref[0:C_COLS, :], h)
    vt_ref[0] = _dot_nt(wkv_t_ref[C_COLS:2 * C_COLS, :], h)
    logf = _log_sigmoid(_dot(h, wt_ref[...]) + b_ref[...])
    lf_ref[...] = logf

    @pl.when(i % tiles_per_seq == 0)
    def _():
        carry_ref[...] = jnp.zeros_like(carry_ref)

    cum = _dot3(tri_ref[...], logf) + carry_ref[0:1, :]
    dc_ref[...] = cum
    carry_ref[...] = jnp.broadcast_to(cum[cum.shape[0] - 1:, :], carry_ref.shape)


def _c_proj(x, g, w, w_tail, bias, batch, seq):
    rows = x.shape[0]
    tm = _row_tile(rows) if seq > 1 else rows
    tiles_per_seq = max(seq // tm, 1)
    seqs_per_tile = max(tm // seq, 1)
    tri = jnp.asarray(np.tril(np.ones((tm, tm), np.float32)), BF16)
    wkv_t = w[:, C_COLS:].T
    big = pl.BlockSpec((tm, C_COLS), lambda i: (i, 0))
    small = pl.BlockSpec((tm, C_TAIL), lambda i: (i, 0))
    t_shape = (batch // seqs_per_tile, C_COLS, seq * seqs_per_tile)
    t_spec = pl.BlockSpec((1, C_COLS, tm), lambda i: (i // tiles_per_seq, 0, i % tiles_per_seq))
    return pl.pallas_call(
        functools.partial(_c_proj_kernel, tiles_per_seq=tiles_per_seq), grid=(rows // tm,),
        in_specs=[pl.BlockSpec((tm, D_MODEL), lambda i: (i, 0)),
                  pl.BlockSpec((1, D_MODEL), lambda i: (0, 0)),
                  pl.BlockSpec(w.shape, lambda i: (0, 0)),
                  pl.BlockSpec(wkv_t.shape, lambda i: (0, 0)),
                  pl.BlockSpec(w_tail.shape, lambda i: (0, 0)),
                  pl.BlockSpec((1, C_TAIL), lambda i: (0, 0)),
                  pl.BlockSpec(tri.shape, lambda i: (0, 0))],
        out_specs=[big, big, big, t_spec, t_spec, small, small],
        out_shape=[jax.ShapeDtypeStruct((rows, C_COLS), BF16)] * 3 + [jax.ShapeDtypeStruct(t_shape, F32)] * 2
        + [jax.ShapeDtypeStruct((rows, C_TAIL), F32)] * 2,
        scratch_shapes=[pltpu.VMEM((8, C_TAIL), F32)],
        compiler_params=_params("arbitrary"), name="forget_in_proj")(
            x, g.reshape(1, D_MODEL), w, wkv_t, w_tail, bias, tri)


C_TILE = 512


def _c_attn_kernel(q_ref, k_ref, v_ref, dq_ref, dk_ref, o_ref, m_ref, l_ref, acc_ref):
    hp = pl.program_id(1)
    i = pl.program_id(2)
    t = C_TILE
    q = q_ref[0]
    dq_tile = dq_ref[0]
    lane = lax.broadcasted_iota(I32, (t, V7X_LANES), 1)
    lo = lane < HALF
    qm = [_pick_head(q, 0), _pick_head(q, 1)]
    dq = [jnp.sum(jnp.where(lane == 2 * hp + half, dq_tile, 0.0), axis=1, keepdims=True) for half in range(2)]
    causal = lax.broadcasted_iota(I32, (t, t), 0) >= lax.broadcasted_iota(I32, (t, t), 1)
    m_ref[...] = jnp.full_like(m_ref, NEG_INF)
    l_ref[...] = jnp.zeros_like(l_ref)
    acc_ref[...] = jnp.zeros_like(acc_ref)

    def tile(c, diagonal):
        start = pl.multiple_of(c * t, t)
        k = k_ref[0, pl.ds(start, t), :]
        v = v_ref[0, pl.ds(start, t), :]
        dk = dk_ref[0, 0, c]
        for half in range(2):
            s = _dot_nt(qm[half], k) * SCALE + dq[half] - dk[half:half + 1, :]
            if diagonal:
                s = jnp.where(causal, s, NEG_INF)
            m_old = m_ref[half]
            m_new = jnp.maximum(m_old, jnp.max(s, axis=1, keepdims=True))
            p = jnp.exp(s - m_new[:, 0:1])
            if diagonal:
                p = jnp.where(causal, p, 0.0)
            alpha = jnp.exp(m_old - m_new)
            l_ref[half] = alpha * l_ref[half] + jnp.sum(p, axis=1, keepdims=True)
            acc_ref[half] = alpha * acc_ref[half] + _dot(p.astype(BF16), v)
            m_ref[half] = m_new

    def body(c, carry):
        tile(c, False)
        return carry

    lax.fori_loop(0, i, body, 0)
    tile(i, True)
    o_ref[0] = jnp.where(lo, acc_ref[0] / l_ref[0], acc_ref[1] / l_ref[1])


def _c_attn(q, k, v, dcum, batch, seq):
    t = C_TILE
    nt = seq // t
    pairs = C_HEADS // 2
    q3, k3, v3 = (a.reshape(batch, seq, C_COLS) for a in (q, k, v))
    dq = dcum.reshape(batch, seq, C_TAIL)
    dk = dq[:, :, :C_HEADS].transpose(0, 2, 1).reshape(batch, pairs, 2, nt, t).transpose(0, 1, 3, 2, 4)
    qspec = pl.BlockSpec((1, t, V7X_LANES), lambda b, h, i: (b, i, h))
    kspec = pl.BlockSpec((1, seq, V7X_LANES), lambda b, h, i: (b, 0, h))
    o = pl.pallas_call(
        _c_attn_kernel, grid=(batch, pairs, nt),
        in_specs=[qspec, kspec, kspec,
                  pl.BlockSpec((1, t, C_TAIL), lambda b, h, i: (b, i, 0)),
                  pl.BlockSpec((1, 1, nt, 2, t), lambda b, h, i: (b, h, 0, 0, 0))],
        out_specs=qspec, out_shape=jax.ShapeDtypeStruct((batch, seq, C_COLS), F32),
        scratch_shapes=[pltpu.VMEM((2, t, V7X_LANES), F32)] * 3,
        compiler_params=_params("arbitrary", "arbitrary", "arbitrary"), name="forget_attention")(
            q3, k3, v3, dq, dk)
    return o.reshape(batch * seq, C_COLS)


C_STEP_PAGES = 8


def _c_step_kernel(pt_ref, q_ref, kn_ref, vn_ref, lfn_ref, *rest):
    page_refs = rest[:3 * C_STEP_PAGES]
    lower_ref, o_ref, qb_ref, m_ref, l_ref, acc_ref, carry_ref = rest[3 * C_STEP_PAGES:]
    j = pl.program_id(1)
    page = qb_ref.shape[1]
    heads = (C_HEADS, HEAD_DIM)

    def per_head(x):
        return jnp.broadcast_to(x[:, None, :], heads + (x.shape[1],)).reshape(C_COLS, x.shape[1])

    @pl.when(j == 0)
    def _():
        q = q_ref[0].astype(F32)
        lane0 = lax.broadcasted_iota(I32, (C_COLS, page), 1) == 0
        qb_ref[...] = jnp.broadcast_to(q, (C_COLS, page))
        s_self = jnp.sum((kn_ref[0] * q).reshape(heads + (1,)), axis=1) * SCALE
        m_ref[...] = jnp.broadcast_to(s_self, (C_HEADS, page))
        l_ref[...] = jnp.where(lax.broadcasted_iota(I32, (C_HEADS, page), 1) == 0, 1.0, 0.0)
        acc_ref[...] = jnp.where(lane0, jnp.broadcast_to(vn_ref[0], (C_COLS, page)), 0.0)
        carry_ref[...] = jnp.broadcast_to(lfn_ref[0], (C_HEADS, page))

    for pg in range(C_STEP_PAGES):
        k_ref, v_ref, lf_ref = page_refs[3 * pg:3 * pg + 3]
        lf = lf_ref[0]
        hi, mid, lo3 = _split3(lf)
        decay = _dot(hi, lower_ref[...]) + _dot(mid, lower_ref[...]) + _dot(lo3, lower_ref[...]) + carry_ref[...]
        s = jnp.sum((k_ref[0] * qb_ref[...]).reshape(heads + (page,)), axis=1) * SCALE + decay
        m_old = m_ref[...]
        m_new = jnp.maximum(m_old, _row_max(s))
        p = jnp.exp(s - m_new)
        alpha = jnp.exp(m_old - m_new)
        l_ref[...] = alpha * l_ref[...] + p
        acc_ref[...] = per_head(alpha) * acc_ref[...] + per_head(p) * v_ref[0]
        m_ref[...] = m_new
        carry_ref[...] = carry_ref[...] + _row_sum(lf)

    @pl.when(j == pl.num_programs(1) - 1)
    def _():
        o_ref[0] = _row_sum(acc_ref[...]) / per_head(_row_sum(l_ref[...]))


def _c_step_attn(q, k_new, v_new, lf_new, cache_k, cache_v, cache_logf, page_table):
    n, n_pages = page_table.shape
    page = cache_k.shape[1]
    assert n_pages % C_STEP_PAGES == 0
    lower = jnp.asarray(np.tril(np.ones((page, page), np.float32), -1), BF16)
    col = lambda c: pl.BlockSpec((1, c, 1), lambda b, j, pt: (b, 0, 0))

    def paged(rows, pg):
        return pl.BlockSpec((1, rows, page), lambda b, j, pt: (pt[b, n_pages - 1 - (j * C_STEP_PAGES + pg)], 0, 0))

    k_view = cache_k.transpose(0, 2, 3, 1).reshape(-1, C_COLS, page)
    v_view = cache_v.transpose(0, 2, 3, 1).reshape(-1, C_COLS, page)
    lf_view = cache_logf.transpose(0, 2, 1)
    page_specs, page_args = [], []
    for pg in range(C_STEP_PAGES):
        page_specs += [paged(C_COLS, pg), paged(C_COLS, pg), paged(C_HEADS, pg)]
        page_args += [k_view, v_view, lf_view]
    o = pl.pallas_call(
        _c_step_kernel,
        grid_spec=pltpu.PrefetchScalarGridSpec(
            num_scalar_prefetch=1, grid=(n, n_pages // C_STEP_PAGES),
            in_specs=[col(C_COLS), col(C_COLS), col(C_COLS), col(C_HEADS)] + page_specs
            + [pl.BlockSpec(lower.shape, lambda b, j, pt: (0, 0))],
            out_specs=col(C_COLS),
            scratch_shapes=[pltpu.VMEM((C_COLS, page), F32), pltpu.VMEM((C_HEADS, page), F32),
                            pltpu.VMEM((C_HEADS, page), F32), pltpu.VMEM((C_COLS, page), F32),
                            pltpu.VMEM((C_HEADS, page), F32)]),
        out_shape=jax.ShapeDtypeStruct((n, C_COLS, 1), F32),
        compiler_params=_params("arbitrary", "arbitrary"), name="forget_paged_step")(
            page_table, q.reshape(n, C_COLS, 1), k_new.reshape(n, C_COLS, 1), v_new.reshape(n, C_COLS, 1),
            lf_new[:, :C_HEADS].reshape(n, C_HEADS, 1), *page_args, lower)
    return o.reshape(n, C_COLS)


def _mixer_c(xp, xs, g, w_in, forget_bias, w_out, caches, page_table, batch, seq):
    w = w_in[:, :3 * C_COLS].astype(BF16)
    w_tail = jnp.pad(w_in[:, 3 * C_COLS:], ((0, 0), (0, C_TAIL - C_HEADS))).astype(BF16)
    bias = jnp.pad(forget_bias, (0, C_TAIL - C_HEADS)).reshape(1, C_TAIL)
    w_out = w_out.astype(BF16)
    cache_k, cache_v, cache_logf = caches

    q, k, v, kt, vt, logf, dcum = _c_proj(xp, g, w, w_tail, bias, batch, seq)
    o = _c_attn(q, k, v, dcum, batch, seq)
    yp = _out_proj(xp, o, w_out)
    heads_last = lambda a: a.reshape(a.shape[0], C_HEADS, HEAD_DIM, a.shape[2]).transpose(0, 3, 1, 2)
    state = [heads_last(kt), heads_last(vt), logf[:, :C_HEADS].reshape(batch, seq, C_HEADS)]

    n = xs.shape[0]
    sq, sk, sv, skt, svt, slogf, _ = _c_proj(xs, g, w, w_tail, bias, n, 1)
    k_new, v_new = skt[0].T, svt[0].T
    so = _c_step_attn(sq, k_new, v_new, slogf, cache_k, cache_v, cache_logf, page_table)
    ys = _out_proj(xs, so, w_out)
    state += [k_new.reshape(n, 1, C_HEADS, HEAD_DIM), v_new.reshape(n, 1, C_HEADS, HEAD_DIM),
              slogf[:, :C_HEADS].reshape(n, 1, C_HEADS)]
    return yp, ys, state


B_TAIL = V7X_LANES
B_SLAB = 2 * B_KV_COLS


def _b_proj_kernel(x_ref, g_ref, w_ref, wt_ref, b_ref, q_ref, cmp_ref, sel_ref, win_ref, gate_ref):
    h = _rms(x_ref[...], g_ref[...]).astype(BF16)
    q_ref[...] = _dot(h, w_ref[:, 0:B_Q_COLS]).astype(BF16)
    c0 = B_Q_COLS
    cmp_ref[...] = _dot(h, w_ref[:, c0:c0 + B_SLAB])
    sel_ref[...] = _dot(h, w_ref[:, c0 + B_SLAB:c0 + 2 * B_SLAB])
    win_ref[...] = _dot(h, w_ref[:, c0 + 2 * B_SLAB:c0 + 3 * B_SLAB])
    gate_ref[...] = jax.nn.sigmoid(_dot(h, wt_ref[...]) + b_ref[...])


def _b_proj(x, g, w, w_tail, bias):
    rows = x.shape[0]
    tm = _row_tile(rows)
    spec = lambda cols: pl.BlockSpec((tm, cols), lambda i: (i, 0))
    shape = lambda cols, dt: jax.ShapeDtypeStruct((rows, cols), dt)
    return pl.pallas_call(
        _b_proj_kernel, grid=(rows // tm,),
        in_specs=[spec(D_MODEL), pl.BlockSpec((1, D_MODEL), lambda i: (0, 0)),
                  pl.BlockSpec(w.shape, lambda i: (0, 0)), pl.BlockSpec(w_tail.shape, lambda i: (0, 0)),
                  pl.BlockSpec((1, B_TAIL), lambda i: (0, 0))],
        out_specs=[spec(B_Q_COLS)] + [spec(B_SLAB)] * 3 + [spec(B_TAIL)],
        out_shape=[shape(B_Q_COLS, BF16)] + [shape(B_SLAB, F32)] * 3 + [shape(B_TAIL, F32)],
        compiler_params=_params("arbitrary"), name="sparse_in_proj_step")(x, g.reshape(1, D_MODEL), w, w_tail, bias)


B_GROUP_Q = 2 * V7X_LANES
B_GQ_COLS = B_KV_HEADS * B_GROUP_Q
B_GKV_COLS = B_KV_HEADS * V7X_LANES
B_GGATE_COLS = B_KV_HEADS * V7X_LANES


def _b_proj_grouped_kernel(x_ref, g_ref, w_ref, wst_ref, wt_ref, b_ref, q_ref, cmp_ref, cmpt_ref, selt_ref, wint_ref,
                           selg_ref, wing_ref, gate_ref):
    h = _rms(x_ref[...], g_ref[...]).astype(BF16)
    c = 0
    q_ref[...] = _dot(h, w_ref[:, c:c + B_GQ_COLS]).astype(BF16)
    c += B_GQ_COLS
    cmp_ref[...] = _dot(h, w_ref[:, c:c + B_SLAB]).astype(BF16)
    c += B_SLAB
    for ref in (selg_ref, wing_ref):
        ref[...] = _dot(h, w_ref[:, c:c + B_GKV_COLS]).astype(BF16)
        c += B_GKV_COLS
    for idx, ref in enumerate((cmpt_ref, selt_ref, wint_ref)):
        ref[0] = _dot_nt(wst_ref[idx * B_SLAB:(idx + 1) * B_SLAB, :], h)
    gate_ref[...] = jax.nn.sigmoid(_dot(h, wt_ref[...]) + b_ref[...])


def _b_grouped_weights(w_in, gate_bias):
    q = w_in[:, :B_Q_COLS].reshape(D_MODEL, B_KV_HEADS, B_REP * HEAD_DIM)
    q = jnp.pad(q, ((0, 0), (0, 0), (0, B_GROUP_Q - B_REP * HEAD_DIM))).reshape(D_MODEL, B_GQ_COLS)
    main = w_in[:, B_Q_COLS:B_Q_COLS + 3 * B_SLAB]

    def grouped(slab):
        kv = slab.reshape(D_MODEL, 2, B_KV_HEADS, HEAD_DIM)
        return kv.transpose(0, 2, 1, 3).reshape(D_MODEL, B_GKV_COLS)

    w = jnp.concatenate([q, main[:, :B_SLAB], grouped(main[:, B_SLAB:2 * B_SLAB]), grouped(main[:, 2 * B_SLAB:])],
                        axis=1)

    def gates(a):
        a = a.reshape(a.shape[:-1] + (3, B_KV_HEADS, B_REP))
        a = jnp.moveaxis(a, -2, -3).reshape(a.shape[:-3] + (B_KV_HEADS, 3 * B_REP))
        pad = [(0, 0)] * (a.ndim - 1) + [(0, V7X_LANES - 3 * B_REP)]
        return jnp.pad(a, pad).reshape(a.shape[:-2] + (B_GGATE_COLS,))

    tail = w_in[:, B_Q_COLS + 3 * B_SLAB:]
    return w.astype(BF16), main.T.astype(BF16), gates(tail).astype(BF16), gates(gate_bias).reshape(1, B_GGATE_COLS)


def _b_proj_grouped(x, g, w, w_state_t, w_tail, bias, batch, seq):
    rows = x.shape[0]
    tm = _row_tile(rows)
    tiles_per_seq = seq // tm
    spec = lambda cols: pl.BlockSpec((tm, cols), lambda i: (i, 0))
    shape = lambda cols, dt: jax.ShapeDtypeStruct((rows, cols), dt)
    t_spec = pl.BlockSpec((1, B_SLAB, tm), lambda i: (i // tiles_per_seq, 0, i % tiles_per_seq))
    t_shape = jax.ShapeDtypeStruct((batch, B_SLAB, seq), F32)
    return pl.pallas_call(
        _b_proj_grouped_kernel, grid=(rows // tm,),
        in_specs=[spec(D_MODEL), pl.BlockSpec((1, D_MODEL), lambda i: (0, 0)),
                  pl.BlockSpec(w.shape, lambda i: (0, 0)), pl.BlockSpec(w_state_t.shape, lambda i: (0, 0)),
                  pl.BlockSpec(w_tail.shape, lambda i: (0, 0)), pl.BlockSpec((1, B_GGATE_COLS), lambda i: (0, 0))],
        out_specs=[spec(B_GQ_COLS), spec(B_SLAB)] + [t_spec] * 3 + [spec(B_GKV_COLS)] * 2 + [spec(B_GGATE_COLS)],
        out_shape=[shape(B_GQ_COLS, BF16), shape(B_SLAB, BF16)] + [t_shape] * 3 + [shape(B_GKV_COLS, BF16)] * 2
        + [shape(B_GGATE_COLS, F32)],
        compiler_params=_params("arbitrary"), name="sparse_in_proj_prompt")(
            x, g.reshape(1, D_MODEL), w, w_state_t, w_tail, bias)


B_PHI_ROWS = 2 * B_KV_HEADS * B_PHI_HIDDEN
B_CHUNK_COLS = B_CMP_STRIDE * B_SLAB


def _compress_weights(phi_pos, phi_w1, phi_w2, pad_heads):
    half = B_CMP_BLOCK // B_CMP_STRIDE
    eye = jnp.eye(B_KV_HEADS, dtype=F32)
    w1r = phi_w1.reshape(2, half, B_CMP_STRIDE, HEAD_DIM, B_PHI_HIDDEN)
    first = jnp.einsum('cpsde,gh->cspgehd', w1r, eye).reshape(2, B_CMP_STRIDE, B_PHI_ROWS, B_KV_COLS)
    pos_term = jnp.einsum('cpsd,cpsde->cpe', phi_pos.reshape(2, half, B_CMP_STRIDE, HEAD_DIM), w1r,
                          precision=lax.Precision.HIGHEST)
    pos = jnp.broadcast_to(pos_term[:, :, None, :, None], (2, half, B_KV_HEADS, B_PHI_HIDDEN, V7X_LANES))
    second = jnp.einsum('ced,gh->cgdhe', phi_w2, eye)
    if pad_heads:
        second = jnp.pad(second, ((0, 0), (0, 0), (0, V7X_LANES - HEAD_DIM), (0, 0), (0, 0)))
    second = second.reshape(2, -1, B_KV_HEADS * B_PHI_HIDDEN)
    return first.astype(BF16), pos.reshape(2, B_PHI_ROWS, V7X_LANES), second.astype(BF16)


def _compress_first(load, first_ref, c):
    acc = None
    for s in range(B_CMP_STRIDE):
        part = _dot_nt(first_ref[c, s], load(s, c).astype(BF16))
        acc = part if acc is None else acc + part
    return acc


def _compress_second(acc, pos_ref, second_ref, c):
    n_chunks = acc.shape[1]
    acc = acc + pos_ref[c][:, 0:1]
    rows = B_PHI_ROWS // 2
    pre = acc[:rows] + pltpu.roll(acc[rows:], n_chunks - 1, 1)
    return _dot(second_ref[c], jax.nn.gelu(pre).astype(BF16))


def _b_compress_prompt_kernel(x_ref, first_ref, pos_ref, second_ref, o_ref):
    def load(s, c):
        c0 = s * B_SLAB + c * B_KV_COLS
        return x_ref[0, :, c0:c0 + B_KV_COLS]

    for c in range(2):
        o_ref[0, c] = _compress_second(_compress_first(load, first_ref, c), pos_ref, second_ref, c).astype(BF16)


def _b_compress_prompt(cmp, weights, batch, seq):
    n_chunks = seq // B_CMP_STRIDE
    first, pos, second = weights
    out_rows = second.shape[1]
    whole = lambda a: pl.BlockSpec(a.shape, lambda b: (0,) * a.ndim)
    return pl.pallas_call(
        _b_compress_prompt_kernel, grid=(batch,),
        in_specs=[pl.BlockSpec((1, n_chunks, B_CHUNK_COLS), lambda b: (b, 0, 0)),
                  whole(first), whole(pos), whole(second)],
        out_specs=pl.BlockSpec((1, 2, out_rows, n_chunks), lambda b: (b, 0, 0, 0)),
        out_shape=jax.ShapeDtypeStruct((batch, 2, out_rows, n_chunks), BF16),
        compiler_params=_params("arbitrary"), name="sparse_compress_prompt")(
            cmp.reshape(batch, n_chunks, B_CHUNK_COLS), first, pos, second).reshape(
                batch, 2, B_KV_HEADS, V7X_LANES, n_chunks)


def _b_compress_step_kernel(pt_ref, cache_ref, first_ref, pos_ref, second_ref, o_ref, buf_ref, acc_ref, sem_ref,
                            *, pages_per_half, chunks_per_page):
    n = pl.program_id(0)
    hf = pl.program_id(1)
    step = n * 2 + hf
    total = pl.num_programs(0) * 2
    slot = step % 2
    half_chunks = pages_per_half * chunks_per_page

    def copies(st, sl):
        nn = st // 2
        hh = st % 2
        return [pltpu.make_async_copy(cache_ref.at[pt_ref[nn, hh * pages_per_half + j]],
                                      buf_ref.at[sl, pl.ds(j * chunks_per_page, chunks_per_page), :],
                                      sem_ref.at[sl]) for j in range(pages_per_half)]

    @pl.when(step == 0)
    def _():
        for cp in copies(0, 0):
            cp.start()

    @pl.when(step + 1 < total)
    def _():
        for cp in copies(step + 1, 1 - slot):
            cp.start()

    for cp in copies(step, slot):
        cp.wait()

    def load(s, c):
        c0 = s * B_SLAB + c * B_KV_COLS
        return buf_ref[slot, :, c0:c0 + B_KV_COLS]

    for c in range(2):
        part = _compress_first(load, first_ref, c)

        @pl.when(hf == 0)
        def _():
            acc_ref[c, :, 0:half_chunks] = part

        @pl.when(hf == 1)
        def _():
            acc_ref[c, :, half_chunks:2 * half_chunks] = part

    @pl.when(hf == 1)
    def _():
        for c in range(2):
            o_ref[0, c] = _compress_second(acc_ref[c], pos_ref, second_ref, c).astype(BF16)


def _b_compress_step(cache_cmp, page_table, weights):
    n, n_pages = page_table.shape
    page = cache_cmp.shape[1]
    chunks_per_page = page // B_CMP_STRIDE
    n_chunks = n_pages * chunks_per_page
    pages_per_half = n_pages // 2
    first, pos, second = weights
    whole = lambda a: pl.BlockSpec(a.shape, lambda b, h, pt: (0,) * a.ndim)
    return pl.pallas_call(
        functools.partial(_b_compress_step_kernel, pages_per_half=pages_per_half, chunks_per_page=chunks_per_page),
        grid_spec=pltpu.PrefetchScalarGridSpec(
            num_scalar_prefetch=1, grid=(n, 2),
            in_specs=[pl.BlockSpec(memory_space=pl.ANY), whole(first), whole(pos), whole(second)],
            out_specs=pl.BlockSpec((1, 2, B_KV_COLS, n_chunks), lambda b, h, pt: (b, 0, 0, 0)),
            scratch_shapes=[pltpu.VMEM((2, n_chunks // 2, B_CHUNK_COLS), F32),
                            pltpu.VMEM((2, B_PHI_ROWS, n_chunks), F32),
                            pltpu.SemaphoreType.DMA((2,))]),
        out_shape=jax.ShapeDtypeStruct((n, 2, B_KV_COLS, n_chunks), BF16),
        compiler_params=_params("arbitrary", "arbitrary"), name="sparse_compress_step")(
            page_table, cache_cmp.reshape(-1, chunks_per_page, B_CHUNK_COLS), first, pos, second)


B_TILE = 128
B_FAR_TILES = REL_MAX_DIST // B_TILE + 2


def _toeplitz_tiles(table):
    shape = (B_FAR_TILES + 1, B_TILE, B_TILE)
    dist = (B_TILE * (lax.broadcasted_iota(I32, shape, 0) - 1) + lax.broadcasted_iota(I32, shape, 1)
            - lax.broadcasted_iota(I32, shape, 2))
    return jnp.where(dist[None] >= 0, _bias_of_dist(table, dist), NEG_INF)


def _cmp_bias_tiles(table, seq, n_chunks):
    shape = (seq // B_TILE, B_TILE, n_chunks)
    dist = (B_TILE * lax.broadcasted_iota(I32, shape, 0) + lax.broadcasted_iota(I32, shape, 1)
            - B_CMP_STRIDE * lax.broadcasted_iota(I32, shape, 2) - (B_CMP_BLOCK - 1))
    return _bias_of_dist(table, dist, head_axis=1)


def _overlap_t(n_sel_padded, n_cmp_padded, n_sel, n_cmp):
    c_start = np.arange(n_cmp_padded)[None, :] * B_CMP_STRIDE
    s_start = np.arange(n_sel_padded)[:, None] * B_SEL_BLOCK
    ov = (c_start < s_start + B_SEL_BLOCK) & (c_start + B_CMP_BLOCK > s_start)
    ov &= (np.arange(n_cmp_padded)[None, :] < n_cmp) & (np.arange(n_sel_padded)[:, None] < n_sel)
    return ov.astype(np.float32)


B_SEL_CHUNK = 512
B_SEL_PAD = 128
B_PAD_SCORE = -3e30


def _b_attn_kernel(q_ref, gate_ref, kvt_ref, tcmp_ref, sel_ref, win_ref, ttab_ref, ovt_ref, o_ref,
                   l_ref, acc_ref, imp_ref, *, n_sel):
    g = pl.program_id(1)
    i = pl.program_id(2)
    t = B_TILE
    rows = B_REP * t
    ch = B_SEL_CHUNK
    tiles_per_chunk = ch // t
    q = q_ref[0]
    gate = gate_ref[0]
    n_cmp_pad = kvt_ref.shape[4]
    qpos_c = i * t + lax.broadcasted_iota(I32, (rows, n_cmp_pad), 0) % t
    cidx = lax.broadcasted_iota(I32, (rows, n_cmp_pad), 1)
    mask_c = qpos_c >= cidx * B_CMP_STRIDE + (B_CMP_BLOCK - 1)
    blk = lax.broadcasted_iota(I32, (n_sel, t), 0)
    cur = (i * t + lax.broadcasted_iota(I32, (n_sel, t), 1)) // B_SEL_BLOCK
    forced = (blk == 0) | (blk == cur) | (blk == cur - 1)
    kb = lax.broadcasted_iota(I32, (ch, B_SEL_PAD), 1)
    kblk = lax.broadcasted_iota(I32, (ch, B_SEL_PAD), 0) // B_SEL_BLOCK
    n_win_tiles = B_WINDOW // t + 1
    wq = lax.broadcasted_iota(I32, (rows, n_win_tiles * t), 0) % t
    wk = lax.broadcasted_iota(I32, (rows, n_win_tiles * t), 1)
    mask_w = (B_WINDOW + wq - wk < B_WINDOW) & (wk // t >= n_win_tiles - 1 - i)
    n_chunks = (i + tiles_per_chunk) // tiles_per_chunk
    heads = lambda x: [x[r * t:(r + 1) * t] for r in range(B_REP)]
    q_all = jnp.concatenate(
        [_align_head(_pick_head(q[:, (r // 2) * V7X_LANES:(r // 2 + 1) * V7X_LANES], r % 2), r % 2, 0)
         for r in range(B_REP)], axis=0)

    def bias_rows(tiles):
        return jnp.concatenate([jnp.concatenate([ttab_ref[g * B_REP + r, d] for d in tiles], axis=1)
                                for r in range(B_REP)], axis=0)

    s = _dot(q_all, kvt_ref[0, 0, 0]) * SCALE + tcmp_ref[0].reshape(rows, n_cmp_pad)
    p, _, den = _masked_softmax(s, mask_c)
    p = p / den
    o_cmp = heads(_dot_nt(p.astype(BF16), kvt_ref[0, 1, 0]))
    p_sum = functools.reduce(lambda a, b: a + b, heads(p))

    hi, mid, lo3 = _split3(p_sum)
    imp = _dot_nt(ovt_ref[...], hi) + _dot_nt(ovt_ref[...], mid) + _dot_nt(ovt_ref[...], lo3)
    imp = jnp.where(forced, B_FORCE, jnp.where(blk <= cur, imp, NEG_INF))
    imp_ref[...] = imp

    def rank_body(j, rank):
        row = imp_ref[pl.ds(j, 1), :]
        ahead = (row > imp) | ((row == imp) & (j < blk))
        return rank + jnp.where(ahead, 1.0, 0.0)

    n_visible = jnp.minimum(n_sel, (i * t + t - 1) // B_SEL_BLOCK + 1)
    rank = lax.fori_loop(0, n_visible, rank_body, jnp.zeros((n_sel, t), F32))

    penalty = jnp.where(rank < B_TOPN, 0.0, NEG_INF)
    if n_sel < B_SEL_PAD:
        penalty = jnp.concatenate([penalty, jnp.zeros((B_SEL_PAD - n_sel, t), F32)], axis=0)
    penalty = jnp.transpose(penalty).astype(BF16)
    lhs = jnp.concatenate([q_all * SCALE, jnp.concatenate([penalty] * B_REP, axis=0)], axis=1)

    def scores(c):
        start = pl.multiple_of(c * ch, ch)
        kv = sel_ref[0, pl.ds(start, ch), :]
        onehot = jnp.where(kb == kblk + c * (ch // B_SEL_BLOCK), 1.0, 0.0).astype(BF16)
        tiles = [jnp.clip(i - c * tiles_per_chunk - j, -1, B_FAR_TILES - 1) + 1 for j in range(tiles_per_chunk)]
        return _dot_nt(lhs, jnp.concatenate([kv, onehot], axis=1)) + bias_rows(tiles), kv

    def lane_fold(x, op):
        return functools.reduce(op, [x[:, k * V7X_LANES:(k + 1) * V7X_LANES] for k in range(ch // V7X_LANES)])

    m_lanes = lax.fori_loop(0, n_chunks, lambda c, m: jnp.maximum(m, lane_fold(scores(c)[0], jnp.maximum)),
                            jnp.full((rows, V7X_LANES), NEG_INF, F32))
    m_rows = _row_max(m_lanes)

    l_ref[...] = jnp.zeros_like(l_ref)
    acc_ref[...] = jnp.zeros_like(acc_ref)

    def sum_body(c, carry):
        s, kv = scores(c)
        p = jnp.exp(s - m_rows)
        l_ref[...] = l_ref[...] + lane_fold(p, lambda a, b: a + b)
        acc_ref[...] = acc_ref[...] + _dot(p.astype(BF16), kv)
        return carry

    lax.fori_loop(0, n_chunks, sum_body, 0)
    o_sel = heads(acc_ref[...] / _row_sum(l_ref[...]))

    starts = [pl.multiple_of(jnp.maximum(i - (n_win_tiles - 1) + j, 0) * t, t) for j in range(n_win_tiles)]
    kvw = jnp.concatenate([win_ref[0, pl.ds(st, t), :] for st in starts], axis=0)
    s = _dot_nt(q_all, kvw) * SCALE + bias_rows([n_win_tiles - j for j in range(n_win_tiles)])
    p, _, den = _masked_softmax(s, mask_w)
    o_win = heads(_dot(p.astype(BF16), kvw) / den)
    slabs = [jnp.zeros((t, V7X_LANES), F32) for _ in range(B_GROUP_Q // V7X_LANES)]
    for r in range(B_REP):
        o_v = gate[:, B_REP + r:B_REP + r + 1] * o_sel[r] + gate[:, 2 * B_REP + r:2 * B_REP + r + 1] * o_win[r]
        o = gate[:, r:r + 1] * _pick_head(o_cmp[r], 0) + _align_head(_pick_head(o_v, 1), 1, 0)
        slabs[r // 2] = slabs[r // 2] + _align_head(o, 0, r % 2)
    for sidx, slab in enumerate(slabs):
        o_ref[0, :, sidx * V7X_LANES:(sidx + 1) * V7X_LANES] = slab


def _b_attn(q, gate, kvt, selg, wing, table, batch, seq):
    t = B_TILE
    n_chunks = seq // B_CMP_STRIDE
    n_sel = seq // B_SEL_BLOCK
    assert n_sel <= B_SEL_PAD and n_sel % 8 == 0 and seq % B_SEL_CHUNK == 0
    tcmp = _cmp_bias_tiles(table, seq, n_chunks)
    ttab = _toeplitz_tiles(table)
    ovt = jnp.asarray(_overlap_t(n_sel, n_chunks, n_sel, n_chunks - 1), BF16)
    whole = lambda a: pl.BlockSpec(a.shape, lambda b, g, i: (0,) * a.ndim)
    seq_block = pl.BlockSpec((1, seq, V7X_LANES), lambda b, g, i: (b, 0, g))
    o = pl.pallas_call(
        functools.partial(_b_attn_kernel, n_sel=n_sel), grid=(batch, B_KV_HEADS, seq // t),
        in_specs=[pl.BlockSpec((1, t, B_GROUP_Q), lambda b, g, i: (b, i, g)),
                  pl.BlockSpec((1, t, V7X_LANES), lambda b, g, i: (b, i, g)),
                  pl.BlockSpec((1, 2, 1, V7X_LANES, n_chunks), lambda b, g, i: (b, 0, g, 0, 0)),
                  pl.BlockSpec((1, B_REP, t, n_chunks), lambda b, g, i: (i, g, 0, 0)),
                  seq_block, seq_block, whole(ttab), whole(ovt)],
        out_specs=pl.BlockSpec((1, t, B_GROUP_Q), lambda b, g, i: (b, i, g)),
        out_shape=jax.ShapeDtypeStruct((batch, seq, B_GQ_COLS), F32),
        scratch_shapes=[pltpu.VMEM((B_REP * t, V7X_LANES), F32)] * 2 + [pltpu.VMEM((n_sel, t), F32)],
        compiler_params=_params("arbitrary", "arbitrary", "arbitrary"), name="sparse_prompt_attention")(
            q.reshape(batch, seq, B_GQ_COLS), gate.reshape(batch, seq, B_GGATE_COLS), kvt, tcmp,
            selg.reshape(batch, seq, B_GKV_COLS), wing.reshape(batch, seq, B_GKV_COLS), ttab, ovt)
    return o.reshape(batch * seq, B_GQ_COLS)


B_STEP_SEL_PAD = 256
B_STEP_ROWS = 8
B_STEP_SLOTS = 4


def _b_step_cmp_kernel(q_ref, kvt_ref, bias_ref, ov_ref, oc_ref, ids_ref, *, n_cmp, n_sel, cur):
    pad = B_STEP_SEL_PAD
    n_cmp_pad = kvt_ref.shape[3]
    q = q_ref[0].astype(F32)
    mask_c = lax.broadcasted_iota(I32, (1, n_cmp_pad), 1) < n_cmp
    blk = lax.broadcasted_iota(I32, (1, pad), 1)
    forced = (blk == 0) | (blk == cur) | (blk == cur - 1)
    b_idx = lax.broadcasted_iota(I32, (pad, pad), 0)
    j_idx = lax.broadcasted_iota(I32, (pad, pad), 1)
    slot = lax.broadcasted_iota(I32, (pad, V7X_LANES), 1).astype(F32)
    b_val = lax.broadcasted_iota(I32, (pad, V7X_LANES), 0).astype(F32)
    for g in range(B_KV_HEADS):
        kv_rows = slice(g * HEAD_DIM, (g + 1) * HEAD_DIM)
        kct = kvt_ref[0, 0, kv_rows, :].astype(F32)
        vct = kvt_ref[0, 1, kv_rows, :].astype(F32)
        p_sum = jnp.zeros((1, n_cmp_pad), F32)
        for r in range(B_REP):
            h = g * B_REP + r
            rows = slice(h * HEAD_DIM, (h + 1) * HEAD_DIM)
            s = _col_dot(kct, q[rows]) * SCALE + bias_ref[h:h + 1, :]
            p, _, den = _masked_softmax(s, mask_c)
            p = p / den
            p_sum = p_sum + p
            oc_ref[0, rows, :] = _row_sum(vct * p)
        hi, mid, lo3 = _split3(jnp.broadcast_to(p_sum, (B_STEP_ROWS, n_cmp_pad)))
        imp = (_dot(hi, ov_ref[...]) + _dot(mid, ov_ref[...]) + _dot(lo3, ov_ref[...]))[0:1]
        imp = jnp.where(forced, B_FORCE, jnp.where(blk <= cur, imp, NEG_INF))
        imp = jnp.where(blk < n_sel, imp, B_PAD_SCORE)
        other = jnp.broadcast_to(imp, (pad, pad))
        mine = jnp.transpose(other)
        ahead = (other > mine) | ((other == mine) & (j_idx < b_idx))
        rank = jnp.sum(jnp.where(ahead, 1.0, 0.0), axis=1, keepdims=True)
        ids = jnp.sum(jnp.where(rank == slot, b_val, 0.0), axis=0, keepdims=True)
        ids_ref[0, g:g + 1, :] = ids.astype(I32)


def _b_step_sel_kernel(ids_ref, pt_ref, q_ref, gate_ref, oc_ref, seln_ref, winn_ref, wbuf_ref, *rest,
                       n_past_blocks, blocks_per_page):
    sel_blocks = rest[:B_STEP_SLOTS * B_KV_HEADS]
    bblk_ref, bwin_ref, o_ref, m_ref, l_ref, acc_ref = rest[B_STEP_SLOTS * B_KV_HEADS:]
    n = pl.program_id(0)
    kb = pl.program_id(1)
    page = sel_blocks[0].shape[4]
    q = q_ref[0].astype(F32)
    seln = seln_ref[0]
    self_bias = bblk_ref[n_past_blocks]
    lane = lax.broadcasted_iota(I32, (1, page), 1)

    def layout(h):
        g = h // B_REP
        return (slice(h * HEAD_DIM, (h + 1) * HEAD_DIM), slice(g * HEAD_DIM, (g + 1) * HEAD_DIM),
                slice(B_KV_COLS + g * HEAD_DIM, B_KV_COLS + (g + 1) * HEAD_DIM))

    def self_score(h, new):
        rows, k_rows, _ = layout(h)
        return _col_dot(new[k_rows], q[rows]) * SCALE + self_bias[h:h + 1, 0:1]

    @pl.when(kb == 0)
    def _():
        for h in range(B_HEADS):
            _, _, v_rows = layout(h)
            m_ref[h] = jnp.broadcast_to(self_score(h, seln), (1, page))
            l_ref[h] = jnp.where(lane == 0, 1.0, 0.0)
            acc_ref[h] = jnp.where(lane == 0, jnp.broadcast_to(seln[v_rows], (HEAD_DIM, page)), 0.0)

    for slot in range(B_STEP_SLOTS):
        for g in range(B_KV_HEADS):
            blkid = ids_ref[(n * B_KV_HEADS + g) * V7X_LANES + kb * B_STEP_SLOTS + slot]
            mask = (blkid < n_past_blocks) & (lane // B_SEL_BLOCK == blkid % blocks_per_page)
            bias_blk = bblk_ref[jnp.minimum(blkid, n_past_blocks)]
            kt = sel_blocks[slot * B_KV_HEADS + g][0, 0, 0]
            vt = sel_blocks[slot * B_KV_HEADS + g][0, 1, 0]
            for r in range(B_REP):
                h = g * B_REP + r
                rows, _, _ = layout(h)
                s = jnp.where(mask, _col_dot(kt, q[rows]) * SCALE + bias_blk[h:h + 1, :], NEG_INF)
                m_old = m_ref[h]
                m_new = jnp.maximum(m_old, _row_max(s))
                p = jnp.where(mask, jnp.exp(s - m_new), 0.0)
                alpha = jnp.exp(m_old - m_new)
                l_ref[h] = alpha * l_ref[h] + p
                acc_ref[h] = alpha * acc_ref[h] + vt * p
                m_ref[h] = m_new

    @pl.when(kb == pl.num_programs(1) - 1)
    def _():
        gate = gate_ref[0]
        winn = winn_ref[0]
        wb = wbuf_ref.shape[4]
        wlane = lax.broadcasted_iota(I32, (1, wb), 1)
        wmask = (wb - wlane >= 0) & (wb - wlane < B_WINDOW)
        for h in range(B_HEADS):
            g = h // B_REP
            rows, _, v_rows = layout(h)
            s = jnp.where(wmask, _col_dot(wbuf_ref[0, 0, g], q[rows]) * SCALE + bwin_ref[h:h + 1, :], NEG_INF)
            s_self = self_score(h, winn)
            m = jnp.maximum(_row_max(s), s_self)
            pb = jnp.where(wmask, jnp.exp(s - m), 0.0)
            ps = jnp.exp(s_self - m)
            o_win = (_row_sum(wbuf_ref[0, 1, g] * pb) + ps * winn[v_rows]) / (_row_sum(pb) + ps)
            o_sel = _row_sum(acc_ref[h]) / _row_sum(l_ref[h])
            o_ref[0, rows, :] = (gate[:, h:h + 1] * oc_ref[0, rows, :] + gate[:, B_HEADS + h:B_HEADS + h + 1] * o_sel
                                 + gate[:, 2 * B_HEADS + h:2 * B_HEADS + h + 1] * o_win)


def _b_step_attn(q, gate, kvt, sel_new, win_new, cache_sel, cache_win, page_table, table):
    n, n_pages = page_table.shape
    page = cache_sel.shape[1]
    past = n_pages * page
    wb = cache_win.shape[1]
    assert wb == B_WINDOW and past % B_SEL_BLOCK == 0 and page % B_SEL_BLOCK == 0
    n_cmp_pad = past // B_CMP_STRIDE
    n_cmp = (past + 1) // B_CMP_STRIDE - (B_CMP_BLOCK // B_CMP_STRIDE) + 1
    n_sel = -(-(past + 1) // B_SEL_BLOCK)
    n_past_blocks = past // B_SEL_BLOCK
    blocks_per_page = page // B_SEL_BLOCK
    assert n_sel <= B_STEP_SEL_PAD and n_cmp <= n_cmp_pad and B_TOPN % B_STEP_SLOTS == 0

    bias_c = _bias_of_dist(table, np.maximum(past - (B_CMP_STRIDE * np.arange(n_cmp_pad) + B_CMP_BLOCK - 1), 0))
    ov = jnp.asarray(_overlap_t(B_STEP_SEL_PAD, n_cmp_pad, n_sel, n_cmp).T, BF16)
    col = lambda c: pl.BlockSpec((1, c, 1), lambda b: (b, 0, 0))
    q_col = q.reshape(n, B_Q_COLS, 1)
    oc, ids = pl.pallas_call(
        functools.partial(_b_step_cmp_kernel, n_cmp=n_cmp, n_sel=n_sel, cur=past // B_SEL_BLOCK), grid=(n,),
        in_specs=[col(B_Q_COLS), pl.BlockSpec((1, 2, B_KV_COLS, n_cmp_pad), lambda b: (b, 0, 0, 0)),
                  pl.BlockSpec(bias_c.shape, lambda b: (0, 0)), pl.BlockSpec(ov.shape, lambda b: (0, 0))],
        out_specs=[col(B_Q_COLS), pl.BlockSpec((1, B_KV_HEADS, V7X_LANES), lambda b: (b, 0, 0))],
        out_shape=[jax.ShapeDtypeStruct((n, B_Q_COLS, 1), F32),
                   jax.ShapeDtypeStruct((n, B_KV_HEADS, V7X_LANES), I32)],
        compiler_params=_params("arbitrary"), name="sparse_step_compressed")(q_col, kvt, bias_c, ov)

    blk_pos = B_SEL_BLOCK * np.arange(n_past_blocks + 1)[:, None] + np.arange(B_SEL_BLOCK)[None, :]
    bblk = _bias_of_dist(table, np.tile(np.maximum(past - blk_pos, 0), (1, blocks_per_page)), head_axis=1)
    bwin = _bias_of_dist(table, wb - np.arange(wb))
    scol = lambda c: pl.BlockSpec((1, c, 1), lambda b, kb, ids, pt: (b, 0, 0))

    def sel_spec(g, slot):
        def index(b, kb, ids, pt):
            blk = jnp.minimum(ids[(b * B_KV_HEADS + g) * V7X_LANES + kb * B_STEP_SLOTS + slot], n_past_blocks - 1)
            return (pt[b, blk // blocks_per_page], 0, g, 0, 0)
        return pl.BlockSpec((1, 2, 1, HEAD_DIM, page), index)

    sel_view = cache_sel.transpose(0, 2, 3, 4, 1)
    win_view = cache_win.transpose(0, 2, 3, 4, 1)
    o = pl.pallas_call(
        functools.partial(_b_step_sel_kernel, n_past_blocks=n_past_blocks, blocks_per_page=blocks_per_page),
        grid_spec=pltpu.PrefetchScalarGridSpec(
            num_scalar_prefetch=2, grid=(n, B_TOPN // B_STEP_SLOTS),
            in_specs=[scol(B_Q_COLS), pl.BlockSpec((1, 1, B_TAIL), lambda b, kb, ids, pt: (b, 0, 0)),
                      scol(B_Q_COLS), scol(B_SLAB), scol(B_SLAB),
                      pl.BlockSpec((1,) + win_view.shape[1:], lambda b, kb, ids, pt: (b, 0, 0, 0, 0))]
            + [sel_spec(g, slot) for slot in range(B_STEP_SLOTS) for g in range(B_KV_HEADS)]
            + [pl.BlockSpec(bblk.shape, lambda b, kb, ids, pt: (0, 0, 0)),
               pl.BlockSpec(bwin.shape, lambda b, kb, ids, pt: (0, 0))],
            out_specs=scol(B_Q_COLS),
            scratch_shapes=[pltpu.VMEM((B_HEADS, 1, page), F32), pltpu.VMEM((B_HEADS, 1, page), F32),
                            pltpu.VMEM((B_HEADS, HEAD_DIM, page), F32)]),
        out_shape=jax.ShapeDtypeStruct((n, B_Q_COLS, 1), F32),
        compiler_params=_params("arbitrary", "arbitrary"), name="sparse_step_selected")(
            ids.reshape(-1), page_table, q_col, gate.reshape(n, 1, B_TAIL), oc,
            sel_new.reshape(n, B_SLAB, 1), win_new.reshape(n, B_SLAB, 1), win_view,
            *([sel_view] * (B_STEP_SLOTS * B_KV_HEADS)), bblk, bwin)
    return o.reshape(n, B_Q_COLS)


def _mixer_b(xp, xs, g, w_in, gate_bias, phi_pos, phi_w1, phi_w2, w_out, table, caches, page_table, batch, seq):
    cache_cmp, cache_sel, cache_win = caches
    five = lambda a, rows: a.reshape(-1, rows, 2, B_KV_HEADS, HEAD_DIM)

    w, w_state_t, w_tail, bias = _b_grouped_weights(w_in, gate_bias)
    q, cmp, cmpt, selt, wint, selg, wing, gate = _b_proj_grouped(xp, g, w, w_state_t, w_tail, bias, batch, seq)
    kvt = _b_compress_prompt(cmp, _compress_weights(phi_pos, phi_w1, phi_w2, True), batch, seq)
    o = _b_attn(q, gate, kvt, selg, wing, table, batch, seq)
    w_out_grouped = jnp.pad(w_out.reshape(B_KV_HEADS, B_REP * HEAD_DIM, D_MODEL),
                            ((0, 0), (0, B_GROUP_Q - B_REP * HEAD_DIM), (0, 0))).reshape(B_GQ_COLS, D_MODEL)
    yp = _out_proj(xp, o, w_out_grouped.astype(BF16))
    keep = min(B_WINDOW, seq)
    tokens_first = lambda a: a.reshape(batch, 2, B_KV_HEADS, HEAD_DIM, a.shape[2]).transpose(0, 4, 1, 2, 3)
    state = [tokens_first(cmpt), tokens_first(selt), tokens_first(wint[:, :, seq - keep:])]

    n_main = B_Q_COLS + 3 * B_SLAB
    w = w_in[:, :n_main].astype(BF16)
    w_tail = jnp.pad(w_in[:, n_main:], ((0, 0), (0, B_TAIL - B_GATE_COLS))).astype(BF16)
    bias = jnp.pad(gate_bias, (0, B_TAIL - B_GATE_COLS)).reshape(1, B_TAIL)
    sq, scmp, ssel, swin, sgate = _b_proj(xs, g, w, w_tail, bias)
    skvt = _b_compress_step(cache_cmp, page_table, _compress_weights(phi_pos, phi_w1, phi_w2, False))
    so = _b_step_attn(sq, sgate, skvt, ssel, swin, cache_sel, cache_win, page_table, table)
    ys = _out_proj(xs, so, w_out.astype(BF16))
    wb = cache_win.shape[1]
    win_all = jnp.concatenate([cache_win, five(swin, 1)], axis=1)
    state += [five(scmp, 1), five(ssel, 1), win_all[:, wb + 1 - min(B_WINDOW, wb + 1):]]
    return yp, ys, state


def kernel(x_prompt, x_sample, cache_l0_w128, cache_l0_w512, cache_l0_w2048, cache_l1_cmp, cache_l1_sel, cache_l1_win, cache_l2_k, cache_l2_v, cache_l2_logf, cache_l3_w128, cache_l3_w512, cache_l3_w2048, page_table, norm_g, ffn_w_gate, ffn_w_up, ffn_w_down, final_norm_g, rel_bias_table, a_w_in, a_w_out, b_w_in, b_gate_bias, b_phi_pos, b_phi_w1, b_phi_w2, b_w_out, c_w_in, c_forget_bias, c_w_out):
    batch, seq, _ = x_prompt.shape
    depth = norm_g.shape[0]
    layer_caches = ((cache_l0_w128, cache_l0_w512, cache_l0_w2048), (cache_l1_cmp, cache_l1_sel, cache_l1_win),
                    (cache_l2_k, cache_l2_v, cache_l2_logf), (cache_l3_w128, cache_l3_w512, cache_l3_w2048))
    xp = x_prompt.reshape(batch * seq, D_MODEL)
    xs = x_sample.reshape(-1, D_MODEL)
    table = rel_bias_table
    new_state = []
    for i in range(depth):
        kind, j = i % N_MIXERS, i // N_MIXERS
        last = i == depth - 1
        f1 = (norm_g[i, 0], ffn_w_gate[i, 0].astype(BF16), ffn_w_up[i, 0].astype(BF16), ffn_w_down[i, 0].astype(BF16))
        f2 = (norm_g[i, 2], ffn_w_gate[i, 1].astype(BF16), ffn_w_up[i, 1].astype(BF16), ffn_w_down[i, 1].astype(BF16))
        xp, xs = _ffn(xp, *f1), _ffn(xs, *f1)
        if kind == 0:
            xp, xs, state = _mixer_a(xp, xs, norm_g[i, 1], a_w_in[j], a_w_out[j], table, layer_caches[i], batch, seq)
        elif kind == 1:
            xp, xs, state = _mixer_b(xp, xs, norm_g[i, 1], b_w_in[j], b_gate_bias[j], b_phi_pos[j], b_phi_w1[j],
                                     b_phi_w2[j], b_w_out[j], table, layer_caches[i], page_table, batch, seq)
        else:
            xp, xs, state = _mixer_c(xp, xs, norm_g[i, 1], c_w_in[j], c_forget_bias[j], c_w_out[j],
                                     layer_caches[i], page_table, batch, seq)
        new_state.extend(state)
        final_g = final_norm_g if last else None
        xp, xs = _ffn(xp, *f2, final_g), _ffn(xs, *f2, final_g)
    return (xp.reshape(batch, seq, D_MODEL), xs.reshape(-1, 1, D_MODEL), *new_state)
```

```python
import functools
import math

import numpy as np
import jax
import jax.numpy as jnp
from jax import lax
from jax.experimental import pallas as pl
from jax.experimental.pallas import tpu as pltpu

F32 = jnp.float32
BF16 = jnp.bfloat16
I32 = jnp.int32

D_MODEL = 1024
HEAD_DIM = 64
D_FF = 2816
RMS_EPS = 1e-6
NEG_INF = -1e30
SCALE = HEAD_DIM ** -0.5
N_MIXERS = 3

NUM_BUCKETS = 32
REL_MAX_DIST = 2048

A_GROUPS = ((128, 1), (512, 4), (2048, 16))
A_GROUP_HEADS = 4
A_GROUP_COLS = A_GROUP_HEADS * HEAD_DIM
A_HEADS = A_GROUP_HEADS * len(A_GROUPS)
A_BAND = 128

B_HEADS = 12
B_KV_HEADS = 4
B_REP = B_HEADS // B_KV_HEADS
B_CMP_BLOCK = 32
B_CMP_STRIDE = 16
B_SEL_BLOCK = 64
B_TOPN = 16
B_WINDOW = 512
B_PHI_HIDDEN = 128
B_FORCE = 1e4
B_Q_COLS = B_HEADS * HEAD_DIM
B_KV_COLS = B_KV_HEADS * HEAD_DIM
B_GATE_COLS = 3 * B_HEADS

C_HEADS = 16
C_COLS = C_HEADS * HEAD_DIM

V7X_LANES = 128
V7X_VMEM_LIMIT_BYTES = 56 * 1024 * 1024
ROW_TILE = 512
FF_CHUNK = 256
HALF = HEAD_DIM


def _params(*sem):
    return pltpu.CompilerParams(dimension_semantics=sem, vmem_limit_bytes=V7X_VMEM_LIMIT_BYTES)


def _dot(a, b):
    return jnp.dot(a, b, preferred_element_type=F32)


def _dot_nt(a, b):
    return lax.dot_general(a, b, (((1,), (1,)), ((), ())), preferred_element_type=F32)


def _split3(x):
    hi = x.astype(BF16)
    r1 = x - hi.astype(F32)
    mid = r1.astype(BF16)
    lo = (r1 - mid.astype(F32)).astype(BF16)
    return hi, mid, lo


def _dot3(a_bf16_exact, x):
    hi, mid, lo = _split3(x)
    return _dot(a_bf16_exact, hi) + _dot(a_bf16_exact, mid) + _dot(a_bf16_exact, lo)


def _rms(x, g):
    return x * lax.rsqrt(jnp.mean(x * x, axis=-1, keepdims=True) + RMS_EPS) * g


def _lane_lo(shape):
    return (lax.broadcasted_iota(I32, shape, len(shape) - 1) % V7X_LANES) < HALF


def _pick_head(slab, half):
    lo = _lane_lo(slab.shape)
    return jnp.where(lo if half == 0 else jnp.logical_not(lo), slab, jnp.zeros_like(slab))


def _align_head(slab, src_half, dst_half):
    if src_half == dst_half:
        return slab
    return pltpu.roll(slab, HALF, 1)


def _masked_softmax(s, mask):
    s = jnp.where(mask, s, NEG_INF)
    m = jnp.max(s, axis=-1, keepdims=True)
    p = jnp.where(mask, jnp.exp(s - m), 0.0)
    den = jnp.sum(p, axis=-1, keepdims=True)
    return p, m, jnp.where(den > 0, den, 1.0)


def _rel_bucket_np(dist):
    exact = NUM_BUCKETS // 2
    d = np.maximum(dist, 0)
    logd = (np.log(np.maximum(d, 1).astype(np.float32) / np.float32(exact))
            / np.float32(math.log(REL_MAX_DIST / exact))).astype(np.float32)
    far = np.minimum(exact + (logd * np.float32(NUM_BUCKETS - exact)).astype(np.int32), NUM_BUCKETS - 1)
    return np.where(d < exact, d, far).astype(np.int32)


def _bucket_starts():
    b = _rel_bucket_np(np.arange(2 * REL_MAX_DIST + 1))
    assert np.all(np.diff(b) >= 0) and b[-1] == NUM_BUCKETS - 1
    return [int(np.argmax(b >= k)) for k in range(NUM_BUCKETS)]


_BUCKET_STARTS = _bucket_starts()


def _bias_of_dist(table, dist, head_axis=0):
    d = jnp.expand_dims(jnp.asarray(dist, I32), head_axis)
    t = table.astype(F32)
    shape = [1] * d.ndim
    shape[head_axis] = t.shape[1]
    full = tuple(t.shape[1] if a == head_axis else n for a, n in enumerate(d.shape))
    out = jnp.broadcast_to(t[0].reshape(shape), full)
    for k in range(1, NUM_BUCKETS):
        out = jnp.where(d >= _BUCKET_STARTS[k], t[k].reshape(shape), out)
    return out


def _col_dot(kt, q_col):
    return jnp.sum(kt * q_col, axis=0, keepdims=True)


def _row_max(x):
    return jnp.max(x, axis=-1, keepdims=True)


def _row_sum(x):
    return jnp.sum(x, axis=-1, keepdims=True)


def _ffn_kernel(x_ref, g_ref, wg_ref, wu_ref, wd_ref, *rest, final):
    x = x_ref[...]
    h = _rms(x, g_ref[...]).astype(BF16)
    acc = jnp.zeros_like(x)
    for c in range(D_FF // FF_CHUNK):
        sl = slice(c * FF_CHUNK, (c + 1) * FF_CHUNK)
        gate = _dot(h, wg_ref[:, sl])
        up = _dot(h, wu_ref[:, sl])
        act = (gate * jax.nn.sigmoid(gate) * up).astype(BF16)
        acc = acc + _dot(act, wd_ref[sl, :])
    y = x + 0.5 * acc
    if final:
        gf_ref, o_ref = rest
        o_ref[...] = _rms(y, gf_ref[...])
    else:
        (o_ref,) = rest
        o_ref[...] = y


def _row_tile(rows):
    return ROW_TILE if rows % ROW_TILE == 0 else rows


def _ffn(x, g, wg, wu, wd, final_g=None):
    rows = x.shape[0]
    tm = _row_tile(rows)
    row = pl.BlockSpec((tm, D_MODEL), lambda i: (i, 0))
    vec = pl.BlockSpec((1, D_MODEL), lambda i: (0, 0))
    whole = lambda a: pl.BlockSpec(a.shape, lambda i: (0,) * a.ndim)
    args = [x, g.reshape(1, D_MODEL), wg, wu, wd]
    specs = [row, vec, whole(wg), whole(wu), whole(wd)]
    if final_g is not None:
        args.append(final_g.reshape(1, D_MODEL))
        specs.append(vec)
    return pl.pallas_call(
        functools.partial(_ffn_kernel, final=final_g is not None),
        grid=(rows // tm,), in_specs=specs, out_specs=row,
        out_shape=jax.ShapeDtypeStruct(x.shape, F32),
        compiler_params=_params("arbitrary"), name="macaron_swiglu")(*args)


def _out_proj_kernel(x_ref, o_ref, w_ref, y_ref):
    y_ref[...] = x_ref[...] + _dot(o_ref[...].astype(BF16), w_ref[...])


def _out_proj(x, o, w):
    rows = x.shape[0]
    tm = _row_tile(rows)
    cols = o.shape[1]
    return pl.pallas_call(
        _out_proj_kernel, grid=(rows // tm,),
        in_specs=[pl.BlockSpec((tm, D_MODEL), lambda i: (i, 0)),
                  pl.BlockSpec((tm, cols), lambda i: (i, 0)),
                  pl.BlockSpec(w.shape, lambda i: (0, 0))],
        out_specs=pl.BlockSpec((tm, D_MODEL), lambda i: (i, 0)),
        out_shape=jax.ShapeDtypeStruct(x.shape, F32),
        compiler_params=_params("arbitrary"), name="mixer_out_proj")(x, o, w)


def _a_proj_kernel(x_ref, g_ref, w_ref, q0_ref, q1_ref, q2_ref, kv0_ref, kv1_ref, kv2_ref):
    h = _rms(x_ref[...], g_ref[...]).astype(BF16)
    nq = A_HEADS * HEAD_DIM
    for gi, (q_ref, kv_ref) in enumerate(((q0_ref, kv0_ref), (q1_ref, kv1_ref), (q2_ref, kv2_ref))):
        c0 = gi * A_GROUP_COLS
        q_ref[...] = _dot(h, w_ref[:, c0:c0 + A_GROUP_COLS]).astype(BF16)
        kv_ref[:, :A_GROUP_COLS] = _dot(h, w_ref[:, nq + c0:nq + c0 + A_GROUP_COLS])
        kv_ref[:, A_GROUP_COLS:] = _dot(h, w_ref[:, 2 * nq + c0:2 * nq + c0 + A_GROUP_COLS])


def _a_proj(x, g, w):
    rows = x.shape[0]
    tm = _row_tile(rows)
    qspec = pl.BlockSpec((tm, A_GROUP_COLS), lambda i: (i, 0))
    kvspec = pl.BlockSpec((tm, 2 * A_GROUP_COLS), lambda i: (i, 0))
    return pl.pallas_call(
        _a_proj_kernel, grid=(rows // tm,),
        in_specs=[pl.BlockSpec((tm, D_MODEL), lambda i: (i, 0)),
                  pl.BlockSpec((1, D_MODEL), lambda i: (0, 0)),
                  pl.BlockSpec(w.shape, lambda i: (0, 0))],
        out_specs=[qspec] * 3 + [kvspec] * 3,
        out_shape=[jax.ShapeDtypeStruct((rows, A_GROUP_COLS), BF16)] * 3
        + [jax.ShapeDtypeStruct((rows, 2 * A_GROUP_COLS), F32)] * 3,
        compiler_params=_params("arbitrary"), name="dilated_in_proj")(x, g.reshape(1, D_MODEL), w)


def _a_attn_kernel(q_ref, kvc_ref, kvp_ref, bias_ref, o_ref, l_ref):
    blk = pl.program_id(2)
    band = A_BAND
    rows = A_GROUP_HEADS * band
    q = q_ref[0]
    k = jnp.concatenate([kvp_ref[0, :, :A_GROUP_COLS], kvc_ref[0, :, :A_GROUP_COLS]], axis=0).astype(BF16)
    v = jnp.concatenate([kvp_ref[0, :, A_GROUP_COLS:], kvc_ref[0, :, A_GROUP_COLS:]], axis=0).astype(BF16)
    head_of_lane = lax.broadcasted_iota(I32, (band, A_GROUP_COLS), 1) // HEAD_DIM
    q_all = jnp.concatenate([jnp.where(head_of_lane == h, q, jnp.zeros_like(q)) for h in range(A_GROUP_HEADS)],
                            axis=0)
    qi = lax.broadcasted_iota(I32, (rows, 2 * band), 0) % band
    kj = lax.broadcasted_iota(I32, (rows, 2 * band), 1)
    off = qi + band - kj
    mask = (off >= 0) & (off <= band) & ((kj >= band) | (blk > 0))
    s = _dot_nt(q_all, k) * SCALE + bias_ref[...].reshape(rows, 2 * band)
    p, m, den = _masked_softmax(s, mask)
    out = _dot(p.astype(BF16), v) / den
    lse = jnp.broadcast_to(m + jnp.log(den), (rows, A_GROUP_COLS))
    pick = lambda x: functools.reduce(
        lambda a, b: a + b, [jnp.where(head_of_lane == h, x[h * band:(h + 1) * band], 0.0) for h in range(A_GROUP_HEADS)])
    o_ref[0] = pick(out)
    l_ref[0] = pick(lse)


def _a_attn_group(q, kv, bias, batch, seq, dil):
    sub = seq // dil
    nb = sub // A_BAND
    qv = q.reshape(batch, sub, dil * A_GROUP_COLS)
    kvv = kv.reshape(batch, sub, dil * 2 * A_GROUP_COLS)
    qspec = pl.BlockSpec((1, A_BAND, A_GROUP_COLS), lambda b, r, i: (b, i, r))
    o, lse = pl.pallas_call(
        _a_attn_kernel, grid=(batch, dil, nb),
        in_specs=[qspec,
                  pl.BlockSpec((1, A_BAND, 2 * A_GROUP_COLS), lambda b, r, i: (b, i, r)),
                  pl.BlockSpec((1, A_BAND, 2 * A_GROUP_COLS), lambda b, r, i: (b, jnp.maximum(i - 1, 0), r)),
                  pl.BlockSpec(bias.shape, lambda b, r, i: (0, 0, 0))],
        out_specs=[qspec, qspec],
        out_shape=[jax.ShapeDtypeStruct(qv.shape, F32)] * 2,
        compiler_params=_params("arbitrary", "arbitrary", "arbitrary"), name="dilated_band_attention")(
            qv, kvv, kvv, bias)
    return o.reshape(batch * seq, A_GROUP_COLS), lse.reshape(batch * seq, A_GROUP_COLS)


def _a_band_bias(table, g, dil):
    off = np.arange(A_BAND)[:, None] + A_BAND - np.arange(2 * A_BAND)[None, :]
    dist = np.clip(off, 0, A_BAND) * dil
    return _bias_of_dist(table[:, g * A_GROUP_HEADS:(g + 1) * A_GROUP_HEADS], dist)


def _a_out_kernel(x_ref, o0_ref, o1_ref, o2_ref, l0_ref, l1_ref, l2_ref, w_ref, y_ref):
    ls = [l0_ref[...], l1_ref[...], l2_ref[...]]
    m = jnp.maximum(jnp.maximum(ls[0], ls[1]), ls[2])
    es = [jnp.exp(l - m) for l in ls]
    den = es[0] + es[1] + es[2]
    y = x_ref[...]
    for gi, o_ref in enumerate((o0_ref, o1_ref, o2_ref)):
        og = (o_ref[...] * (es[gi] / den)).astype(BF16)
        y = y + _dot(og, w_ref[gi * A_GROUP_COLS:(gi + 1) * A_GROUP_COLS, :])
    y_ref[...] = y


def _a_out(x, outs, lses, w):
    rows = x.shape[0]
    tm = _row_tile(rows)
    gspec = pl.BlockSpec((tm, A_GROUP_COLS), lambda i: (i, 0))
    xspec = pl.BlockSpec((tm, D_MODEL), lambda i: (i, 0))
    return pl.pallas_call(
        _a_out_kernel, grid=(rows // tm,),
        in_specs=[xspec] + [gspec] * 6 + [pl.BlockSpec(w.shape, lambda i: (0, 0))],
        out_specs=xspec, out_shape=jax.ShapeDtypeStruct(x.shape, F32),
        compiler_params=_params("arbitrary"), name="dilated_combine_out_proj")(x, *outs, *lses, w)


def _a_step_kernel(q0_ref, q1_ref, q2_ref, n0_ref, n1_ref, n2_ref, b0_ref, b1_ref, b2_ref,
                   bias0_ref, bias1_ref, bias2_ref, self_ref, o_ref):
    groups = ((q0_ref, n0_ref, b0_ref, bias0_ref), (q1_ref, n1_ref, b1_ref, bias1_ref),
              (q2_ref, n2_ref, b2_ref, bias2_ref))
    outs, lses = [], []
    for gi, (q_ref, n_ref, b_ref, bias_ref) in enumerate(groups):
        dil = A_GROUPS[gi][1]
        win = b_ref.shape[4]
        mask = lax.broadcasted_iota(I32, (1, win), 1) % dil == 0
        q = q_ref[0].astype(F32)
        new = n_ref[0]
        for j in range(A_GROUP_HEADS):
            head = gi * A_GROUP_HEADS + j
            rows = slice(j * HEAD_DIM, (j + 1) * HEAD_DIM)
            v_rows = slice(A_GROUP_COLS + j * HEAD_DIM, A_GROUP_COLS + (j + 1) * HEAD_DIM)
            s = jnp.where(mask, _col_dot(b_ref[0, 0, j], q[rows]) * SCALE + bias_ref[j:j + 1, :], NEG_INF)
            s_self = _col_dot(new[rows], q[rows]) * SCALE + self_ref[head:head + 1, 0:1]
            m = jnp.maximum(_row_max(s), s_self)
            pb = jnp.where(mask, jnp.exp(s - m), 0.0)
            ps = jnp.exp(s_self - m)
            den = _row_sum(pb) + ps
            outs.append((_row_sum(b_ref[0, 1, j] * pb) + ps * new[v_rows]) / den)
            lses.append(m + jnp.log(den))
    n_groups = len(groups)
    for j in range(A_GROUP_HEADS):
        ls = [lses[gi * A_GROUP_HEADS + j] for gi in range(n_groups)]
        m = functools.reduce(jnp.maximum, ls)
        es = [jnp.exp(l - m) for l in ls]
        den = functools.reduce(lambda a, b: a + b, es)
        for gi in range(n_groups):
            head = gi * A_GROUP_HEADS + j
            o_ref[0, head * HEAD_DIM:(head + 1) * HEAD_DIM, :] = outs[head] * (es[gi] / den)


def _a_step_attn(qs, news, bufs, table):
    n = qs[0].shape[0]
    cols = lambda a: a.reshape(n, a.shape[1], 1)
    views = [b.transpose(0, 2, 3, 4, 1) for b in bufs]
    biases = []
    for g, (win, _) in enumerate(A_GROUPS):
        assert bufs[g].shape[1] == win, "the step kernel reads a full window buffer"
        biases.append(_bias_of_dist(table[:, g * A_GROUP_HEADS:(g + 1) * A_GROUP_HEADS], win - np.arange(win)))
    self_bias = jnp.broadcast_to(_bias_of_dist(table, np.zeros((1,), np.int64)), (A_HEADS, V7X_LANES))
    col_spec = lambda c: pl.BlockSpec((1, c, 1), lambda b: (b, 0, 0))
    whole = lambda a: pl.BlockSpec(a.shape, lambda b: (0,) * a.ndim)
    o = pl.pallas_call(
        _a_step_kernel, grid=(n,),
        in_specs=[col_spec(A_GROUP_COLS)] * 3 + [col_spec(2 * A_GROUP_COLS)] * 3
        + [pl.BlockSpec((1,) + v.shape[1:], lambda b: (b, 0, 0, 0, 0)) for v in views]
        + [whole(b) for b in biases] + [whole(self_bias)],
        out_specs=col_spec(A_HEADS * HEAD_DIM),
        out_shape=jax.ShapeDtypeStruct((n, A_HEADS * HEAD_DIM, 1), F32),
        compiler_params=_params("arbitrary"), name="dilated_step_attention")(
            *[cols(q) for q in qs], *[cols(x) for x in news], *views, *biases, self_bias)
    return o.reshape(n, A_HEADS * HEAD_DIM)


def _mixer_a(xp, xs, g, w_in, w_out, table, caches, batch, seq):
    w_in = w_in.astype(BF16)
    w_out = w_out.astype(BF16)
    pq0, pq1, pq2, pkv0, pkv1, pkv2 = _a_proj(xp, g, w_in)
    outs, lses, state = [], [], []
    for gi, ((win, dil), q, kv) in enumerate(zip(A_GROUPS, (pq0, pq1, pq2), (pkv0, pkv1, pkv2))):
        o, l = _a_attn_group(q, kv, _a_band_bias(table, gi, dil), batch, seq, dil)
        outs.append(o)
        lses.append(l)
        keep = min(win, seq)
        state.append(kv.reshape(batch, seq, 2 * A_GROUP_COLS)[:, seq - keep:].reshape(batch, keep, 2, A_GROUP_HEADS, HEAD_DIM))
    yp = _a_out(xp, outs, lses, w_out)

    n = xs.shape[0]
    sq0, sq1, sq2, skv0, skv1, skv2 = _a_proj(xs, g, w_in)
    so = _a_step_attn((sq0, sq1, sq2), (skv0, skv1, skv2), caches, table)
    ys = _out_proj(xs, so, w_out)
    for (win, _), kv, buf in zip(A_GROUPS, (skv0, skv1, skv2), caches):
        new = kv.reshape(n, 1, 2, A_GROUP_HEADS, HEAD_DIM)
        state.append(jnp.concatenate([buf, new], axis=1)[:, buf.shape[1] + 1 - min(win, buf.shape[1] + 1):])
    return yp, ys, state


C_TAIL = V7X_LANES


def _log_sigmoid(z):
    return -(jnp.maximum(-z, 0.0) + jnp.log1p(jnp.exp(-jnp.abs(z))))


def _c_proj_kernel(x_ref, g_ref, w_ref, wkv_t_ref, wt_ref, b_ref, tri_ref, q_ref, k_ref, v_ref, kt_ref, vt_ref,
                   lf_ref, dc_ref, carry_ref, *, tiles_per_seq):
    i = pl.program_id(0)
    h = _rms(x_ref[...], g_ref[...]).astype(BF16)
    q_ref[...] = _dot(h, w_ref[:, 0:C_COLS]).astype(BF16)
    k_ref[...] = _dot(h, w_ref[:, C_COLS:2 * C_COLS]).astype(BF16)
    v_ref[...] = _dot(h, w_ref[:, 2 * C_COLS:3 * C_COLS]).astype(BF16)
    kt_ref[0] = _dot_nt(wkv_t_ref[0:C_COLS, :], h)
    vt_ref[0] = _dot_nt(wkv_t_ref[C_COLS:2 * C_COLS, :], h)
    logf = _log_sigmoid(_dot(h, wt_ref[...]) + b_ref[...])
    lf_ref[...] = logf

    @pl.when(i % tiles_per_seq == 0)
    def _():
        carry_ref[...] = jnp.zeros_like(carry_ref)

    cum = _dot3(tri_ref[...], logf) + carry_ref[0:1, :]
    dc_ref[...] = cum
    carry_ref[...] = jnp.broadcast_to(cum[cum.shape[0] - 1:, :], carry_ref.shape)


def _c_proj(x, g, w, w_tail, bias, batch, seq):
    rows = x.shape[0]
    tm = _row_tile(rows) if seq > 1 else rows
    tiles_per_seq = max(seq // tm, 1)
    seqs_per_tile = max(tm // seq, 1)
    tri = jnp.asarray(np.tril(np.ones((tm, tm), np.float32)), BF16)
    wkv_t = w[:, C_COLS:].T
    big = pl.BlockSpec((tm, C_COLS), lambda i: (i, 0))
    small = pl.BlockSpec((tm, C_TAIL), lambda i: (i, 0))
    t_shape = (batch // seqs_per_tile, C_COLS, seq * seqs_per_tile)
    t_spec = pl.BlockSpec((1, C_COLS, tm), lambda i: (i // tiles_per_seq, 0, i % tiles_per_seq))
    return pl.pallas_call(
        functools.partial(_c_proj_kernel, tiles_per_seq=tiles_per_seq), grid=(rows // tm,),
        in_specs=[pl.BlockSpec((tm, D_MODEL), lambda i: (i, 0)),
                  pl.BlockSpec((1, D_MODEL), lambda i: (0, 0)),
                  pl.BlockSpec(w.shape, lambda i: (0, 0)),
                  pl.BlockSpec(wkv_t.shape, lambda i: (0, 0)),
                  pl.BlockSpec(w_tail.shape, lambda i: (0, 0)),
                  pl.BlockSpec((1, C_TAIL), lambda i: (0, 0)),
                  pl.BlockSpec(tri.shape, lambda i: (0, 0))],
        out_specs=[big, big, big, t_spec, t_spec, small, small],
        out_shape=[jax.ShapeDtypeStruct((rows, C_COLS), BF16)] * 3 + [jax.ShapeDtypeStruct(t_shape, F32)] * 2
        + [jax.ShapeDtypeStruct((rows, C_TAIL), F32)] * 2,
        scratch_shapes=[pltpu.VMEM((8, C_TAIL), F32)],
        compiler_params=_params("arbitrary"), name="forget_in_proj")(
            x, g.reshape(1, D_MODEL), w, wkv_t, w_tail, bias, tri)


C_TILE = 512


def _c_attn_kernel(q_ref, k_ref, v_ref, dq_ref, dk_ref, o_ref, m_ref, l_ref, acc_ref):
    hp = pl.program_id(1)
    i = pl.program_id(2)
    t = C_TILE
    q = q_ref[0]
    dq_tile = dq_ref[0]
    lane = lax.broadcasted_iota(I32, (t, V7X_LANES), 1)
    lo = lane < HALF
    qm = [_pick_head(q, 0), _pick_head(q, 1)]
    dq = [jnp.sum(jnp.where(lane == 2 * hp + half, dq_tile, 0.0), axis=1, keepdims=True) for half in range(2)]
    causal = lax.broadcasted_iota(I32, (t, t), 0) >= lax.broadcasted_iota(I32, (t, t), 1)
    m_ref[...] = jnp.full_like(m_ref, NEG_INF)
    l_ref[...] = jnp.zeros_like(l_ref)
    acc_ref[...] = jnp.zeros_like(acc_ref)

    def tile(c, diagonal):
        start = pl.multiple_of(c * t, t)
        k = k_ref[0, pl.ds(start, t), :]
        v = v_ref[0, pl.ds(start, t), :]
        dk = dk_ref[0, 0, c]
        for half in range(2):
            s = _dot_nt(qm[half], k) * SCALE + dq[half] - dk[half:half + 1, :]
            if diagonal:
                s = jnp.where(causal, s, NEG_INF)
            m_old = m_ref[half]
            m_new = jnp.maximum(m_old, jnp.max(s, axis=1, keepdims=True))
            p = jnp.exp(s - m_new[:, 0:1])
            if diagonal:
                p = jnp.where(causal, p, 0.0)
            alpha = jnp.exp(m_old - m_new)
            l_ref[half] = alpha * l_ref[half] + jnp.sum(p, axis=1, keepdims=True)
            acc_ref[half] = alpha * acc_ref[half] + _dot(p.astype(BF16), v)
            m_ref[half] = m_new

    def body(c, carry):
        tile(c, False)
        return carry

    lax.fori_loop(0, i, body, 0)
    tile(i, True)
    o_ref[0] = jnp.where(lo, acc_ref[0] / l_ref[0], acc_ref[1] / l_ref[1])


def _c_attn(q, k, v, dcum, batch, seq):
    t = C_TILE
    nt = seq // t
    pairs = C_HEADS // 2
    q3, k3, v3 = (a.reshape(batch, seq, C_COLS) for a in (q, k, v))
    dq = dcum.reshape(batch, seq, C_TAIL)
    dk = dq[:, :, :C_HEADS].transpose(0, 2, 1).reshape(batch, pairs, 2, nt, t).transpose(0, 1, 3, 2, 4)
    qspec = pl.BlockSpec((1, t, V7X_LANES), lambda b, h, i: (b, i, h))
    kspec = pl.BlockSpec((1, seq, V7X_LANES), lambda b, h, i: (b, 0, h))
    o = pl.pallas_call(
        _c_attn_kernel, grid=(batch, pairs, nt),
        in_specs=[qspec, kspec, kspec,
                  pl.BlockSpec((1, t, C_TAIL), lambda b, h, i: (b, i, 0)),
                  pl.BlockSpec((1, 1, nt, 2, t), lambda b, h, i: (b, h, 0, 0, 0))],
        out_specs=qspec, out_shape=jax.ShapeDtypeStruct((batch, seq, C_COLS), F32),
        scratch_shapes=[pltpu.VMEM((2, t, V7X_LANES), F32)] * 3,
        compiler_params=_params("arbitrary", "arbitrary", "arbitrary"), name="forget_attention")(
            q3, k3, v3, dq, dk)
    return o.reshape(batch * seq, C_COLS)


C_STEP_PAGES = 8


def _c_step_kernel(pt_ref, q_ref, kn_ref, vn_ref, lfn_ref, *rest):
    page_refs = rest[:3 * C_STEP_PAGES]
    lower_ref, o_ref, qb_ref, m_ref, l_ref, acc_ref, carry_ref = rest[3 * C_STEP_PAGES:]
    j = pl.program_id(1)
    page = qb_ref.shape[1]
    heads = (C_HEADS, HEAD_DIM)

    def per_head(x):
        return jnp.broadcast_to(x[:, None, :], heads + (x.shape[1],)).reshape(C_COLS, x.shape[1])

    @pl.when(j == 0)
    def _():
        q = q_ref[0].astype(F32)
        lane0 = lax.broadcasted_iota(I32, (C_COLS, page), 1) == 0
        qb_ref[...] = jnp.broadcast_to(q, (C_COLS, page))
        s_self = jnp.sum((kn_ref[0] * q).reshape(heads + (1,)), axis=1) * SCALE
        m_ref[...] = jnp.broadcast_to(s_self, (C_HEADS, page))
        l_ref[...] = jnp.where(lax.broadcasted_iota(I32, (C_HEADS, page), 1) == 0, 1.0, 0.0)
        acc_ref[...] = jnp.where(lane0, jnp.broadcast_to(vn_ref[0], (C_COLS, page)), 0.0)
        carry_ref[...] = jnp.broadcast_to(lfn_ref[0], (C_HEADS, page))

    for pg in range(C_STEP_PAGES):
        k_ref, v_ref, lf_ref = page_refs[3 * pg:3 * pg + 3]
        lf = lf_ref[0]
        hi, mid, lo3 = _split3(lf)
        decay = _dot(hi, lower_ref[...]) + _dot(mid, lower_ref[...]) + _dot(lo3, lower_ref[...]) + carry_ref[...]
        s = jnp.sum((k_ref[0] * qb_ref[...]).reshape(heads + (page,)), axis=1) * SCALE + decay
        m_old = m_ref[...]
        m_new = jnp.maximum(m_old, _row_max(s))
        p = jnp.exp(s - m_new)
        alpha = jnp.exp(m_old - m_new)
        l_ref[...] = alpha * l_ref[...] + p
        acc_ref[...] = per_head(alpha) * acc_ref[...] + per_head(p) * v_ref[0]
        m_ref[...] = m_new
        carry_ref[...] = carry_ref[...] + _row_sum(lf)

    @pl.when(j == pl.num_programs(1) - 1)
    def _():
        o_ref[0] = _row_sum(acc_ref[...]) / per_head(_row_sum(l_ref[...]))


def _c_step_attn(q, k_new, v_new, lf_new, cache_k, cache_v, cache_logf, page_table):
    n, n_pages = page_table.shape
    page = cache_k.shape[1]
    assert n_pages % C_STEP_PAGES == 0
    lower = jnp.asarray(np.tril(np.ones((page, page), np.float32), -1), BF16)
    col = lambda c: pl.BlockSpec((1, c, 1), lambda b, j, pt: (b, 0, 0))

    def paged(rows, pg):
        return pl.BlockSpec((1, rows, page), lambda b, j, pt: (pt[b, n_pages - 1 - (j * C_STEP_PAGES + pg)], 0, 0))

    k_view = cache_k.transpose(0, 2, 3, 1).reshape(-1, C_COLS, page)
    v_view = cache_v.transpose(0, 2, 3, 1).reshape(-1, C_COLS, page)
    lf_view = cache_logf.transpose(0, 2, 1)
    page_specs, page_args = [], []
    for pg in range(C_STEP_PAGES):
        page_specs += [paged(C_COLS, pg), paged(C_COLS, pg), paged(C_HEADS, pg)]
        page_args += [k_view, v_view, lf_view]
    o = pl.pallas_call(
        _c_step_kernel,
        grid_spec=pltpu.PrefetchScalarGridSpec(
            num_scalar_prefetch=1, grid=(n, n_pages // C_STEP_PAGES),
            in_specs=[col(C_COLS), col(C_COLS), col(C_COLS), col(C_HEADS)] + page_specs
            + [pl.BlockSpec(lower.shape, lambda b, j, pt: (0, 0))],
            out_specs=col(C_COLS),
            scratch_shapes=[pltpu.VMEM((C_COLS, page), F32), pltpu.VMEM((C_HEADS, page), F32),
                            pltpu.VMEM((C_HEADS, page), F32), pltpu.VMEM((C_COLS, page), F32),
                            pltpu.VMEM((C_HEADS, page), F32)]),
        out_shape=jax.ShapeDtypeStruct((n, C_COLS, 1), F32),
        compiler_params=_params("arbitrary", "arbitrary"), name="forget_paged_step")(
            page_table, q.reshape(n, C_COLS, 1), k_new.reshape(n, C_COLS, 1), v_new.reshape(n, C_COLS, 1),
            lf_new[:, :C_HEADS].reshape(n, C_HEADS, 1), *page_args, lower)
    return o.reshape(n, C_COLS)


def _mixer_c(xp, xs, g, w_in, forget_bias, w_out, caches, page_table, batch, seq):
    w = w_in[:, :3 * C_COLS].astype(BF16)
    w_tail = jnp.pad(w_in[:, 3 * C_COLS:], ((0, 0), (0, C_TAIL - C_HEADS))).astype(BF16)
    bias = jnp.pad(forget_bias, (0, C_TAIL - C_HEADS)).reshape(1, C_TAIL)
    w_out = w_out.astype(BF16)
    cache_k, cache_v, cache_logf = caches

    q, k, v, kt, vt, logf, dcum = _c_proj(xp, g, w, w_tail, bias, batch, seq)
    o = _c_attn(q, k, v, dcum, batch, seq)
    yp = _out_proj(xp, o, w_out)
    heads_last = lambda a: a.reshape(a.shape[0], C_HEADS, HEAD_DIM, a.shape[2]).transpose(0, 3, 1, 2)
    state = [heads_last(kt), heads_last(vt), logf[:, :C_HEADS].reshape(batch, seq, C_HEADS)]

    n = xs.shape[0]
    sq, sk, sv, skt, svt, slogf, _ = _c_proj(xs, g, w, w_tail, bias, n, 1)
    k_new, v_new = skt[0].T, svt[0].T
    so = _c_step_attn(sq, k_new, v_new, slogf, cache_k, cache_v, cache_logf, page_table)
    ys = _out_proj(xs, so, w_out)
    state += [k_new.reshape(n, 1, C_HEADS, HEAD_DIM), v_new.reshape(n, 1, C_HEADS, HEAD_DIM),
              slogf[:, :C_HEADS].reshape(n, 1, C_HEADS)]
    return yp, ys, state


B_TAIL = V7X_LANES
B_SLAB = 2 * B_KV_COLS


def _b_proj_kernel(x_ref, g_ref, w_ref, wt_ref, b_ref, q_ref, cmp_ref, sel_ref, win_ref, gate_ref):
    h = _rms(x_ref[...], g_ref[...]).astype(BF16)
    q_ref[...] = _dot(h, w_ref[:, 0:B_Q_COLS]).astype(BF16)
    c0 = B_Q_COLS
    cmp_ref[...] = _dot(h, w_ref[:, c0:c0 + B_SLAB])
    sel_ref[...] = _dot(h, w_ref[:, c0 + B_SLAB:c0 + 2 * B_SLAB])
    win_ref[...] = _dot(h, w_ref[:, c0 + 2 * B_SLAB:c0 + 3 * B_SLAB])
    gate_ref[...] = jax.nn.sigmoid(_dot(h, wt_ref[...]) + b_ref[...])


def _b_proj(x, g, w, w_tail, bias):
    rows = x.shape[0]
    tm = _row_tile(rows)
    spec = lambda cols: pl.BlockSpec((tm, cols), lambda i: (i, 0))
    shape = lambda cols, dt: jax.ShapeDtypeStruct((rows, cols), dt)
    return pl.pallas_call(
        _b_proj_kernel, grid=(rows // tm,),
        in_specs=[spec(D_MODEL), pl.BlockSpec((1, D_MODEL), lambda i: (0, 0)),
                  pl.BlockSpec(w.shape, lambda i: (0, 0)), pl.BlockSpec(w_tail.shape, lambda i: (0, 0)),
                  pl.BlockSpec((1, B_TAIL), lambda i: (0, 0))],
        out_specs=[spec(B_Q_COLS)] + [spec(B_SLAB)] * 3 + [spec(B_TAIL)],
        out_shape=[shape(B_Q_COLS, BF16)] + [shape(B_SLAB, F32)] * 3 + [shape(B_TAIL, F32)],
        compiler_params=_params("arbitrary"), name="sparse_in_proj_step")(x, g.reshape(1, D_MODEL), w, w_tail, bias)


B_GROUP_Q = 2 * V7X_LANES
B_GQ_COLS = B_KV_HEADS * B_GROUP_Q
B_GKV_COLS = B_KV_HEADS * V7X_LANES
B_GGATE_COLS = B_KV_HEADS * V7X_LANES


def _b_proj_grouped_kernel(x_ref, g_ref, w_ref, wst_ref, wt_ref, b_ref, q_ref, cmp_ref, cmpt_ref, selt_ref, wint_ref,
                           selg_ref, wing_ref, gate_ref):
    h = _rms(x_ref[...], g_ref[...]).astype(BF16)
    c = 0
    q_ref[...] = _dot(h, w_ref[:, c:c + B_GQ_COLS]).astype(BF16)
    c += B_GQ_COLS
    cmp_ref[...] = _dot(h, w_ref[:, c:c + B_SLAB]).astype(BF16)
    c += B_SLAB
    for ref in (selg_ref, wing_ref):
        ref[...] = _dot(h, w_ref[:, c:c + B_GKV_COLS]).astype(BF16)
        c += B_GKV_COLS
    for idx, ref in enumerate((cmpt_ref, selt_ref, wint_ref)):
        ref[0] = _dot_nt(wst_ref[idx * B_SLAB:(idx + 1) * B_SLAB, :], h)
    gate_ref[...] = jax.nn.sigmoid(_dot(h, wt_ref[...]) + b_ref[...])


def _b_grouped_weights(w_in, gate_bias):
    q = w_in[:, :B_Q_COLS].reshape(D_MODEL, B_KV_HEADS, B_REP * HEAD_DIM)
    q = jnp.pad(q, ((0, 0), (0, 0), (0, B_GROUP_Q - B_REP * HEAD_DIM))).reshape(D_MODEL, B_GQ_COLS)
    main = w_in[:, B_Q_COLS:B_Q_COLS + 3 * B_SLAB]

    def grouped(slab):
        kv = slab.reshape(D_MODEL, 2, B_KV_HEADS, HEAD_DIM)
        return kv.transpose(0, 2, 1, 3).reshape(D_MODEL, B_GKV_COLS)

    w = jnp.concatenate([q, main[:, :B_SLAB], grouped(main[:, B_SLAB:2 * B_SLAB]), grouped(main[:, 2 * B_SLAB:])],
                        axis=1)

    def gates(a):
        a = a.reshape(a.shape[:-1] + (3, B_KV_HEADS, B_REP))
        a = jnp.moveaxis(a, -2, -3).reshape(a.shape[:-3] + (B_KV_HEADS, 3 * B_REP))
        pad = [(0, 0)] * (a.ndim - 1) + [(0, V7X_LANES - 3 * B_REP)]
        return jnp.pad(a, pad).reshape(a.shape[:-2] + (B_GGATE_COLS,))

    tail = w_in[:, B_Q_COLS + 3 * B_SLAB:]
    return w.astype(BF16), main.T.astype(BF16), gates(tail).astype(BF16), gates(gate_bias).reshape(1, B_GGATE_COLS)


def _b_proj_grouped(x, g, w, w_state_t, w_tail, bias, batch, seq):
    rows = x.shape[0]
    tm = _row_tile(rows)
    tiles_per_seq = seq // tm
    spec = lambda cols: pl.BlockSpec((tm, cols), lambda i: (i, 0))
    shape = lambda cols, dt: jax.ShapeDtypeStruct((rows, cols), dt)
    t_spec = pl.BlockSpec((1, B_SLAB, tm), lambda i: (i // tiles_per_seq, 0, i % tiles_per_seq))
    t_shape = jax.ShapeDtypeStruct((batch, B_SLAB, seq), F32)
    return pl.pallas_call(
        _b_proj_grouped_kernel, grid=(rows // tm,),
        in_specs=[spec(D_MODEL), pl.BlockSpec((1, D_MODEL), lambda i: (0, 0)),
                  pl.BlockSpec(w.shape, lambda i: (0, 0)), pl.BlockSpec(w_state_t.shape, lambda i: (0, 0)),
                  pl.BlockSpec(w_tail.shape, lambda i: (0, 0)), pl.BlockSpec((1, B_GGATE_COLS), lambda i: (0, 0))],
        out_specs=[spec(B_GQ_COLS), spec(B_SLAB)] + [t_spec] * 3 + [spec(B_GKV_COLS)] * 2 + [spec(B_GGATE_COLS)],
        out_shape=[shape(B_GQ_COLS, BF16), shape(B_SLAB, BF16)] + [t_shape] * 3 + [shape(B_GKV_COLS, BF16)] * 2
        + [shape(B_GGATE_COLS, F32)],
        compiler_params=_params("arbitrary"), name="sparse_in_proj_prompt")(
            x, g.reshape(1, D_MODEL), w, w_state_t, w_tail, bias)


B_PHI_ROWS = 2 * B_KV_HEADS * B_PHI_HIDDEN
B_CHUNK_COLS = B_CMP_STRIDE * B_SLAB


def _compress_weights(phi_pos, phi_w1, phi_w2, pad_heads):
    half = B_CMP_BLOCK // B_CMP_STRIDE
    eye = jnp.eye(B_KV_HEADS, dtype=F32)
    w1r = phi_w1.reshape(2, half, B_CMP_STRIDE, HEAD_DIM, B_PHI_HIDDEN)
    first = jnp.einsum('cpsde,gh->cspgehd', w1r, eye).reshape(2, B_CMP_STRIDE, B_PHI_ROWS, B_KV_COLS)
    pos_term = jnp.einsum('cpsd,cpsde->cpe', phi_pos.reshape(2, half, B_CMP_STRIDE, HEAD_DIM), w1r,
                          precision=lax.Precision.HIGHEST)
    pos = jnp.broadcast_to(pos_term[:, :, None, :, None], (2, half, B_KV_HEADS, B_PHI_HIDDEN, V7X_LANES))
    second = jnp.einsum('ced,gh->cgdhe', phi_w2, eye)
    if pad_heads:
        second = jnp.pad(second, ((0, 0), (0, 0), (0, V7X_LANES - HEAD_DIM), (0, 0), (0, 0)))
    second = second.reshape(2, -1, B_KV_HEADS * B_PHI_HIDDEN)
    return first.astype(BF16), pos.reshape(2, B_PHI_ROWS, V7X_LANES), second.astype(BF16)


def _compress_first(load, first_ref, c):
    acc = None
    for s in range(B_CMP_STRIDE):
        part = _dot_nt(first_ref[c, s], load(s, c).astype(BF16))
        acc = part if acc is None else acc + part
    return acc


def _compress_second(acc, pos_ref, second_ref, c):
    n_chunks = acc.shape[1]
    acc = acc + pos_ref[c][:, 0:1]
    rows = B_PHI_ROWS // 2
    pre = acc[:rows] + pltpu.roll(acc[rows:], n_chunks - 1, 1)
    return _dot(second_ref[c], jax.nn.gelu(pre).astype(BF16))


def _b_compress_prompt_kernel(x_ref, first_ref, pos_ref, second_ref, o_ref):
    def load(s, c):
        c0 = s * B_SLAB + c * B_KV_COLS
        return x_ref[0, :, c0:c0 + B_KV_COLS]

    for c in range(2):
        o_ref[0, c] = _compress_second(_compress_first(load, first_ref, c), pos_ref, second_ref, c).astype(BF16)


def _b_compress_prompt(cmp, weights, batch, seq):
    n_chunks = seq // B_CMP_STRIDE
    first, pos, second = weights
    out_rows = second.shape[1]
    whole = lambda a: pl.BlockSpec(a.shape, lambda b: (0,) * a.ndim)
    return pl.pallas_call(
        _b_compress_prompt_kernel, grid=(batch,),
        in_specs=[pl.BlockSpec((1, n_chunks, B_CHUNK_COLS), lambda b: (b, 0, 0)),
                  whole(first), whole(pos), whole(second)],
        out_specs=pl.BlockSpec((1, 2, out_rows, n_chunks), lambda b: (b, 0, 0, 0)),
        out_shape=jax.ShapeDtypeStruct((batch, 2, out_rows, n_chunks), BF16),
        compiler_params=_params("arbitrary"), name="sparse_compress_prompt")(
            cmp.reshape(batch, n_chunks, B_CHUNK_COLS), first, pos, second).reshape(
                batch, 2, B_KV_HEADS, V7X_LANES, n_chunks)


def _b_compress_step_kernel(pt_ref, cache_ref, first_ref, pos_ref, second_ref, o_ref, buf_ref, acc_ref, sem_ref,
                            *, pages_per_half, chunks_per_page):
    n = pl.program_id(0)
    hf = pl.program_id(1)
    step = n * 2 + hf
    total = pl.num_programs(0) * 2
    slot = step % 2
    half_chunks = pages_per_half * chunks_per_page

    def copies(st, sl):
        nn = st // 2
        hh = st % 2
        return [pltpu.make_async_copy(cache_ref.at[pt_ref[nn, hh * pages_per_half + j]],
                                      buf_ref.at[sl, pl.ds(j * chunks_per_page, chunks_per_page), :],
                                      sem_ref.at[sl]) for j in range(pages_per_half)]

    @pl.when(step == 0)
    def _():
        for cp in copies(0, 0):
            cp.start()

    @pl.when(step + 1 < total)
    def _():
        for cp in copies(step + 1, 1 - slot):
            cp.start()

    for cp in copies(step, slot):
        cp.wait()

    def load(s, c):
        c0 = s * B_SLAB + c * B_KV_COLS
        return buf_ref[slot, :, c0:c0 + B_KV_COLS]

    for c in range(2):
        part = _compress_first(load, first_ref, c)

        @pl.when(hf == 0)
        def _():
            acc_ref[c, :, 0:half_chunks] = part

        @pl.when(hf == 1)
        def _():
            acc_ref[c, :, half_chunks:2 * half_chunks] = part

    @pl.when(hf == 1)
    def _():
        for c in range(2):
            o_ref[0, c] = _compress_second(acc_ref[c], pos_ref, second_ref, c).astype(BF16)


def _b_compress_step(cache_cmp, page_table, weights):
    n, n_pages = page_table.shape
    page = cache_cmp.shape[1]
    chunks_per_page = page // B_CMP_STRIDE
    n_chunks = n_pages * chunks_per_page
    pages_per_half = n_pages // 2
    first, pos, second = weights
    whole = lambda a: pl.BlockSpec(a.shape, lambda b, h, pt: (0,) * a.ndim)
    return pl.pallas_call(
        functools.partial(_b_compress_step_kernel, pages_per_half=pages_per_half, chunks_per_page=chunks_per_page),
        grid_spec=pltpu.PrefetchScalarGridSpec(
            num_scalar_prefetch=1, grid=(n, 2),
            in_specs=[pl.BlockSpec(memory_space=pl.ANY), whole(first), whole(pos), whole(second)],
            out_specs=pl.BlockSpec((1, 2, B_KV_COLS, n_chunks), lambda b, h, pt: (b, 0, 0, 0)),
            scratch_shapes=[pltpu.VMEM((2, n_chunks // 2, B_CHUNK_COLS), F32),
                            pltpu.VMEM((2, B_PHI_ROWS, n_chunks), F32),
                            pltpu.SemaphoreType.DMA((2,))]),
        out_shape=jax.ShapeDtypeStruct((n, 2, B_KV_COLS, n_chunks), BF16),
        compiler_params=_params("arbitrary", "arbitrary"), name="sparse_compress_step")(
            page_table, cache_cmp.reshape(-1, chunks_per_page, B_CHUNK_COLS), first, pos, second)


B_TILE = 128
B_FAR_TILES = REL_MAX_DIST // B_TILE + 2


def _toeplitz_tiles(table):
    shape = (B_FAR_TILES + 1, B_TILE, B_TILE)
    dist = (B_TILE * (lax.broadcasted_iota(I32, shape, 0) - 1) + lax.broadcasted_iota(I32, shape, 1)
            - lax.broadcasted_iota(I32, shape, 2))
    return jnp.where(dist[None] >= 0, _bias_of_dist(table, dist), NEG_INF)


def _cmp_bias_tiles(table, seq, n_chunks):
    shape = (seq // B_TILE, B_TILE, n_chunks)
    dist = (B_TILE * lax.broadcasted_iota(I32, shape, 0) + lax.broadcasted_iota(I32, shape, 1)
            - B_CMP_STRIDE * lax.broadcasted_iota(I32, shape, 2) - (B_CMP_BLOCK - 1))
    return _bias_of_dist(table, dist, head_axis=1)


def _overlap_t(n_sel_padded, n_cmp_padded, n_sel, n_cmp):
    c_start = np.arange(n_cmp_padded)[None, :] * B_CMP_STRIDE
    s_start = np.arange(n_sel_padded)[:, None] * B_SEL_BLOCK
    ov = (c_start < s_start + B_SEL_BLOCK) & (c_start + B_CMP_BLOCK > s_start)
    ov &= (np.arange(n_cmp_padded)[None, :] < n_cmp) & (np.arange(n_sel_padded)[:, None] < n_sel)
    return ov.astype(np.float32)


B_SEL_CHUNK = 1024
B_SEL_PAD = 128
B_PAD_SCORE = -3e30


def _b_attn_kernel(q_ref, gate_ref, kvt_ref, tcmp_ref, sel_ref, win_ref, ttab_ref, ovt_ref, o_ref,
                   l_ref, acc_ref, imp_ref, *, n_sel):
    g = pl.program_id(1)
    i = pl.program_id(2)
    t = B_TILE
    rows = B_REP * t
    ch = B_SEL_CHUNK
    tiles_per_chunk = ch // t
    q = q_ref[0]
    gate = gate_ref[0]
    n_cmp_pad = kvt_ref.shape[4]
    qpos_c = i * t + lax.broadcasted_iota(I32, (rows, n_cmp_pad), 0) % t
    cidx = lax.broadcasted_iota(I32, (rows, n_cmp_pad), 1)
    mask_c = qpos_c >= cidx * B_CMP_STRIDE + (B_CMP_BLOCK - 1)
    blk = lax.broadcasted_iota(I32, (n_sel, t), 0)
    cur = (i * t + lax.broadcasted_iota(I32, (n_sel, t), 1)) // B_SEL_BLOCK
    forced = (blk == 0) | (blk == cur) | (blk == cur - 1)
    kb = lax.broadcasted_iota(I32, (ch, B_SEL_PAD), 1)
    kblk = lax.broadcasted_iota(I32, (ch, B_SEL_PAD), 0) // B_SEL_BLOCK
    n_win_tiles = B_WINDOW // t + 1
    wq = lax.broadcasted_iota(I32, (rows, n_win_tiles * t), 0) % t
    wk = lax.broadcasted_iota(I32, (rows, n_win_tiles * t), 1)
    mask_w = (B_WINDOW + wq - wk < B_WINDOW) & (wk // t >= n_win_tiles - 1 - i)
    n_chunks = (i + tiles_per_chunk) // tiles_per_chunk
    heads = lambda x: [x[r * t:(r + 1) * t] for r in range(B_REP)]
    q_all = jnp.concatenate(
        [_align_head(_pick_head(q[:, (r // 2) * V7X_LANES:(r // 2 + 1) * V7X_LANES], r % 2), r % 2, 0)
         for r in range(B_REP)], axis=0)

    def bias_rows(tiles):
        return jnp.concatenate([jnp.concatenate([ttab_ref[g * B_REP + r, d] for d in tiles], axis=1)
                                for r in range(B_REP)], axis=0)

    s = _dot(q_all, kvt_ref[0, 0, 0]) * SCALE + tcmp_ref[0].reshape(rows, n_cmp_pad)
    p, _, den = _masked_softmax(s, mask_c)
    p = p / den
    o_cmp = heads(_dot_nt(p.astype(BF16), kvt_ref[0, 1, 0]))
    p_sum = functools.reduce(lambda a, b: a + b, heads(p))

    hi, mid, lo3 = _split3(p_sum)
    imp = _dot_nt(ovt_ref[...], hi) + _dot_nt(ovt_ref[...], mid) + _dot_nt(ovt_ref[...], lo3)
    imp = jnp.where(forced, B_FORCE, jnp.where(blk <= cur, imp, NEG_INF))
    imp_ref[...] = imp

    def rank_body(j, rank):
        row = imp_ref[pl.ds(j, 1), :]
        ahead = (row > imp) | ((row == imp) & (j < blk))
        return rank + jnp.where(ahead, 1.0, 0.0)

    n_visible = jnp.minimum(n_sel, (i * t + t - 1) // B_SEL_BLOCK + 1)
    rank = lax.fori_loop(0, n_visible, rank_body, jnp.zeros((n_sel, t), F32))

    penalty = jnp.where(rank < B_TOPN, 0.0, NEG_INF)
    if n_sel < B_SEL_PAD:
        penalty = jnp.concatenate([penalty, jnp.zeros((B_SEL_PAD - n_sel, t), F32)], axis=0)
    penalty = jnp.transpose(penalty).astype(BF16)
    lhs = jnp.concatenate([q_all * SCALE, jnp.concatenate([penalty] * B_REP, axis=0)], axis=1)

    def scores(c):
        start = pl.multiple_of(c * ch, ch)
        kv = sel_ref[0, pl.ds(start, ch), :]
        onehot = jnp.where(kb == kblk + c * (ch // B_SEL_BLOCK), 1.0, 0.0).astype(BF16)
        tiles = [jnp.clip(i - c * tiles_per_chunk - j, -1, B_FAR_TILES - 1) + 1 for j in range(tiles_per_chunk)]
        return _dot_nt(lhs, jnp.concatenate([kv, onehot], axis=1)) + bias_rows(tiles), kv

    def lane_fold(x, op):
        return functools.reduce(op, [x[:, k * V7X_LANES:(k + 1) * V7X_LANES] for k in range(ch // V7X_LANES)])

    m_lanes = lax.fori_loop(0, n_chunks, lambda c, m: jnp.maximum(m, lane_fold(scores(c)[0], jnp.maximum)),
                            jnp.full((rows, V7X_LANES), NEG_INF, F32))
    m_rows = _row_max(m_lanes)

    l_ref[...] = jnp.zeros_like(l_ref)
    acc_ref[...] = jnp.zeros_like(acc_ref)

    def sum_body(c, carry):
        s, kv = scores(c)
        p = jnp.exp(s - m_rows)
        l_ref[...] = l_ref[...] + lane_fold(p, lambda a, b: a + b)
        acc_ref[...] = acc_ref[...] + _dot(p.astype(BF16), kv)
        return carry

    lax.fori_loop(0, n_chunks, sum_body, 0)
    o_sel = heads(acc_ref[...] / _row_sum(l_ref[...]))

    starts = [pl.multiple_of(jnp.maximum(i - (n_win_tiles - 1) + j, 0) * t, t) for j in range(n_win_tiles)]
    kvw = jnp.concatenate([win_ref[0, pl.ds(st, t), :] for st in starts], axis=0)
    s = _dot_nt(q_all, kvw) * SCALE + bias_rows([n_win_tiles - j for j in range(n_win_tiles)])
    p, _, den = _masked_softmax(s, mask_w)
    o_win = heads(_dot(p.astype(BF16), kvw) / den)
    slabs = [jnp.zeros((t, V7X_LANES), F32) for _ in range(B_GROUP_Q // V7X_LANES)]
    for r in range(B_REP):
        o_v = gate[:, B_REP + r:B_REP + r + 1] * o_sel[r] + gate[:, 2 * B_REP + r:2 * B_REP + r + 1] * o_win[r]
        o = gate[:, r:r + 1] * _pick_head(o_cmp[r], 0) + _align_head(_pick_head(o_v, 1), 1, 0)
        slabs[r // 2] = slabs[r // 2] + _align_head(o, 0, r % 2)
    for sidx, slab in enumerate(slabs):
        o_ref[0, :, sidx * V7X_LANES:(sidx + 1) * V7X_LANES] = slab


def _b_attn(q, gate, kvt, selg, wing, table, batch, seq):
    t = B_TILE
    n_chunks = seq // B_CMP_STRIDE
    n_sel = seq // B_SEL_BLOCK
    assert n_sel <= B_SEL_PAD and n_sel % 8 == 0 and seq % B_SEL_CHUNK == 0
    tcmp = _cmp_bias_tiles(table, seq, n_chunks)
    ttab = _toeplitz_tiles(table)
    ovt = jnp.asarray(_overlap_t(n_sel, n_chunks, n_sel, n_chunks - 1), BF16)
    whole = lambda a: pl.BlockSpec(a.shape, lambda b, g, i: (0,) * a.ndim)
    seq_block = pl.BlockSpec((1, seq, V7X_LANES), lambda b, g, i: (b, 0, g))
    o = pl.pallas_call(
        functools.partial(_b_attn_kernel, n_sel=n_sel), grid=(batch, B_KV_HEADS, seq // t),
        in_specs=[pl.BlockSpec((1, t, B_GROUP_Q), lambda b, g, i: (b, i, g)),
                  pl.BlockSpec((1, t, V7X_LANES), lambda b, g, i: (b, i, g)),
                  pl.BlockSpec((1, 2, 1, V7X_LANES, n_chunks), lambda b, g, i: (b, 0, g, 0, 0)),
                  pl.BlockSpec((1, B_REP, t, n_chunks), lambda b, g, i: (i, g, 0, 0)),
                  seq_block, seq_block, whole(ttab), whole(ovt)],
        out_specs=pl.BlockSpec((1, t, B_GROUP_Q), lambda b, g, i: (b, i, g)),
        out_shape=jax.ShapeDtypeStruct((batch, seq, B_GQ_COLS), F32),
        scratch_shapes=[pltpu.VMEM((B_REP * t, V7X_LANES), F32)] * 2 + [pltpu.VMEM((n_sel, t), F32)],
        compiler_params=_params("arbitrary", "arbitrary", "arbitrary"), name="sparse_prompt_attention")(
            q.reshape(batch, seq, B_GQ_COLS), gate.reshape(batch, seq, B_GGATE_COLS), kvt, tcmp,
            selg.reshape(batch, seq, B_GKV_COLS), wing.reshape(batch, seq, B_GKV_COLS), ttab, ovt)
    return o.reshape(batch * seq, B_GQ_COLS)


B_STEP_SEL_PAD = 256
B_STEP_ROWS = 8
B_STEP_SLOTS = 4


def _b_step_cmp_kernel(q_ref, kvt_ref, bias_ref, ov_ref, oc_ref, ids_ref, *, n_cmp, n_sel, cur):
    pad = B_STEP_SEL_PAD
    n_cmp_pad = kvt_ref.shape[3]
    q = q_ref[0].astype(F32)
    mask_c = lax.broadcasted_iota(I32, (1, n_cmp_pad), 1) < n_cmp
    blk = lax.broadcasted_iota(I32, (1, pad), 1)
    forced = (blk == 0) | (blk == cur) | (blk == cur - 1)
    b_idx = lax.broadcasted_iota(I32, (pad, pad), 0)
    j_idx = lax.broadcasted_iota(I32, (pad, pad), 1)
    slot = lax.broadcasted_iota(I32, (pad, V7X_LANES), 1).astype(F32)
    b_val = lax.broadcasted_iota(I32, (pad, V7X_LANES), 0).astype(F32)
    for g in range(B_KV_HEADS):
        kv_rows = slice(g * HEAD_DIM, (g + 1) * HEAD_DIM)
        kct = kvt_ref[0, 0, kv_rows, :].astype(F32)
        vct = kvt_ref[0, 1, kv_rows, :].astype(F32)
        p_sum = jnp.zeros((1, n_cmp_pad), F32)
        for r in range(B_REP):
            h = g * B_REP + r
            rows = slice(h * HEAD_DIM, (h + 1) * HEAD_DIM)
            s = _col_dot(kct, q[rows]) * SCALE + bias_ref[h:h + 1, :]
            p, _, den = _masked_softmax(s, mask_c)
            p = p / den
            p_sum = p_sum + p
            oc_ref[0, rows, :] = _row_sum(vct * p)
        hi, mid, lo3 = _split3(jnp.broadcast_to(p_sum, (B_STEP_ROWS, n_cmp_pad)))
        imp = (_dot(hi, ov_ref[...]) + _dot(mid, ov_ref[...]) + _dot(lo3, ov_ref[...]))[0:1]
        imp = jnp.where(forced, B_FORCE, jnp.where(blk <= cur, imp, NEG_INF))
        imp = jnp.where(blk < n_sel, imp, B_PAD_SCORE)
        other = jnp.broadcast_to(imp, (pad, pad))
        mine = jnp.transpose(other)
        ahead = (other > mine) | ((other == mine) & (j_idx < b_idx))
        rank = jnp.sum(jnp.where(ahead, 1.0, 0.0), axis=1, keepdims=True)
        ids = jnp.sum(jnp.where(rank == slot, b_val, 0.0), axis=0, keepdims=True)
        ids_ref[0, g:g + 1, :] = ids.astype(I32)


def _b_step_sel_kernel(ids_ref, pt_ref, q_ref, gate_ref, oc_ref, seln_ref, winn_ref, wbuf_ref, *rest,
                       n_past_blocks, blocks_per_page):
    sel_blocks = rest[:B_STEP_SLOTS * B_KV_HEADS]
    bblk_ref, bwin_ref, o_ref, m_ref, l_ref, acc_ref = rest[B_STEP_SLOTS * B_KV_HEADS:]
    n = pl.program_id(0)
    kb = pl.program_id(1)
    page = sel_blocks[0].shape[4]
    q = q_ref[0].astype(F32)
    seln = seln_ref[0]
    self_bias = bblk_ref[n_past_blocks]
    lane = lax.broadcasted_iota(I32, (1, page), 1)

    def layout(h):
        g = h // B_REP
        return (slice(h * HEAD_DIM, (h + 1) * HEAD_DIM), slice(g * HEAD_DIM, (g + 1) * HEAD_DIM),
                slice(B_KV_COLS + g * HEAD_DIM, B_KV_COLS + (g + 1) * HEAD_DIM))

    def self_score(h, new):
        rows, k_rows, _ = layout(h)
        return _col_dot(new[k_rows], q[rows]) * SCALE + self_bias[h:h + 1, 0:1]

    @pl.when(kb == 0)
    def _():
        for h in range(B_HEADS):
            _, _, v_rows = layout(h)
            m_ref[h] = jnp.broadcast_to(self_score(h, seln), (1, page))
            l_ref[h] = jnp.where(lane == 0, 1.0, 0.0)
            acc_ref[h] = jnp.where(lane == 0, jnp.broadcast_to(seln[v_rows], (HEAD_DIM, page)), 0.0)

    for slot in range(B_STEP_SLOTS):
        for g in range(B_KV_HEADS):
            blkid = ids_ref[(n * B_KV_HEADS + g) * V7X_LANES + kb * B_STEP_SLOTS + slot]
            mask = (blkid < n_past_blocks) & (lane // B_SEL_BLOCK == blkid % blocks_per_page)
            bias_blk = bblk_ref[jnp.minimum(blkid, n_past_blocks)]
            kt = sel_blocks[slot * B_KV_HEADS + g][0, 0, 0]
            vt = sel_blocks[slot * B_KV_HEADS + g][0, 1, 0]
            for r in range(B_REP):
                h = g * B_REP + r
                rows, _, _ = layout(h)
                s = jnp.where(mask, _col_dot(kt, q[rows]) * SCALE + bias_blk[h:h + 1, :], NEG_INF)
                m_old = m_ref[h]
                m_new = jnp.maximum(m_old, _row_max(s))
                p = jnp.where(mask, jnp.exp(s - m_new), 0.0)
                alpha = jnp.exp(m_old - m_new)
                l_ref[h] = alpha * l_ref[h] + p
                acc_ref[h] = alpha * acc_ref[h] + vt * p
                m_ref[h] = m_new

    @pl.when(kb == pl.num_programs(1) - 1)
    def _():
        gate = gate_ref[0]
        winn = winn_ref[0]
        wb = wbuf_ref.shape[4]
        wlane = lax.broadcasted_iota(I32, (1, wb), 1)
        wmask = (wb - wlane >= 0) & (wb - wlane < B_WINDOW)
        for h in range(B_HEADS):
            g = h // B_REP
            rows, _, v_rows = layout(h)
            s = jnp.where(wmask, _col_dot(wbuf_ref[0, 0, g], q[rows]) * SCALE + bwin_ref[h:h + 1, :], NEG_INF)
            s_self = self_score(h, winn)
            m = jnp.maximum(_row_max(s), s_self)
            pb = jnp.where(wmask, jnp.exp(s - m), 0.0)
            ps = jnp.exp(s_self - m)
            o_win = (_row_sum(wbuf_ref[0, 1, g] * pb) + ps * winn[v_rows]) / (_row_sum(pb) + ps)
            o_sel = _row_sum(acc_ref[h]) / _row_sum(l_ref[h])
            o_ref[0, rows, :] = (gate[:, h:h + 1] * oc_ref[0, rows, :] + gate[:, B_HEADS + h:B_HEADS + h + 1] * o_sel
                                 + gate[:, 2 * B_HEADS + h:2 * B_HEADS + h + 1] * o_win)


def _b_step_attn(q, gate, kvt, sel_new, win_new, cache_sel, cache_win, page_table, table):
    n, n_pages = page_table.shape
    page = cache_sel.shape[1]
    past = n_pages * page
    wb = cache_win.shape[1]
    assert wb == B_WINDOW and past % B_SEL_BLOCK == 0 and page % B_SEL_BLOCK == 0
    n_cmp_pad = past // B_CMP_STRIDE
    n_cmp = (past + 1) // B_CMP_STRIDE - (B_CMP_BLOCK // B_CMP_STRIDE) + 1
    n_sel = -(-(past + 1) // B_SEL_BLOCK)
    n_past_blocks = past // B_SEL_BLOCK
    blocks_per_page = page // B_SEL_BLOCK
    assert n_sel <= B_STEP_SEL_PAD and n_cmp <= n_cmp_pad and B_TOPN % B_STEP_SLOTS == 0

    bias_c = _bias_of_dist(table, np.maximum(past - (B_CMP_STRIDE * np.arange(n_cmp_pad) + B_CMP_BLOCK - 1), 0))
    ov = jnp.asarray(_overlap_t(B_STEP_SEL_PAD, n_cmp_pad, n_sel, n_cmp).T, BF16)
    col = lambda c: pl.BlockSpec((1, c, 1), lambda b: (b, 0, 0))
    q_col = q.reshape(n, B_Q_COLS, 1)
    oc, ids = pl.pallas_call(
        functools.partial(_b_step_cmp_kernel, n_cmp=n_cmp, n_sel=n_sel, cur=past // B_SEL_BLOCK), grid=(n,),
        in_specs=[col(B_Q_COLS), pl.BlockSpec((1, 2, B_KV_COLS, n_cmp_pad), lambda b: (b, 0, 0, 0)),
                  pl.BlockSpec(bias_c.shape, lambda b: (0, 0)), pl.BlockSpec(ov.shape, lambda b: (0, 0))],
        out_specs=[col(B_Q_COLS), pl.BlockSpec((1, B_KV_HEADS, V7X_LANES), lambda b: (b, 0, 0))],
        out_shape=[jax.ShapeDtypeStruct((n, B_Q_COLS, 1), F32),
                   jax.ShapeDtypeStruct((n, B_KV_HEADS, V7X_LANES), I32)],
        compiler_params=_params("arbitrary"), name="sparse_step_compressed")(q_col, kvt, bias_c, ov)

    blk_pos = B_SEL_BLOCK * np.arange(n_past_blocks + 1)[:, None] + np.arange(B_SEL_BLOCK)[None, :]
    bblk = _bias_of_dist(table, np.tile(np.maximum(past - blk_pos, 0), (1, blocks_per_page)), head_axis=1)
    bwin = _bias_of_dist(table, wb - np.arange(wb))
    scol = lambda c: pl.BlockSpec((1, c, 1), lambda b, kb, ids, pt: (b, 0, 0))

    def sel_spec(g, slot):
        def index(b, kb, ids, pt):
            blk = jnp.minimum(ids[(b * B_KV_HEADS + g) * V7X_LANES + kb * B_STEP_SLOTS + slot], n_past_blocks - 1)
            return (pt[b, blk // blocks_per_page], 0, g, 0, 0)
        return pl.BlockSpec((1, 2, 1, HEAD_DIM, page), index)

    sel_view = cache_sel.transpose(0, 2, 3, 4, 1)
    win_view = cache_win.transpose(0, 2, 3, 4, 1)
    o = pl.pallas_call(
        functools.partial(_b_step_sel_kernel, n_past_blocks=n_past_blocks, blocks_per_page=blocks_per_page),
        grid_spec=pltpu.PrefetchScalarGridSpec(
            num_scalar_prefetch=2, grid=(n, B_TOPN // B_STEP_SLOTS),
            in_specs=[scol(B_Q_COLS), pl.BlockSpec((1, 1, B_TAIL), lambda b, kb, ids, pt: (b, 0, 0)),
                      scol(B_Q_COLS), scol(B_SLAB), scol(B_SLAB),
                      pl.BlockSpec((1,) + win_view.shape[1:], lambda b, kb, ids, pt: (b, 0, 0, 0, 0))]
            + [sel_spec(g, slot) for slot in range(B_STEP_SLOTS) for g in range(B_KV_HEADS)]
            + [pl.BlockSpec(bblk.shape, lambda b, kb, ids, pt: (0, 0, 0)),
               pl.BlockSpec(bwin.shape, lambda b, kb, ids, pt: (0, 0))],
            out_specs=scol(B_Q_COLS),
            scratch_shapes=[pltpu.VMEM((B_HEADS, 1, page), F32), pltpu.VMEM((B_HEADS, 1, page), F32),
                            pltpu.VMEM((B_HEADS, HEAD_DIM, page), F32)]),
        out_shape=jax.ShapeDtypeStruct((n, B_Q_COLS, 1), F32),
        compiler_params=_params("arbitrary", "arbitrary"), name="sparse_step_selected")(
            ids.reshape(-1), page_table, q_col, gate.reshape(n, 1, B_TAIL), oc,
            sel_new.reshape(n, B_SLAB, 1), win_new.reshape(n, B_SLAB, 1), win_view,
            *([sel_view] * (B_STEP_SLOTS * B_KV_HEADS)), bblk, bwin)
    return o.reshape(n, B_Q_COLS)


def _mixer_b(xp, xs, g, w_in, gate_bias, phi_pos, phi_w1, phi_w2, w_out, table, caches, page_table, batch, seq):
    cache_cmp, cache_sel, cache_win = caches
    five = lambda a, rows: a.reshape(-1, rows, 2, B_KV_HEADS, HEAD_DIM)

    w, w_state_t, w_tail, bias = _b_grouped_weights(w_in, gate_bias)
    q, cmp, cmpt, selt, wint, selg, wing, gate = _b_proj_grouped(xp, g, w, w_state_t, w_tail, bias, batch, seq)
    kvt = _b_compress_prompt(cmp, _compress_weights(phi_pos, phi_w1, phi_w2, True), batch, seq)
    o = _b_attn(q, gate, kvt, selg, wing, table, batch, seq)
    w_out_grouped = jnp.pad(w_out.reshape(B_KV_HEADS, B_REP * HEAD_DIM, D_MODEL),
                            ((0, 0), (0, B_GROUP_Q - B_REP * HEAD_DIM), (0, 0))).reshape(B_GQ_COLS, D_MODEL)
    yp = _out_proj(xp, o, w_out_grouped.astype(BF16))
    keep = min(B_WINDOW, seq)
    tokens_first = lambda a: a.reshape(batch, 2, B_KV_HEADS, HEAD_DIM, a.shape[2]).transpose(0, 4, 1, 2, 3)
    state = [tokens_first(cmpt), tokens_first(selt), tokens_first(wint[:, :, seq - keep:])]

    n_main = B_Q_COLS + 3 * B_SLAB
    w = w_in[:, :n_main].astype(BF16)
    w_tail = jnp.pad(w_in[:, n_main:], ((0, 0), (0, B_TAIL - B_GATE_COLS))).astype(BF16)
    bias = jnp.pad(gate_bias, (0, B_TAIL - B_GATE_COLS)).reshape(1, B_TAIL)
    sq, scmp, ssel, swin, sgate = _b_proj(xs, g, w, w_tail, bias)
    skvt = _b_compress_step(cache_cmp, page_table, _compress_weights(phi_pos, phi_w1, phi_w2, False))
    so = _b_step_attn(sq, sgate, skvt, ssel, swin, cache_sel, cache_win, page_table, table)
    ys = _out_proj(xs, so, w_out.astype(BF16))
    wb = cache_win.shape[1]
    win_all = jnp.concatenate([cache_win, five(swin, 1)], axis=1)
    state += [five(scmp, 1), five(ssel, 1), win_all[:, wb + 1 - min(B_WINDOW, wb + 1):]]
    return yp, ys, state


def kernel(x_prompt, x_sample, cache_l0_w128, cache_l0_w512, cache_l0_w2048, cache_l1_cmp, cache_l1_sel, cache_l1_win, cache_l2_k, cache_l2_v, cache_l2_logf, cache_l3_w128, cache_l3_w512, cache_l3_w2048, page_table, norm_g, ffn_w_gate, ffn_w_up, ffn_w_down, final_norm_g, rel_bias_table, a_w_in, a_w_out, b_w_in, b_gate_bias, b_phi_pos, b_phi_w1, b_phi_w2, b_w_out, c_w_in, c_forget_bias, c_w_out):
    batch, seq, _ = x_prompt.shape
    depth = norm_g.shape[0]
    layer_caches = ((cache_l0_w128, cache_l0_w512, cache_l0_w2048), (cache_l1_cmp, cache_l1_sel, cache_l1_win),
                    (cache_l2_k, cache_l2_v, cache_l2_logf), (cache_l3_w128, cache_l3_w512, cache_l3_w2048))
    xp = x_prompt.reshape(batch * seq, D_MODEL)
    xs = x_sample.reshape(-1, D_MODEL)
    table = rel_bias_table
    new_state = []
    for i in range(depth):
        kind, j = i % N_MIXERS, i // N_MIXERS
        last = i == depth - 1
        f1 = (norm_g[i, 0], ffn_w_gate[i, 0].astype(BF16), ffn_w_up[i, 0].astype(BF16), ffn_w_down[i, 0].astype(BF16))
        f2 = (norm_g[i, 2], ffn_w_gate[i, 1].astype(BF16), ffn_w_up[i, 1].astype(BF16), ffn_w_down[i, 1].astype(BF16))
        xp, xs = _ffn(xp, *f1), _ffn(xs, *f1)
        if kind == 0:
            xp, xs, state = _mixer_a(xp, xs, norm_g[i, 1], a_w_in[j], a_w_out[j], table, layer_caches[i], batch, seq)
        elif kind == 1:
            xp, xs, state = _mixer_b(xp, xs, norm_g[i, 1], b_w_in[j], b_gate_bias[j], b_phi_pos[j], b_phi_w1[j],
                                     b_phi_w2[j], b_w_out[j], table, layer_caches[i], page_table, batch, seq)
        else:
            xp, xs, state = _mixer_c(xp, xs, norm_g[i, 1], c_w_in[j], c_forget_bias[j], c_w_out[j],
                                     layer_caches[i], page_table, batch, seq)
        new_state.extend(state)
        final_g = final_norm_g if last else None
        xp, xs = _ffn(xp, *f2, final_g), _ffn(xs, *f2, final_g)
    return (xp.reshape(batch, seq, D_MODEL), xs.reshape(-1, 1, D_MODEL), *new_state)
```
